```python
import math
import jax, jax.numpy as jnp
from jax import lax
import numpy as np

D_MODEL = 1024
BATCH = 8
SEQ = 4096
DEPTH = 4

MEM_LEN = 256
EPS = 1e-6
NEG_INF = -1e30

CONV_DIM = D_MODEL
CONV_KERNEL = 31

SSD_INNER = 2 * D_MODEL
SSD_HEAD_DIM = 64
SSD_HEADS = SSD_INNER // SSD_HEAD_DIM
SSD_GROUPS = 4
SSD_STATE = 128
SSD_CONV = 4
SSD_CHUNK = 128
SSD_XBC = SSD_INNER + 2 * SSD_GROUPS * SSD_STATE

ATTN_HEADS = 16
ATTN_KV_HEADS = 4
ATTN_HEAD_DIM = 64
ATTN_DIM = ATTN_HEADS * ATTN_HEAD_DIM
ATTN_KV_DIM = ATTN_KV_HEADS * ATTN_HEAD_DIM
ATTN_WINDOW = 128
ATTN_BLOCK = 128

REL_BUCKETS = 32
REL_MAX_DIST = 128

XATTN_HEADS = 4
XATTN_HEAD_DIM = D_MODEL // XATTN_HEADS

N_BRANCH = 3
MLP_HIDDEN = 4 * D_MODEL

OFF_CONV = 0
OFF_Z = OFF_CONV + 2 * CONV_DIM
OFF_XBC = OFF_Z + SSD_INNER
OFF_DT = OFF_XBC + SSD_XBC
OFF_Q = OFF_DT + SSD_HEADS
OFF_K = OFF_Q + ATTN_DIM
OFF_V = OFF_K + ATTN_KV_DIM
OFF_GATE = OFF_V + ATTN_KV_DIM
IN_COLS = OFF_GATE + N_BRANCH * D_MODEL

kernel_name = "hybrid_conv_ssd_swa_gated_trunk"


def rms_norm(x, g):
    x32 = x.astype(jnp.float32)
    y = x32 * lax.rsqrt(jnp.mean(x32 * x32, axis=-1, keepdims=True) + EPS)
    return (y * g.astype(jnp.float32)).astype(x.dtype)


def layer_norm(x, g, b):
    x32 = x.astype(jnp.float32)
    mu = jnp.mean(x32, axis=-1, keepdims=True)
    xc = x32 - mu
    y = xc * lax.rsqrt(jnp.mean(xc * xc, axis=-1, keepdims=True) + EPS)
    return (y * g.astype(jnp.float32) + b.astype(jnp.float32)).astype(x.dtype)


def causal_depthwise_conv(x, w, b):
    k, c = w.shape
    y = lax.conv_general_dilated(
        x, w[:, None, :].astype(x.dtype), window_strides=(1,), padding=[(k - 1, 0)],
        dimension_numbers=("NWC", "WIO", "NWC"), feature_group_count=c)
    return y + b.astype(x.dtype)


def conformer_conv_branch(u, dw_w, dw_b, ln_g, ln_b):
    a, gate = jnp.split(u, 2, axis=-1)
    h = a * jax.nn.sigmoid(gate)
    h = causal_depthwise_conv(h, dw_w, dw_b)
    h = layer_norm(h, ln_g, ln_b)
    return jax.nn.silu(h)


def segsum_exp(a):
    t = a.shape[-1]
    cs = jnp.cumsum(a, axis=-1)
    diff = cs[..., :, None] - cs[..., None, :]
    mask = jnp.tril(jnp.ones((t, t), dtype=bool))
    return jnp.where(mask, jnp.exp(jnp.where(mask, diff, 0.0)), 0.0)


def ssd_chunked(x, dt, a, bm, cm):
    b, l, h, p = x.shape
    g, n = bm.shape[-2:]
    r = h // g
    q = SSD_CHUNK
    nc = l // q
    xdt = (x * dt[..., None].astype(x.dtype)).reshape(b, nc, q, g, r, p)
    bc = bm.reshape(b, nc, q, g, n)
    cc = cm.reshape(b, nc, q, g, n)
    da = (dt * a).reshape(b, nc, q, g, r).transpose(0, 1, 3, 4, 2)
    cs = jnp.cumsum(da, axis=-1)
    decay = segsum_exp(da).astype(x.dtype)
    cb = jnp.einsum("bcqgn,bcsgn->bcgqs", cc, bc)
    scores = cb[:, :, :, None] * decay
    y_diag = jnp.einsum("bcgrqs,bcsgrp->bcqgrp", scores, xdt)
    decay_to_end = jnp.exp(cs[..., -1:] - cs).astype(x.dtype)
    chunk_states = jnp.einsum("bcqgn,bcgrq,bcqgrp->bcgrpn", bc, decay_to_end, xdt)
    chunk_decay = jnp.exp(cs[..., -1])

    def step(state, inp):
        s_c, d_c = inp
        return state * d_c[..., None, None] + s_c, state

    init = jnp.zeros((b, g, r, p, n), jnp.float32)
    _, states_in = lax.scan(
        step, init,
        (jnp.moveaxis(chunk_states.astype(jnp.float32), 1, 0), jnp.moveaxis(chunk_decay, 1, 0)))
    states_in = jnp.moveaxis(states_in, 0, 1).astype(x.dtype)
    decay_from_start = jnp.exp(cs).astype(x.dtype)
    y_off = jnp.einsum("bcqgn,bcgrpn,bcgrq->bcqgrp", cc, states_in, decay_from_start)
    return (y_diag + y_off).reshape(b, l, h, p)


def ssd_branch(z, xbc, dt_raw, conv_w, conv_b, dt_bias, a_log, d_skip, norm_g):
    b, l, _ = xbc.shape
    xbc = jax.nn.silu(causal_depthwise_conv(xbc, conv_w, conv_b))
    xs = xbc[..., :SSD_INNER].reshape(b, l, SSD_HEADS, SSD_HEAD_DIM)
    bm = xbc[..., SSD_INNER:SSD_INNER + SSD_GROUPS * SSD_STATE].reshape(b, l, SSD_GROUPS, SSD_STATE)
    cm = xbc[..., SSD_INNER + SSD_GROUPS * SSD_STATE:].reshape(b, l, SSD_GROUPS, SSD_STATE)
    dt = jax.nn.softplus(dt_raw.astype(jnp.float32) + dt_bias.astype(jnp.float32))
    a = -jnp.exp(a_log.astype(jnp.float32))
    y = ssd_chunked(xs, dt, a, bm, cm) + xs * d_skip[:, None].astype(xs.dtype)
    y = y.reshape(b, l, SSD_INNER) * jax.nn.silu(z)
    y = rms_norm(y.reshape(b, l, SSD_GROUPS, SSD_INNER // SSD_GROUPS),
                 norm_g.reshape(SSD_GROUPS, SSD_INNER // SSD_GROUPS))
    return y.reshape(b, l, SSD_INNER)


def t5_band_bias(rel_table):
    qi = jnp.arange(ATTN_BLOCK)[:, None] + ATTN_BLOCK
    kj = jnp.arange(2 * ATTN_BLOCK)[None, :]
    dist = qi - kj
    max_exact = REL_BUCKETS // 2
    d = jnp.maximum(dist, 1).astype(jnp.float32)
    large = max_exact + (jnp.log(d / max_exact) / math.log(REL_MAX_DIST / max_exact)
                         * (REL_BUCKETS - max_exact)).astype(jnp.int32)
    large = jnp.minimum(large, REL_BUCKETS - 1)
    bucket = jnp.where(dist < max_exact, jnp.maximum(dist, 0), large)
    bias = jnp.transpose(rel_table[bucket], (2, 0, 1)).astype(jnp.float32)
    return bias, dist


def swa_branch(q, k, v, q_g, k_g, sinks, rel_bias, band_dist):
    b, l, _ = q.shape
    nb = l // ATTN_BLOCK
    r = ATTN_HEADS // ATTN_KV_HEADS
    q = rms_norm(q.reshape(b, l, ATTN_HEADS, ATTN_HEAD_DIM), q_g)
    k = rms_norm(k.reshape(b, l, ATTN_KV_HEADS, ATTN_HEAD_DIM), k_g)
    v = v.reshape(b, l, ATTN_KV_HEADS, ATTN_HEAD_DIM)

    def band(t):
        tp = jnp.pad(t, ((0, 0), (ATTN_BLOCK, 0), (0, 0), (0, 0)))
        prev = tp[:, :l].reshape(b, nb, ATTN_BLOCK, ATTN_KV_HEADS, ATTN_HEAD_DIM)
        cur = t.reshape(b, nb, ATTN_BLOCK, ATTN_KV_HEADS, ATTN_HEAD_DIM)
        return jnp.concatenate([prev, cur], axis=2)

    kb, vb = band(k), band(v)
    qb = q.reshape(b, nb, ATTN_BLOCK, ATTN_KV_HEADS, r, ATTN_HEAD_DIM)
    logits = jnp.einsum("bnqgrd,bnkgd->bngrqk", qb, kb).astype(jnp.float32) * (ATTN_HEAD_DIM ** -0.5)
    logits = logits + rel_bias.reshape(ATTN_KV_HEADS, r, ATTN_BLOCK, 2 * ATTN_BLOCK)
    key_pos = (jnp.arange(nb)[:, None] * ATTN_BLOCK - ATTN_BLOCK
               + jnp.arange(2 * ATTN_BLOCK)[None, :])
    in_window = (band_dist >= 0) & (band_dist < ATTN_WINDOW)
    mask = in_window[None] & (key_pos >= 0)[:, None, :]
    logits = jnp.where(mask[None, :, None, None], logits, NEG_INF)
    sink = sinks.astype(jnp.float32).reshape(ATTN_KV_HEADS, r)[None, None, :, :, None, None]
    m = jnp.maximum(jnp.max(logits, axis=-1, keepdims=True), sink)
    pexp = jnp.exp(logits - m)
    probs = pexp / (jnp.sum(pexp, axis=-1, keepdims=True) + jnp.exp(sink - m))
    out = jnp.einsum("bngrqk,bnkgd->bnqgrd", probs.astype(v.dtype), vb)
    return out.reshape(b, l, ATTN_DIM)


def memory_cross_attention(h, mem_h, w_q, w_kv, q_g, k_g, w_o):
    b, l, _ = h.shape
    m = mem_h.shape[1]
    q = rms_norm((h @ w_q).reshape(b, l, XATTN_HEADS, XATTN_HEAD_DIM), q_g)
    kv = mem_h @ w_kv
    k = rms_norm(kv[..., :D_MODEL].reshape(b, m, XATTN_HEADS, XATTN_HEAD_DIM), k_g)
    v = kv[..., D_MODEL:].reshape(b, m, XATTN_HEADS, XATTN_HEAD_DIM)
    logits = jnp.einsum("bqhd,bkhd->bhqk", q, k).astype(jnp.float32) * (XATTN_HEAD_DIM ** -0.5)
    probs = jax.nn.softmax(logits, axis=-1)
    out = jnp.einsum("bhqk,bkhd->bqhd", probs.astype(v.dtype), v).reshape(b, l, D_MODEL)
    return out @ w_o


def _fwd_setup_inputs(seed: int = 0) -> dict:
    key = jax.random.key(seed)
    ks = iter(jax.random.split(key, 48))
    f32 = jnp.float32
    L = DEPTH

    def nrm(shape, scale):
        return jax.random.normal(next(ks), shape, f32) * scale

    def gain(shape):
        return 1.0 + nrm(shape, 0.02)

    out_scale = (2.0 * DEPTH) ** -0.5
    dt0 = jnp.exp(jax.random.uniform(next(ks), (L, SSD_HEADS), f32, math.log(1e-3), math.log(1e-1)))
    return {
        "x": nrm((BATCH, SEQ, D_MODEL), 1.0),
        "mem": nrm((BATCH, MEM_LEN, D_MODEL), 1.0),
        "rel_table": nrm((REL_BUCKETS, ATTN_HEADS), 0.1),
        "norm_mix": gain((L, D_MODEL)),
        "w_in": nrm((L, D_MODEL, IN_COLS), D_MODEL ** -0.5),
        "gate_bias": nrm((L, N_BRANCH, D_MODEL), 0.01),
        "conv_dw_w": nrm((L, CONV_KERNEL, CONV_DIM), CONV_KERNEL ** -0.5),
        "conv_dw_b": nrm((L, CONV_DIM), 0.01),
        "conv_ln_g": gain((L, CONV_DIM)),
        "conv_ln_b": nrm((L, CONV_DIM), 0.01),
        "w_conv_out": nrm((L, CONV_DIM, D_MODEL), CONV_DIM ** -0.5),
        "ssd_conv_w": nrm((L, SSD_CONV, SSD_XBC), SSD_CONV ** -0.5),
        "ssd_conv_b": nrm((L, SSD_XBC), 0.01),
        "ssd_dt_bias": dt0 + jnp.log(-jnp.expm1(-dt0)),
        "ssd_A_log": jnp.log(jax.random.uniform(next(ks), (L, SSD_HEADS), f32, 1.0, 16.0)),
        "ssd_D": gain((L, SSD_HEADS)),
        "ssd_norm_g": gain((L, SSD_INNER)),
        "w_ssd_out": nrm((L, SSD_INNER, D_MODEL), SSD_INNER ** -0.5),
        "attn_q_norm": gain((L, ATTN_HEAD_DIM)),
        "attn_k_norm": gain((L, ATTN_HEAD_DIM)),
        "attn_sinks": nrm((L, ATTN_HEADS), 0.5),
        "w_attn_out": nrm((L, ATTN_DIM, D_MODEL), ATTN_DIM ** -0.5),
        "w_mix_out": nrm((L, D_MODEL, D_MODEL), out_scale * D_MODEL ** -0.5),
        "norm_xattn": gain((L, D_MODEL)),
        "norm_mem": gain((L, D_MODEL)),
        "w_xq": nrm((L, D_MODEL, D_MODEL), D_MODEL ** -0.5),
        "w_xkv": nrm((L, D_MODEL, 2 * D_MODEL), D_MODEL ** -0.5),
        "xattn_q_norm": gain((L, XATTN_HEAD_DIM)),
        "xattn_k_norm": gain((L, XATTN_HEAD_DIM)),
        "w_xo": nrm((L, D_MODEL, D_MODEL), out_scale * D_MODEL ** -0.5),
        "norm_mlp": gain((L, D_MODEL)),
        "w_mlp_up": nrm((L, D_MODEL, MLP_HIDDEN), D_MODEL ** -0.5),
        "w_mlp_down": nrm((L, MLP_HIDDEN, D_MODEL), out_scale * MLP_HIDDEN ** -0.5),
    }


def _fwd_reference(x, mem, rel_table, norm_mix, w_in, gate_bias, conv_dw_w, conv_dw_b, conv_ln_g,
              conv_ln_b, w_conv_out, ssd_conv_w, ssd_conv_b, ssd_dt_bias, ssd_A_log, ssd_D,
              ssd_norm_g, w_ssd_out, attn_q_norm, attn_k_norm, attn_sinks, w_attn_out, w_mix_out,
              norm_xattn, norm_mem, w_xq, w_xkv, xattn_q_norm, xattn_k_norm, w_xo, norm_mlp,
              w_mlp_up, w_mlp_down):
    b, l, _ = x.shape
    rel_bias, band_dist = t5_band_bias(rel_table)
    h = x
    for i in range(DEPTH):
        u = rms_norm(h, norm_mix[i])
        proj = u @ w_in[i]
        y_a = conformer_conv_branch(proj[..., OFF_CONV:OFF_Z], conv_dw_w[i], conv_dw_b[i],
                                    conv_ln_g[i], conv_ln_b[i]) @ w_conv_out[i]
        y_b = ssd_branch(proj[..., OFF_Z:OFF_XBC], proj[..., OFF_XBC:OFF_DT], proj[..., OFF_DT:OFF_Q],
                         ssd_conv_w[i], ssd_conv_b[i], ssd_dt_bias[i], ssd_A_log[i], ssd_D[i],
                         ssd_norm_g[i]) @ w_ssd_out[i]
        y_c = swa_branch(proj[..., OFF_Q:OFF_K], proj[..., OFF_K:OFF_V], proj[..., OFF_V:OFF_GATE],
                         attn_q_norm[i], attn_k_norm[i], attn_sinks[i], rel_bias, band_dist) @ w_attn_out[i]
        gates = jax.nn.sigmoid(proj[..., OFF_GATE:IN_COLS].reshape(b, l, N_BRANCH, D_MODEL)
                               + gate_bias[i].astype(proj.dtype))
        merged = gates[..., 0, :] * y_a + gates[..., 1, :] * y_b + gates[..., 2, :] * y_c
        h = h + merged @ w_mix_out[i]
        h = h + memory_cross_attention(rms_norm(h, norm_xattn[i]), rms_norm(mem, norm_mem[i]),
                                       w_xq[i], w_xkv[i], xattn_q_norm[i], xattn_k_norm[i], w_xo[i])
        u = rms_norm(h, norm_mlp[i])
        h = h + jnp.square(jax.nn.relu(u @ w_mlp_up[i])) @ w_mlp_down[i]
    return h


import jax as _jax
import jax.numpy as _jnp

TWIN_FORMAT = 'train_step'
FWD_PARAMS = ['x', 'mem', 'rel_table', 'norm_mix', 'w_in', 'gate_bias', 'conv_dw_w', 'conv_dw_b', 'conv_ln_g', 'conv_ln_b', 'w_conv_out', 'ssd_conv_w', 'ssd_conv_b', 'ssd_dt_bias', 'ssd_A_log', 'ssd_D', 'ssd_norm_g', 'w_ssd_out', 'attn_q_norm', 'attn_k_norm', 'attn_sinks', 'w_attn_out', 'w_mix_out', 'norm_xattn', 'norm_mem', 'w_xq', 'w_xkv', 'xattn_q_norm', 'xattn_k_norm', 'w_xo', 'norm_mlp', 'w_mlp_up', 'w_mlp_down']
TWIN_WEIGHTS = ['rel_table', 'norm_mix', 'w_in', 'gate_bias', 'conv_dw_w', 'conv_dw_b', 'conv_ln_g', 'conv_ln_b', 'w_conv_out', 'ssd_conv_w', 'ssd_conv_b', 'ssd_dt_bias', 'ssd_A_log', 'ssd_D', 'ssd_norm_g', 'w_ssd_out', 'attn_q_norm', 'attn_k_norm', 'attn_sinks', 'w_attn_out', 'w_mix_out', 'norm_xattn', 'norm_mem', 'w_xq', 'w_xkv', 'xattn_q_norm', 'xattn_k_norm', 'w_xo', 'norm_mlp', 'w_mlp_up', 'w_mlp_down']
TWIN_DIFF_INPUT = 'x'
TWIN_INPUTS = ['x', 'mem', 'rel_table', 'norm_mix', 'w_in', 'gate_bias', 'conv_dw_w', 'conv_dw_b', 'conv_ln_g', 'conv_ln_b', 'w_conv_out', 'ssd_conv_w', 'ssd_conv_b', 'ssd_dt_bias', 'ssd_A_log', 'ssd_D', 'ssd_norm_g', 'w_ssd_out', 'attn_q_norm', 'attn_k_norm', 'attn_sinks', 'w_attn_out', 'w_mix_out', 'norm_xattn', 'norm_mem', 'w_xq', 'w_xkv', 'xattn_q_norm', 'xattn_k_norm', 'w_xo', 'norm_mlp', 'w_mlp_up', 'w_mlp_down', 'loss_target', 'm_rel_table', 'm_norm_mix', 'm_w_in', 'm_gate_bias', 'm_conv_dw_w', 'm_conv_dw_b', 'm_conv_ln_g', 'm_conv_ln_b', 'm_w_conv_out', 'm_ssd_conv_w', 'm_ssd_conv_b', 'm_ssd_dt_bias', 'm_ssd_A_log', 'm_ssd_D', 'm_ssd_norm_g', 'm_w_ssd_out', 'm_attn_q_norm', 'm_attn_k_norm', 'm_attn_sinks', 'm_w_attn_out', 'm_w_mix_out', 'm_norm_xattn', 'm_norm_mem', 'm_w_xq', 'm_w_xkv', 'm_xattn_q_norm', 'm_xattn_k_norm', 'm_w_xo', 'm_norm_mlp', 'm_w_mlp_up', 'm_w_mlp_down', 'v_rel_table', 'v_norm_mix', 'v_w_in', 'v_gate_bias', 'v_conv_dw_w', 'v_conv_dw_b', 'v_conv_ln_g', 'v_conv_ln_b', 'v_w_conv_out', 'v_ssd_conv_w', 'v_ssd_conv_b', 'v_ssd_dt_bias', 'v_ssd_A_log', 'v_ssd_D', 'v_ssd_norm_g', 'v_w_ssd_out', 'v_attn_q_norm', 'v_attn_k_norm', 'v_attn_sinks', 'v_w_attn_out', 'v_w_mix_out', 'v_norm_xattn', 'v_norm_mem', 'v_w_xq', 'v_w_xkv', 'v_xattn_q_norm', 'v_xattn_k_norm', 'v_w_xo', 'v_norm_mlp', 'v_w_mlp_up', 'v_w_mlp_down']
TWIN_OUTPUTS = ['loss', 'grad_x', 'grad_rel_table', 'grad_norm_mix', 'grad_w_in', 'grad_gate_bias', 'grad_conv_dw_w', 'grad_conv_dw_b', 'grad_conv_ln_g', 'grad_conv_ln_b', 'grad_w_conv_out', 'grad_ssd_conv_w', 'grad_ssd_conv_b', 'grad_ssd_dt_bias', 'grad_ssd_A_log', 'grad_ssd_D', 'grad_ssd_norm_g', 'grad_w_ssd_out', 'grad_attn_q_norm', 'grad_attn_k_norm', 'grad_attn_sinks', 'grad_w_attn_out', 'grad_w_mix_out', 'grad_norm_xattn', 'grad_norm_mem', 'grad_w_xq', 'grad_w_xkv', 'grad_xattn_q_norm', 'grad_xattn_k_norm', 'grad_w_xo', 'grad_norm_mlp', 'grad_w_mlp_up', 'grad_w_mlp_down', 'delta_rel_table', 'delta_norm_mix', 'delta_w_in', 'delta_gate_bias', 'delta_conv_dw_w', 'delta_conv_dw_b', 'delta_conv_ln_g', 'delta_conv_ln_b', 'delta_w_conv_out', 'delta_ssd_conv_w', 'delta_ssd_conv_b', 'delta_ssd_dt_bias', 'delta_ssd_A_log', 'delta_ssd_D', 'delta_ssd_norm_g', 'delta_w_ssd_out', 'delta_attn_q_norm', 'delta_attn_k_norm', 'delta_attn_sinks', 'delta_w_attn_out', 'delta_w_mix_out', 'delta_norm_xattn', 'delta_norm_mem', 'delta_w_xq', 'delta_w_xkv', 'delta_xattn_q_norm', 'delta_xattn_k_norm', 'delta_w_xo', 'delta_norm_mlp', 'delta_w_mlp_up', 'delta_w_mlp_down', 'new_m_rel_table', 'new_m_norm_mix', 'new_m_w_in', 'new_m_gate_bias', 'new_m_conv_dw_w', 'new_m_conv_dw_b', 'new_m_conv_ln_g', 'new_m_conv_ln_b', 'new_m_w_conv_out', 'new_m_ssd_conv_w', 'new_m_ssd_conv_b', 'new_m_ssd_dt_bias', 'new_m_ssd_A_log', 'new_m_ssd_D', 'new_m_ssd_norm_g', 'new_m_w_ssd_out', 'new_m_attn_q_norm', 'new_m_attn_k_norm', 'new_m_attn_sinks', 'new_m_w_attn_out', 'new_m_w_mix_out', 'new_m_norm_xattn', 'new_m_norm_mem', 'new_m_w_xq', 'new_m_w_xkv', 'new_m_xattn_q_norm', 'new_m_xattn_k_norm', 'new_m_w_xo', 'new_m_norm_mlp', 'new_m_w_mlp_up', 'new_m_w_mlp_down', 'new_v_rel_table', 'new_v_norm_mix', 'new_v_w_in', 'new_v_gate_bias', 'new_v_conv_dw_w', 'new_v_conv_dw_b', 'new_v_conv_ln_g', 'new_v_conv_ln_b', 'new_v_w_conv_out', 'new_v_ssd_conv_w', 'new_v_ssd_conv_b', 'new_v_ssd_dt_bias', 'new_v_ssd_A_log', 'new_v_ssd_D', 'new_v_ssd_norm_g', 'new_v_w_ssd_out', 'new_v_attn_q_norm', 'new_v_attn_k_norm', 'new_v_attn_sinks', 'new_v_w_attn_out', 'new_v_w_mix_out', 'new_v_norm_xattn', 'new_v_norm_mem', 'new_v_w_xq', 'new_v_w_xkv', 'new_v_xattn_q_norm', 'new_v_xattn_k_norm', 'new_v_w_xo', 'new_v_norm_mlp', 'new_v_w_mlp_up', 'new_v_w_mlp_down']
TWIN_LEAF_KINDS = {'loss': 'loss', 'grad_x': 'grad_x', 'grad_rel_table': 'grad_w', 'grad_norm_mix': 'grad_w', 'grad_w_in': 'grad_w', 'grad_gate_bias': 'grad_w', 'grad_conv_dw_w': 'grad_w', 'grad_conv_dw_b': 'grad_w', 'grad_conv_ln_g': 'grad_w', 'grad_conv_ln_b': 'grad_w', 'grad_w_conv_out': 'grad_w', 'grad_ssd_conv_w': 'grad_w', 'grad_ssd_conv_b': 'grad_w', 'grad_ssd_dt_bias': 'grad_w', 'grad_ssd_A_log': 'grad_w', 'grad_ssd_D': 'grad_w', 'grad_ssd_norm_g': 'grad_w', 'grad_w_ssd_out': 'grad_w', 'grad_attn_q_norm': 'grad_w', 'grad_attn_k_norm': 'grad_w', 'grad_attn_sinks': 'grad_w', 'grad_w_attn_out': 'grad_w', 'grad_w_mix_out': 'grad_w', 'grad_norm_xattn': 'grad_w', 'grad_norm_mem': 'grad_w', 'grad_w_xq': 'grad_w', 'grad_w_xkv': 'grad_w', 'grad_xattn_q_norm': 'grad_w', 'grad_xattn_k_norm': 'grad_w', 'grad_w_xo': 'grad_w', 'grad_norm_mlp': 'grad_w', 'grad_w_mlp_up': 'grad_w', 'grad_w_mlp_down': 'grad_w', 'delta_rel_table': 'delta_w', 'delta_norm_mix': 'delta_w', 'delta_w_in': 'delta_w', 'delta_gate_bias': 'delta_w', 'delta_conv_dw_w': 'delta_w', 'delta_conv_dw_b': 'delta_w', 'delta_conv_ln_g': 'delta_w', 'delta_conv_ln_b': 'delta_w', 'delta_w_conv_out': 'delta_w', 'delta_ssd_conv_w': 'delta_w', 'delta_ssd_conv_b': 'delta_w', 'delta_ssd_dt_bias': 'delta_w', 'delta_ssd_A_log': 'delta_w', 'delta_ssd_D': 'delta_w', 'delta_ssd_norm_g': 'delta_w', 'delta_w_ssd_out': 'delta_w', 'delta_attn_q_norm': 'delta_w', 'delta_attn_k_norm': 'delta_w', 'delta_attn_sinks': 'delta_w', 'delta_w_attn_out': 'delta_w', 'delta_w_mix_out': 'delta_w', 'delta_norm_xattn': 'delta_w', 'delta_norm_mem': 'delta_w', 'delta_w_xq': 'delta_w', 'delta_w_xkv': 'delta_w', 'delta_xattn_q_norm': 'delta_w', 'delta_xattn_k_norm': 'delta_w', 'delta_w_xo': 'delta_w', 'delta_norm_mlp': 'delta_w', 'delta_w_mlp_up': 'delta_w', 'delta_w_mlp_down': 'delta_w', 'new_m_rel_table': 'new_m', 'new_m_norm_mix': 'new_m', 'new_m_w_in': 'new_m', 'new_m_gate_bias': 'new_m', 'new_m_conv_dw_w': 'new_m', 'new_m_conv_dw_b': 'new_m', 'new_m_conv_ln_g': 'new_m', 'new_m_conv_ln_b': 'new_m', 'new_m_w_conv_out': 'new_m', 'new_m_ssd_conv_w': 'new_m', 'new_m_ssd_conv_b': 'new_m', 'new_m_ssd_dt_bias': 'new_m', 'new_m_ssd_A_log': 'new_m', 'new_m_ssd_D': 'new_m', 'new_m_ssd_norm_g': 'new_m', 'new_m_w_ssd_out': 'new_m', 'new_m_attn_q_norm': 'new_m', 'new_m_attn_k_norm': 'new_m', 'new_m_attn_sinks': 'new_m', 'new_m_w_attn_out': 'new_m', 'new_m_w_mix_out': 'new_m', 'new_m_norm_xattn': 'new_m', 'new_m_norm_mem': 'new_m', 'new_m_w_xq': 'new_m', 'new_m_w_xkv': 'new_m', 'new_m_xattn_q_norm': 'new_m', 'new_m_xattn_k_norm': 'new_m', 'new_m_w_xo': 'new_m', 'new_m_norm_mlp': 'new_m', 'new_m_w_mlp_up': 'new_m', 'new_m_w_mlp_down': 'new_m', 'new_v_rel_table': 'new_v', 'new_v_norm_mix': 'new_v', 'new_v_w_in': 'new_v', 'new_v_gate_bias': 'new_v', 'new_v_conv_dw_w': 'new_v', 'new_v_conv_dw_b': 'new_v', 'new_v_conv_ln_g': 'new_v', 'new_v_conv_ln_b': 'new_v', 'new_v_w_conv_out': 'new_v', 'new_v_ssd_conv_w': 'new_v', 'new_v_ssd_conv_b': 'new_v', 'new_v_ssd_dt_bias': 'new_v', 'new_v_ssd_A_log': 'new_v', 'new_v_ssd_D': 'new_v', 'new_v_ssd_norm_g': 'new_v', 'new_v_w_ssd_out': 'new_v', 'new_v_attn_q_norm': 'new_v', 'new_v_attn_k_norm': 'new_v', 'new_v_attn_sinks': 'new_v', 'new_v_w_attn_out': 'new_v', 'new_v_w_mix_out': 'new_v', 'new_v_norm_xattn': 'new_v', 'new_v_norm_mem': 'new_v', 'new_v_w_xq': 'new_v', 'new_v_w_xkv': 'new_v', 'new_v_xattn_q_norm': 'new_v', 'new_v_xattn_k_norm': 'new_v', 'new_v_w_xo': 'new_v', 'new_v_norm_mlp': 'new_v', 'new_v_w_mlp_up': 'new_v', 'new_v_w_mlp_down': 'new_v'}


def _forward(args):
    return _fwd_reference(*[args[k] for k in FWD_PARAMS])


def _output_shape():
    out = _jax.eval_shape(lambda: _forward(_fwd_setup_inputs(0)))
    return out.shape, out.dtype

N_MICROBATCH = 1
ADAM_LR = 0.001
ADAM_B1 = 0.9
ADAM_B2 = 0.999
ADAM_EPS = 1e-08
ADAM_WD = 0.01
ADAM_STEP = 10
PER_EXAMPLE_BATCH_AXIS = {'x': 0, 'mem': 0, 'loss_target': 0}
SHARED_INPUTS = []
_WEIGHT_DTYPES = {'rel_table': _jnp.float32, 'norm_mix': _jnp.float32, 'w_in': _jnp.float32, 'gate_bias': _jnp.float32, 'conv_dw_w': _jnp.float32, 'conv_dw_b': _jnp.float32, 'conv_ln_g': _jnp.float32, 'conv_ln_b': _jnp.float32, 'w_conv_out': _jnp.float32, 'ssd_conv_w': _jnp.float32, 'ssd_conv_b': _jnp.float32, 'ssd_dt_bias': _jnp.float32, 'ssd_A_log': _jnp.float32, 'ssd_D': _jnp.float32, 'ssd_norm_g': _jnp.float32, 'w_ssd_out': _jnp.float32, 'attn_q_norm': _jnp.float32, 'attn_k_norm': _jnp.float32, 'attn_sinks': _jnp.float32, 'w_attn_out': _jnp.float32, 'w_mix_out': _jnp.float32, 'norm_xattn': _jnp.float32, 'norm_mem': _jnp.float32, 'w_xq': _jnp.float32, 'w_xkv': _jnp.float32, 'xattn_q_norm': _jnp.float32, 'xattn_k_norm': _jnp.float32, 'w_xo': _jnp.float32, 'norm_mlp': _jnp.float32, 'w_mlp_up': _jnp.float32, 'w_mlp_down': _jnp.float32}
MOMENT_SCALE = {'rel_table': 4.046060e-02, 'norm_mix': 5.675118e-01, 'w_in': 1.464708e-01, 'gate_bias': 3.390453e-01, 'conv_dw_w': 2.455394e-01, 'conv_dw_b': 2.381716e+00, 'conv_ln_g': 1.106271e+00, 'conv_ln_b': 1.436458e+00, 'w_conv_out': 6.007732e-01, 'ssd_conv_w': 2.022414e-01, 'ssd_conv_b': 6.813952e-01, 'ssd_dt_bias': 2.127686e-01, 'ssd_A_log': 1.169684e+00, 'ssd_D': 1.262410e+00, 'ssd_norm_g': 1.004422e+00, 'w_ssd_out': 5.262312e-01, 'attn_q_norm': 2.050647e-01, 'attn_k_norm': 2.064753e-01, 'attn_sinks': 3.615476e-02, 'w_attn_out': 5.081303e-01, 'w_mix_out': 2.665347e+00, 'norm_xattn': 3.654836e-02, 'norm_mem': 1.776119e-01, 'w_xq': 3.587970e-02, 'w_xkv': 1.150051e-01, 'xattn_q_norm': 1.566715e-01, 'xattn_k_norm': 1.570429e-01, 'w_xo': 4.730048e-01, 'norm_mlp': 1.227166e+01, 'w_mlp_up': 4.730457e-01, 'w_mlp_down': 7.500426e+00}


def _to_microbatches(a, axis):
    t = _jnp.moveaxis(a, axis, 0)
    t = t.reshape((N_MICROBATCH, t.shape[0] // N_MICROBATCH) + t.shape[1:])
    return _jnp.moveaxis(t, 1, axis + 1)


def setup_inputs(seed: int = 0) -> dict:
    inp = _fwd_setup_inputs(seed)
    key = _jax.random.fold_in(_jax.random.key(seed), 7919)
    shape, _ = _output_shape()
    out = dict(inp)
    out["loss_target"] = _jax.random.normal(_jax.random.fold_in(key, 0), shape, _jnp.float32)
    for i, name in enumerate(TWIN_WEIGHTS):
        w = inp[name].astype(_jnp.float32)
        if MOMENT_SCALE is None:
            s = _jnp.sqrt(_jnp.mean(_jnp.square(w)) + 1e-30)
        else:
            s = MOMENT_SCALE[name]
        km, kv = _jax.random.split(_jax.random.fold_in(key, i + 1))
        out[name] = w
        out["m_" + name] = s * _jax.random.normal(km, w.shape, _jnp.float32)
        out["v_" + name] = (s * s) * _jax.random.uniform(kv, w.shape, _jnp.float32, 0.5, 1.5)
    if N_MICROBATCH > 1:
        for name, axis in PER_EXAMPLE_BATCH_AXIS.items():
            out[name] = _to_microbatches(out[name], axis)
    return {'x': out['x'], 'mem': out['mem'], 'rel_table': out['rel_table'], 'norm_mix': out['norm_mix'], 'w_in': out['w_in'], 'gate_bias': out['gate_bias'], 'conv_dw_w': out['conv_dw_w'], 'conv_dw_b': out['conv_dw_b'], 'conv_ln_g': out['conv_ln_g'], 'conv_ln_b': out['conv_ln_b'], 'w_conv_out': out['w_conv_out'], 'ssd_conv_w': out['ssd_conv_w'], 'ssd_conv_b': out['ssd_conv_b'], 'ssd_dt_bias': out['ssd_dt_bias'], 'ssd_A_log': out['ssd_A_log'], 'ssd_D': out['ssd_D'], 'ssd_norm_g': out['ssd_norm_g'], 'w_ssd_out': out['w_ssd_out'], 'attn_q_norm': out['attn_q_norm'], 'attn_k_norm': out['attn_k_norm'], 'attn_sinks': out['attn_sinks'], 'w_attn_out': out['w_attn_out'], 'w_mix_out': out['w_mix_out'], 'norm_xattn': out['norm_xattn'], 'norm_mem': out['norm_mem'], 'w_xq': out['w_xq'], 'w_xkv': out['w_xkv'], 'xattn_q_norm': out['xattn_q_norm'], 'xattn_k_norm': out['xattn_k_norm'], 'w_xo': out['w_xo'], 'norm_mlp': out['norm_mlp'], 'w_mlp_up': out['w_mlp_up'], 'w_mlp_down': out['w_mlp_down'], 'loss_target': out['loss_target'], 'm_rel_table': out['m_rel_table'], 'm_norm_mix': out['m_norm_mix'], 'm_w_in': out['m_w_in'], 'm_gate_bias': out['m_gate_bias'], 'm_conv_dw_w': out['m_conv_dw_w'], 'm_conv_dw_b': out['m_conv_dw_b'], 'm_conv_ln_g': out['m_conv_ln_g'], 'm_conv_ln_b': out['m_conv_ln_b'], 'm_w_conv_out': out['m_w_conv_out'], 'm_ssd_conv_w': out['m_ssd_conv_w'], 'm_ssd_conv_b': out['m_ssd_conv_b'], 'm_ssd_dt_bias': out['m_ssd_dt_bias'], 'm_ssd_A_log': out['m_ssd_A_log'], 'm_ssd_D': out['m_ssd_D'], 'm_ssd_norm_g': out['m_ssd_norm_g'], 'm_w_ssd_out': out['m_w_ssd_out'], 'm_attn_q_norm': out['m_attn_q_norm'], 'm_attn_k_norm': out['m_attn_k_norm'], 'm_attn_sinks': out['m_attn_sinks'], 'm_w_attn_out': out['m_w_attn_out'], 'm_w_mix_out': out['m_w_mix_out'], 'm_norm_xattn': out['m_norm_xattn'], 'm_norm_mem': out['m_norm_mem'], 'm_w_xq': out['m_w_xq'], 'm_w_xkv': out['m_w_xkv'], 'm_xattn_q_norm': out['m_xattn_q_norm'], 'm_xattn_k_norm': out['m_xattn_k_norm'], 'm_w_xo': out['m_w_xo'], 'm_norm_mlp': out['m_norm_mlp'], 'm_w_mlp_up': out['m_w_mlp_up'], 'm_w_mlp_down': out['m_w_mlp_down'], 'v_rel_table': out['v_rel_table'], 'v_norm_mix': out['v_norm_mix'], 'v_w_in': out['v_w_in'], 'v_gate_bias': out['v_gate_bias'], 'v_conv_dw_w': out['v_conv_dw_w'], 'v_conv_dw_b': out['v_conv_dw_b'], 'v_conv_ln_g': out['v_conv_ln_g'], 'v_conv_ln_b': out['v_conv_ln_b'], 'v_w_conv_out': out['v_w_conv_out'], 'v_ssd_conv_w': out['v_ssd_conv_w'], 'v_ssd_conv_b': out['v_ssd_conv_b'], 'v_ssd_dt_bias': out['v_ssd_dt_bias'], 'v_ssd_A_log': out['v_ssd_A_log'], 'v_ssd_D': out['v_ssd_D'], 'v_ssd_norm_g': out['v_ssd_norm_g'], 'v_w_ssd_out': out['v_w_ssd_out'], 'v_attn_q_norm': out['v_attn_q_norm'], 'v_attn_k_norm': out['v_attn_k_norm'], 'v_attn_sinks': out['v_attn_sinks'], 'v_w_attn_out': out['v_w_attn_out'], 'v_w_mix_out': out['v_w_mix_out'], 'v_norm_xattn': out['v_norm_xattn'], 'v_norm_mem': out['v_norm_mem'], 'v_w_xq': out['v_w_xq'], 'v_w_xkv': out['v_w_xkv'], 'v_xattn_q_norm': out['v_xattn_q_norm'], 'v_xattn_k_norm': out['v_xattn_k_norm'], 'v_w_xo': out['v_w_xo'], 'v_norm_mlp': out['v_norm_mlp'], 'v_w_mlp_up': out['v_w_mlp_up'], 'v_w_mlp_down': out['v_w_mlp_down']}


def _loss(weights, diff, rest, loss_target):
    with _jax.named_scope("forward"):
        args = {**rest, TWIN_DIFF_INPUT: diff, **{k: w.astype(_WEIGHT_DTYPES[k]) for k, w in weights.items()}}
        y = _forward(args)
    with _jax.named_scope("loss_head"):
        err = _jnp.square(y.astype(_jnp.float32) - loss_target)
        return 0.5 * _jnp.sum(_jnp.mean(err, axis=-1)) if err.ndim else 0.5 * err


def _adamw(w, g, m, v):
    m = ADAM_B1 * m + (1.0 - ADAM_B1) * g
    v = ADAM_B2 * v + (1.0 - ADAM_B2) * _jnp.square(g)
    m_hat = m / (1.0 - ADAM_B1 ** ADAM_STEP)
    v_hat = v / (1.0 - ADAM_B2 ** ADAM_STEP)
    delta = -ADAM_LR * (m_hat / (_jnp.sqrt(v_hat) + ADAM_EPS) + ADAM_WD * w)
    return delta, m, v


def reference(x, mem, rel_table, norm_mix, w_in, gate_bias, conv_dw_w, conv_dw_b, conv_ln_g, conv_ln_b, w_conv_out, ssd_conv_w, ssd_conv_b, ssd_dt_bias, ssd_A_log, ssd_D, ssd_norm_g, w_ssd_out, attn_q_norm, attn_k_norm, attn_sinks, w_attn_out, w_mix_out, norm_xattn, norm_mem, w_xq, w_xkv, xattn_q_norm, xattn_k_norm, w_xo, norm_mlp, w_mlp_up, w_mlp_down, loss_target, m_rel_table, m_norm_mix, m_w_in, m_gate_bias, m_conv_dw_w, m_conv_dw_b, m_conv_ln_g, m_conv_ln_b, m_w_conv_out, m_ssd_conv_w, m_ssd_conv_b, m_ssd_dt_bias, m_ssd_A_log, m_ssd_D, m_ssd_norm_g, m_w_ssd_out, m_attn_q_norm, m_attn_k_norm, m_attn_sinks, m_w_attn_out, m_w_mix_out, m_norm_xattn, m_norm_mem, m_w_xq, m_w_xkv, m_xattn_q_norm, m_xattn_k_norm, m_w_xo, m_norm_mlp, m_w_mlp_up, m_w_mlp_down, v_rel_table, v_norm_mix, v_w_in, v_gate_bias, v_conv_dw_w, v_conv_dw_b, v_conv_ln_g, v_conv_ln_b, v_w_conv_out, v_ssd_conv_w, v_ssd_conv_b, v_ssd_dt_bias, v_ssd_A_log, v_ssd_D, v_ssd_norm_g, v_w_ssd_out, v_attn_q_norm, v_attn_k_norm, v_attn_sinks, v_w_attn_out, v_w_mix_out, v_norm_xattn, v_norm_mem, v_w_xq, v_w_xkv, v_xattn_q_norm, v_xattn_k_norm, v_w_xo, v_norm_mlp, v_w_mlp_up, v_w_mlp_down):
    given = dict(x=x, mem=mem, rel_table=rel_table, norm_mix=norm_mix, w_in=w_in, gate_bias=gate_bias, conv_dw_w=conv_dw_w, conv_dw_b=conv_dw_b, conv_ln_g=conv_ln_g, conv_ln_b=conv_ln_b, w_conv_out=w_conv_out, ssd_conv_w=ssd_conv_w, ssd_conv_b=ssd_conv_b, ssd_dt_bias=ssd_dt_bias, ssd_A_log=ssd_A_log, ssd_D=ssd_D, ssd_norm_g=ssd_norm_g, w_ssd_out=w_ssd_out, attn_q_norm=attn_q_norm, attn_k_norm=attn_k_norm, attn_sinks=attn_sinks, w_attn_out=w_attn_out, w_mix_out=w_mix_out, norm_xattn=norm_xattn, norm_mem=norm_mem, w_xq=w_xq, w_xkv=w_xkv, xattn_q_norm=xattn_q_norm, xattn_k_norm=xattn_k_norm, w_xo=w_xo, norm_mlp=norm_mlp, w_mlp_up=w_mlp_up, w_mlp_down=w_mlp_down, loss_target=loss_target, m_rel_table=m_rel_table, m_norm_mix=m_norm_mix, m_w_in=m_w_in, m_gate_bias=m_gate_bias, m_conv_dw_w=m_conv_dw_w, m_conv_dw_b=m_conv_dw_b, m_conv_ln_g=m_conv_ln_g, m_conv_ln_b=m_conv_ln_b, m_w_conv_out=m_w_conv_out, m_ssd_conv_w=m_ssd_conv_w, m_ssd_conv_b=m_ssd_conv_b, m_ssd_dt_bias=m_ssd_dt_bias, m_ssd_A_log=m_ssd_A_log, m_ssd_D=m_ssd_D, m_ssd_norm_g=m_ssd_norm_g, m_w_ssd_out=m_w_ssd_out, m_attn_q_norm=m_attn_q_norm, m_attn_k_norm=m_attn_k_norm, m_attn_sinks=m_attn_sinks, m_w_attn_out=m_w_attn_out, m_w_mix_out=m_w_mix_out, m_norm_xattn=m_norm_xattn, m_norm_mem=m_norm_mem, m_w_xq=m_w_xq, m_w_xkv=m_w_xkv, m_xattn_q_norm=m_xattn_q_norm, m_xattn_k_norm=m_xattn_k_norm, m_w_xo=m_w_xo, m_norm_mlp=m_norm_mlp, m_w_mlp_up=m_w_mlp_up, m_w_mlp_down=m_w_mlp_down, v_rel_table=v_rel_table, v_norm_mix=v_norm_mix, v_w_in=v_w_in, v_gate_bias=v_gate_bias, v_conv_dw_w=v_conv_dw_w, v_conv_dw_b=v_conv_dw_b, v_conv_ln_g=v_conv_ln_g, v_conv_ln_b=v_conv_ln_b, v_w_conv_out=v_w_conv_out, v_ssd_conv_w=v_ssd_conv_w, v_ssd_conv_b=v_ssd_conv_b, v_ssd_dt_bias=v_ssd_dt_bias, v_ssd_A_log=v_ssd_A_log, v_ssd_D=v_ssd_D, v_ssd_norm_g=v_ssd_norm_g, v_w_ssd_out=v_w_ssd_out, v_attn_q_norm=v_attn_q_norm, v_attn_k_norm=v_attn_k_norm, v_attn_sinks=v_attn_sinks, v_w_attn_out=v_w_attn_out, v_w_mix_out=v_w_mix_out, v_norm_xattn=v_norm_xattn, v_norm_mem=v_norm_mem, v_w_xq=v_w_xq, v_w_xkv=v_w_xkv, v_xattn_q_norm=v_xattn_q_norm, v_xattn_k_norm=v_xattn_k_norm, v_w_xo=v_w_xo, v_norm_mlp=v_norm_mlp, v_w_mlp_up=v_w_mlp_up, v_w_mlp_down=v_w_mlp_down)
    weights = {n: given[n] for n in TWIN_WEIGHTS}
    shared = {n: given[n] for n in SHARED_INPUTS}
    per_example = {n: given[n] for n in ['x', 'mem']}
    grad_fn = _jax.value_and_grad(_loss, argnums=(0, 1))

    def one_microbatch(ex, loss_target):
        ex = dict(ex)
        diff = ex.pop(TWIN_DIFF_INPUT)
        return grad_fn(weights, diff, {**shared, **ex}, loss_target)

    if N_MICROBATCH == 1:
        loss, (grad_w, grad_x) = one_microbatch(per_example, given["loss_target"])
    else:
        def body(carry, xs):
            loss_sum, grad_sum = carry
            l_k, (gw_k, gx_k) = one_microbatch(xs[0], xs[1])
            with _jax.named_scope("update"):
                return (loss_sum + l_k, _jax.tree.map(_jnp.add, grad_sum, gw_k)), gx_k

        init = (_jnp.zeros((), _jnp.float32), _jax.tree.map(_jnp.zeros_like, weights))
        (loss, grad_w), grad_x = _jax.lax.scan(body, init, (per_example, given["loss_target"]))
    with _jax.named_scope("update"):
        delta_w, new_m, new_v = {}, {}, {}
        for n in TWIN_WEIGHTS:
            delta_w[n], new_m[n], new_v[n] = _adamw(weights[n], grad_w[n], given["m_" + n], given["v_" + n])
    return (loss, grad_x, *[grad_w[n] for n in TWIN_WEIGHTS], *[delta_w[n] for n in TWIN_WEIGHTS],
            *[new_m[n] for n in TWIN_WEIGHTS], *[new_v[n] for n in TWIN_WEIGHTS])
```

```python
import functools
import math

import numpy as np
import jax
import jax.numpy as jnp
from jax import lax
from jax.experimental import pallas as pl
from jax.experimental.pallas import tpu as pltpu

F32 = jnp.float32
BF16 = jnp.bfloat16
HI = lax.Precision.HIGHEST
MESH = pl.DeviceIdType.MESH

N_DEV = 8
D_MODEL = 1024
DEPTH = 4
MEM_LEN = 256
EPS = 1e-6
NEG_INF = -1e30
CONV_KERNEL = 31
SSD_INNER = 2048
SSD_HEAD_DIM = 64
SSD_HEADS = 32
SSD_GROUPS = 4
SSD_STATE = 128
SSD_CONV = 4
SSD_CHUNK = 128
SSD_XBC = SSD_INNER + 2 * SSD_GROUPS * SSD_STATE
HEADS_PER_GROUP = SSD_HEADS // SSD_GROUPS
ATTN_HEADS = 16
ATTN_KV_HEADS = 4
ATTN_HEAD_DIM = 64
ATTN_BLOCK = 128
ATTN_REP = ATTN_HEADS // ATTN_KV_HEADS
REL_BUCKETS = 32
REL_MAX_DIST = 128
XATTN_HEADS = 4
XATTN_HEAD_DIM = 256
MLP_HIDDEN = 4096
OFF_Z = 2048
OFF_XBC = 4096
OFF_DT = 7168
OFF_Q = 7200
OFF_GATE = 8736
IN_COLS = 11808
LANES = 128
CONV_PAD = 32

ADAM_LR, ADAM_B1, ADAM_B2, ADAM_EPS, ADAM_WD, ADAM_STEP = 0.001, 0.9, 0.999, 1e-08, 0.01, 10

WEIGHTS = ['rel_table', 'norm_mix', 'w_in', 'gate_bias', 'conv_dw_w', 'conv_dw_b', 'conv_ln_g', 'conv_ln_b', 'w_conv_out',
           'ssd_conv_w', 'ssd_conv_b', 'ssd_dt_bias', 'ssd_A_log', 'ssd_D', 'ssd_norm_g', 'w_ssd_out', 'attn_q_norm',
           'attn_k_norm', 'attn_sinks', 'w_attn_out', 'w_mix_out', 'norm_xattn', 'norm_mem', 'w_xq', 'w_xkv', 'xattn_q_norm',
           'xattn_k_norm', 'w_xo', 'norm_mlp', 'w_mlp_up', 'w_mlp_down']
ROW_SHARDED = ['w_conv_out', 'w_ssd_out', 'w_attn_out', 'w_mix_out', 'w_xq', 'w_xo', 'w_mlp_down']
COL_SHARDED = ['w_in', 'w_xkv', 'w_mlp_up']
BIG = ROW_SHARDED + COL_SHARDED
SMALL_SHARDED = ['gate_bias', 'conv_dw_w', 'ssd_conv_w']
SMALL = [n for n in WEIGHTS if n not in BIG]


def _bdot(a, b, dims):
    return lax.dot_general(a.astype(BF16), b.astype(BF16), (dims, ((), ())), preferred_element_type=F32)


def _hdot(a, b, dims):
    return lax.dot_general(a, b, (dims, ((), ())), precision=HI, preferred_element_type=F32)


def _pick(dim, pref):
    if dim <= pref:
        return dim
    t = (pref // LANES) * LANES
    while dim % t:
        t -= LANES
    return t


def _logical(op):
    arr, kind, _ = op
    r, c = arr.shape[-2:]
    return (r, c * N_DEV) if kind == 'col' else (r, c)


def _opspec(op, br, bc, rc):
    arr, kind, layer = op
    lead = () if layer is None else (layer,)
    none = (None,) * len(lead)
    if kind == 'plain':
        return pl.BlockSpec(none + (br, bc), lambda i, j, k: lead + rc(i, j, k))
    per = arr.shape[-1] // bc

    def imap(i, j, k):
        r, c = rc(i, j, k)
        if per == 1:
            return lead + (c, r, 0)
        return lead + (lax.div(c, per), r, lax.rem(c, per))
    return pl.BlockSpec(none + (None, br, bc), imap)


def _mm(a, b, *, mode, name, add=None, out_dtype=F32, out_col=False, tm=512, tn=1024, tk=1024):
    a = a if isinstance(a, tuple) else (a, 'plain', None)
    b = b if isinstance(b, tuple) else (b, 'plain', None)
    ar, ac = _logical(a)
    br_, bc_ = _logical(b)
    if mode == 'nn':
        m, kd, n = ar, ac, bc_
        assert br_ == kd
    elif mode == 'nt':
        m, kd, n = ar, ac, br_
        assert bc_ == kd
    else:
        m, kd, n = ac, ar, bc_
        assert br_ == kd

    def lim(op, is_col_dim):
        return op[0].shape[-1] if (op[1] == 'col' and is_col_dim) else 1 << 30

    tm = _pick(m, min(tm, lim(a, mode == 'tn')))
    tn = _pick(n, min(tn, lim(b, mode != 'nt'), (n // N_DEV) if out_col else 1 << 30))
    tk = _pick(kd, min(tk, lim(a, mode != 'tn'), lim(b, mode == 'nt')))
    nk = kd // tk
    if mode == 'nn':
        a_spec = _opspec(a, tm, tk, lambda i, j, k: (i, k))
        b_spec = _opspec(b, tk, tn, lambda i, j, k: (k, j))
        dims = ((1,), (0,))
    elif mode == 'nt':
        a_spec = _opspec(a, tm, tk, lambda i, j, k: (i, k))
        b_spec = _opspec(b, tn, tk, lambda i, j, k: (j, k))
        dims = ((1,), (1,))
    else:
        a_spec = _opspec(a, tk, tm, lambda i, j, k: (k, i))
        b_spec = _opspec(b, tk, tn, lambda i, j, k: (k, j))
        dims = ((0,), (0,))
    if out_col:
        out_shape = jax.ShapeDtypeStruct((N_DEV, m, n // N_DEV), out_dtype)
        out_spec = _opspec((out_shape, 'col', None), tm, tn, lambda i, j, k: (i, j))
    else:
        out_shape = jax.ShapeDtypeStruct((m, n), out_dtype)
        out_spec = pl.BlockSpec((tm, tn), lambda i, j, k: (i, j))
    has_add = add is not None

    def body(*refs):
        a_ref, b_ref = refs[0], refs[1]
        add_ref = refs[2] if has_add else None
        o_ref = refs[2 + has_add]
        part = _bdot(a_ref[...], b_ref[...], dims)
        if nk == 1:
            o_ref[...] = (part + add_ref[...].astype(F32) if has_add else part).astype(o_ref.dtype)
            return
        acc_ref = refs[3 + has_add]
        k = pl.program_id(2)

        @pl.when(k == 0)
        def _():
            acc_ref[...] = part + add_ref[...].astype(F32) if has_add else part

        @pl.when(k > 0)
        def _():
            acc_ref[...] += part

        @pl.when(k == nk - 1)
        def _():
            o_ref[...] = acc_ref[...].astype(o_ref.dtype)

    in_specs = [a_spec, b_spec]
    args = [a[0], b[0]]
    if has_add:
        in_specs.append(pl.BlockSpec((tm, tn), lambda i, j, k: (i, j)))
        args.append(add)
    return pl.pallas_call(
        body, name=name, grid=(m // tm, n // tn, nk), in_specs=in_specs, out_specs=out_spec, out_shape=out_shape,
        scratch_shapes=[pltpu.VMEM((tm, tn), F32)] if nk > 1 else [],
        compiler_params=pltpu.CompilerParams(dimension_semantics=("parallel", "parallel", "arbitrary")),
    )(*args)


def _rowwise(f, rows, consts, row_outs, acc_outs=(), *, tr, name):
    nr, nc, nro = len(rows), len(consts), len(row_outs)
    first = rows[0][0]
    t = first.shape[-2]
    assert t % tr == 0
    in_specs = []
    for spec in rows:
        arr, cb, w = spec[:3]
        lead = spec[3] if len(spec) > 3 else None
        if arr.ndim == 2:
            in_specs.append(pl.BlockSpec((tr, w), functools.partial(lambda i, cb: (i, cb), cb=cb)))
        elif lead is not None:
            in_specs.append(pl.BlockSpec((None, tr, w), functools.partial(lambda i, cb, lead: (lead, i, cb), cb=cb, lead=lead)))
        else:
            in_specs.append(pl.BlockSpec((arr.shape[0], tr, w), functools.partial(lambda i, cb: (0, i, cb), cb=cb)))
    for cst in consts:
        in_specs.append(pl.BlockSpec(cst.shape, functools.partial(lambda i, nd: (0,) * nd, nd=cst.ndim)))
    out_specs = [pl.BlockSpec((tr, w), lambda i: (i, 0)) for w, _ in row_outs]
    out_shape = [jax.ShapeDtypeStruct((t, w), dt) for w, dt in row_outs]
    for shp in acc_outs:
        out_specs.append(pl.BlockSpec(shp, functools.partial(lambda i, nd: (0,) * nd, nd=len(shp))))
        out_shape.append(jax.ShapeDtypeStruct(shp, F32))

    def body(*refs):
        ins = [r[...] for r in refs[:nr + nc]]
        ro = refs[nr + nc:nr + nc + nro]
        ao = refs[nr + nc + nro:]
        outs, accs = f(*ins)
        for o_ref, o in zip(ro, outs):
            o_ref[...] = o.astype(o_ref.dtype)
        if ao:
            i = pl.program_id(0)

            @pl.when(i == 0)
            def _():
                for a_ref, acc in zip(ao, accs):
                    a_ref[...] = acc

            @pl.when(i > 0)
            def _():
                for a_ref, acc in zip(ao, accs):
                    a_ref[...] += acc

    res = pl.pallas_call(
        body, name=name, grid=(t // tr,), in_specs=in_specs, out_specs=out_specs, out_shape=out_shape,
        compiler_params=pltpu.CompilerParams(dimension_semantics=("arbitrary",)),
    )(*[s[0] for s in rows], *consts)
    return res


def _vjp_rows(f, n_prim, n_rows_grad):
    def g(*args):
        prim, cots = args[:n_prim], args[n_prim:]
        outs, vjp = jax.vjp(f, *prim)
        grads = vjp(tuple(c.astype(o.dtype) for c, o in zip(cots, outs)))
        return tuple(grads[:n_rows_grad]), tuple(grads[n_rows_grad:])
    return g


def _rms(x, g):
    return x * lax.rsqrt(jnp.mean(x * x, axis=-1, keepdims=True) + EPS) * g


def _f_rms(h, g):
    return (_rms(h, g),)


def _f_rms_bwd(h, g, du, dres):
    _, vjp = jax.vjp(_f_rms, h, g)
    dh, dg = vjp((du.astype(F32),))
    return (dh + dres,), (dg,)


def _f_lnsilu(x, g, b):
    mu = jnp.mean(x, axis=-1, keepdims=True)
    xc = x - mu
    y = xc * lax.rsqrt(jnp.mean(xc * xc, axis=-1, keepdims=True) + EPS) * g + b
    return (jax.nn.silu(y),)


def _f_ssdgate(y, z, g):
    y = y * jax.nn.silu(z)
    w = SSD_INNER // SSD_GROUPS
    return (jnp.concatenate([_rms(y[:, i * w:(i + 1) * w], g[:, i * w:(i + 1) * w]) for i in range(SSD_GROUPS)], axis=-1),)


def _f_merge(pg, ya, yb, yc, gb):
    out = 0.0
    for i, yi in enumerate((ya, yb, yc)):
        out = out + jax.nn.sigmoid(pg[:, i * D_MODEL:(i + 1) * D_MODEL] + gb[i:i + 1, :]) * yi
    return (out,)


def _f_relu2(a):
    return (jnp.square(jnp.maximum(a, 0.0)),)


def _f_xattn(q, kv, qg, kg):
    outs = []
    for h in range(XATTN_HEADS):
        sl = slice(h * XATTN_HEAD_DIM, (h + 1) * XATTN_HEAD_DIM)
        qh = _rms(q[:, sl], qg)
        kh = _rms(kv[:, sl], kg)
        vh = kv[:, D_MODEL + h * XATTN_HEAD_DIM:D_MODEL + (h + 1) * XATTN_HEAD_DIM]
        s = _bdot(qh, kh, ((1,), (1,))) * (XATTN_HEAD_DIM ** -0.5)
        p = jnp.exp(s - jnp.max(s, axis=-1, keepdims=True))
        p = p / jnp.sum(p, axis=-1, keepdims=True)
        outs.append(_bdot(p, vh, ((1,), (0,))))
    return (jnp.concatenate(outs, axis=-1),)


def _f_loss(y, tgt):
    err = y - tgt
    per_row = jnp.sum(err * err, axis=-1, keepdims=True) * (0.5 / D_MODEL)
    loss = jnp.sum(per_row, axis=0, keepdims=True)
    return (err * (1.0 / D_MODEL),), (jnp.broadcast_to(loss, (1, LANES)),)


def _adam_core(w, g, m, v):
    m = ADAM_B1 * m + (1.0 - ADAM_B1) * g
    v = ADAM_B2 * v + (1.0 - ADAM_B2) * jnp.square(g)
    m_hat = m / (1.0 - ADAM_B1 ** ADAM_STEP)
    v_hat = v / (1.0 - ADAM_B2 ** ADAM_STEP)
    delta = -ADAM_LR * (m_hat / (jnp.sqrt(v_hat) + ADAM_EPS) + ADAM_WD * w)
    return delta, m, v


def _sum_slots(g8):
    g = g8[0].astype(F32)
    for s in range(1, N_DEV):
        g = g + g8[s].astype(F32)
    return g


def _f_adam_slots(g8, w, m, v):
    g = _sum_slots(g8)
    return (g,) + _adam_core(w, g, m, v), ()


def _f_sum_slots(g8):
    return (_sum_slots(g8),), ()


def _f_adam(g, w, m, v):
    return _adam_core(w, g, m, v), ()


def _conv_chunk(t):
    return 256 if t % 256 == 0 else t


def _conv_fwd(srcs, w, b, *, kk, pre_glu, post_silu, name):
    t = srcs[0][0].shape[0]
    c = w.shape[-1]
    tt = _conv_chunk(t)
    ns = len(srcs)

    def body(*refs):
        w_ref, b_ref, o_ref, pad_ref = refs[ns:]
        if pre_glu:
            xin = refs[0][...] * jax.nn.sigmoid(refs[1][...])
        else:
            xin = refs[0][...]
        pad_ref[0:CONV_PAD, :] = jnp.zeros((CONV_PAD, LANES), F32)
        pad_ref[CONV_PAD:, :] = xin

        def chunk(i, carry):
            base = pl.multiple_of(i * tt, tt)
            acc = jnp.broadcast_to(b_ref[...], (tt, LANES))
            for j in range(kk):
                acc = acc + pad_ref[pl.ds(base + CONV_PAD - (kk - 1) + j, tt), :] * w_ref[j:j + 1, :]
            o_ref[pl.ds(base, tt), :] = jax.nn.silu(acc) if post_silu else acc
            return carry
        lax.fori_loop(0, t // tt, chunk, 0)

    in_specs = [pl.BlockSpec((t, LANES), functools.partial(lambda i, off: (0, off + i), off=off)) for _, off in srcs]
    in_specs += [pl.BlockSpec((kk, LANES), lambda i: (0, i)), pl.BlockSpec((1, LANES), lambda i: (0, i))]
    return pl.pallas_call(
        body, name=name, grid=(c // LANES,), in_specs=in_specs, out_specs=pl.BlockSpec((t, LANES), lambda i: (0, i)),
        out_shape=jax.ShapeDtypeStruct((t, c), F32), scratch_shapes=[pltpu.VMEM((t + CONV_PAD, LANES), F32)],
        compiler_params=pltpu.CompilerParams(dimension_semantics=("parallel",)),
    )(*[s[0] for s in srcs], w, b)


def _conv_bwd(srcs, w, b, dys, *, kk, pre_glu, post_silu, name, dx_dtype):
    t = srcs[0][0].shape[0]
    c = w.shape[-1]
    tt = _conv_chunk(t)
    ns, nd = len(srcs), len(dys)

    def body(*refs):
        src_refs = refs[:ns]
        dy_refs = refs[ns:ns + nd]
        w_ref, b_ref = refs[ns + nd:ns + nd + 2]
        outs = refs[ns + nd + 2:]
        dx_refs, dw_ref, db_ref = outs[:ns], outs[ns], outs[ns + 1]
        pad_ref, dpad_ref = outs[ns + 2:]
        cb = pl.program_id(0)
        if pre_glu:
            a_in = src_refs[0][...]
            sg = jax.nn.sigmoid(src_refs[1][...])
            xin = a_in * sg
        else:
            xin = src_refs[0][...]
        pad_ref[0:CONV_PAD, :] = jnp.zeros((CONV_PAD, LANES), F32)
        pad_ref[CONV_PAD:, :] = xin
        dpad_ref[t:, :] = jnp.zeros((CONV_PAD, LANES), F32)
        dw_ref[...] = jnp.zeros_like(dw_ref)
        db_ref[...] = jnp.zeros_like(db_ref)

        def load_dy(base):
            dy = dy_refs[0][pl.ds(base, tt), :].astype(F32)
            for (_, first, _n), r in zip(dys[1:], dy_refs[1:]):
                dy = jnp.where(cb >= first, r[pl.ds(base, tt), :].astype(F32), dy)
            return dy

        def chunk1(i, carry):
            base = pl.multiple_of(i * tt, tt)
            dy = load_dy(base)
            if post_silu:
                acc = jnp.broadcast_to(b_ref[...], (tt, LANES))
                for j in range(kk):
                    acc = acc + pad_ref[pl.ds(base + CONV_PAD - (kk - 1) + j, tt), :] * w_ref[j:j + 1, :]
                s = jax.nn.sigmoid(acc)
                dy = dy * (s * (1.0 + acc * (1.0 - s)))
            dpad_ref[pl.ds(base, tt), :] = dy
            db_ref[...] += jnp.sum(dy, axis=0, keepdims=True)
            for j in range(kk):
                dw_ref[j:j + 1, :] += jnp.sum(dy * pad_ref[pl.ds(base + CONV_PAD - (kk - 1) + j, tt), :], axis=0, keepdims=True)
            return carry
        lax.fori_loop(0, t // tt, chunk1, 0)

        def chunk2(i, carry):
            base = pl.multiple_of(i * tt, tt)
            acc = jnp.zeros((tt, LANES), F32)
            for j in range(kk):
                acc = acc + dpad_ref[pl.ds(base + (kk - 1) - j, tt), :] * w_ref[j:j + 1, :]
            if pre_glu:
                a_c = src_refs[0][pl.ds(base, tt), :]
                s_c = jax.nn.sigmoid(src_refs[1][pl.ds(base, tt), :])
                dx_refs[0][pl.ds(base, tt), :] = (acc * s_c).astype(dx_dtype)
                dx_refs[1][pl.ds(base, tt), :] = (acc * a_c * s_c * (1.0 - s_c)).astype(dx_dtype)
            else:
                dx_refs[0][pl.ds(base, tt), :] = acc.astype(dx_dtype)
            return carry
        lax.fori_loop(0, t // tt, chunk2, 0)

    in_specs = [pl.BlockSpec((t, LANES), functools.partial(lambda i, off: (0, off + i), off=off)) for _, off in srcs]
    for _, first, n in dys:
        in_specs.append(pl.BlockSpec((t, LANES), functools.partial(lambda i, first, n: (0, jnp.clip(i - first, 0, n - 1)), first=first, n=n)))
    in_specs += [pl.BlockSpec((kk, LANES), lambda i: (0, i)), pl.BlockSpec((1, LANES), lambda i: (0, i))]
    out_specs = [pl.BlockSpec((t, LANES), lambda i: (0, i)) for _ in srcs]
    out_specs += [pl.BlockSpec((kk, LANES), lambda i: (0, i)), pl.BlockSpec((1, LANES), lambda i: (0, i))]
    out_shape = [jax.ShapeDtypeStruct((t, c), dx_dtype) for _ in srcs]
    out_shape += [jax.ShapeDtypeStruct((kk, c), F32), jax.ShapeDtypeStruct((1, c), F32)]
    return pl.pallas_call(
        body, name=name, grid=(c // LANES,), in_specs=in_specs, out_specs=out_specs, out_shape=out_shape,
        scratch_shapes=[pltpu.VMEM((t + CONV_PAD, LANES), F32), pltpu.VMEM((t + CONV_PAD, LANES), F32)],
        compiler_params=pltpu.CompilerParams(dimension_semantics=("parallel",)),
    )(*[s[0] for s in srcs], *[d[0] for d in dys], w, b)


def _expand_heads(v8):
    r = v8.shape[0]
    return jnp.concatenate([jnp.broadcast_to(v8[:, j:j + 1], (r, SSD_HEAD_DIM)) for j in range(HEADS_PER_GROUP)], axis=-1)


def _ssd_chunk(x, dtfull, bm, cm, s_in, dt_bias, a_log, dskip, sel):
    q = x.shape[0]
    dt_all = jax.nn.softplus(dtfull + dt_bias)
    da_all = dt_all * (-jnp.exp(a_log))
    dt8 = _hdot(dt_all, sel, ((1,), (0,)))
    da8 = _hdot(da_all, sel, ((1,), (0,)))
    d8 = _hdot(jnp.broadcast_to(dskip, (8, LANES)), sel, ((1,), (0,)))[0:1, :]
    row = lax.broadcasted_iota(jnp.int32, (q, q), 0)
    col = lax.broadcasted_iota(jnp.int32, (q, q), 1)
    causal = row >= col
    cs = _hdot(causal.astype(F32), da8, ((1,), (0,)))
    cs_last = cs[q - 1:q, :]
    xdt = x * _expand_heads(dt8)
    cb = _bdot(cm, bm, ((1,), (1,)))
    y_off = _bdot(cm, s_in, ((1,), (1,))) * _expand_heads(jnp.exp(cs))
    new_states = _bdot(xdt * _expand_heads(jnp.exp(cs_last - cs)), bm, ((0,), (0,)))
    chunk_decay = jnp.exp(cs_last)
    decay_rows = jnp.concatenate([jnp.broadcast_to(chunk_decay[:, j:j + 1], (SSD_HEAD_DIM, 1)) for j in range(HEADS_PER_GROUP)], axis=0)
    s_out = s_in * decay_rows + new_states
    y_diag = []
    for j in range(HEADS_PER_GROUP):
        cj = cs[:, j:j + 1]
        diff = cj - jnp.transpose(cj)
        decay = jnp.where(causal, jnp.exp(jnp.where(causal, diff, 0.0)), 0.0)
        y_diag.append(_bdot(cb * decay, xdt[:, j * SSD_HEAD_DIM:(j + 1) * SSD_HEAD_DIM], ((1,), (0,))))
    y = jnp.concatenate(y_diag, axis=-1) + y_off + x * _expand_heads(d8)
    return y, s_out


def _head_select(g):
    lane = lax.broadcasted_iota(jnp.int32, (LANES, HEADS_PER_GROUP), 0)
    head = lax.broadcasted_iota(jnp.int32, (LANES, HEADS_PER_GROUP), 1)
    return (lane == g * HEADS_PER_GROUP + head).astype(F32)


def _ssd_in_specs(cmap):
    q, w = SSD_CHUNK, SSD_INNER // SSD_GROUPS
    nxb = SSD_INNER // SSD_STATE
    return [
        pl.BlockSpec((q, w), lambda c, g: (cmap(c), g)),
        pl.BlockSpec((q, LANES), lambda c, g: (cmap(c), 0)),
        pl.BlockSpec((q, SSD_STATE), lambda c, g: (cmap(c), nxb + g)),
        pl.BlockSpec((q, SSD_STATE), lambda c, g: (cmap(c), nxb + SSD_GROUPS + g)),
    ]


def _ssd_fwd(xbc, pdt, dt_bias, a_log, dskip, *, name):
    t = xbc.shape[0]
    nc = t // SSD_CHUNK
    w = SSD_INNER // SSD_GROUPS

    def body(x_ref, dt_ref, b_ref, c_ref, tb_ref, al_ref, d_ref, y_ref, s_ref, state_ref):
        c, g = pl.program_id(0), pl.program_id(1)

        @pl.when(c == 0)
        def _():
            state_ref[g] = jnp.zeros((w, SSD_STATE), F32)
        s_in = state_ref[g]
        s_ref[...] = s_in
        y, s_out = _ssd_chunk(x_ref[...], dt_ref[...], b_ref[...], c_ref[...], s_in, tb_ref[...], al_ref[...], d_ref[...], _head_select(g))
        y_ref[...] = y
        state_ref[g] = s_out

    vec = pl.BlockSpec((1, LANES), lambda c, g: (0, 0))
    return pl.pallas_call(
        body, name=name, grid=(nc, SSD_GROUPS), in_specs=_ssd_in_specs(lambda c: c) + [vec, vec, vec],
        out_specs=[pl.BlockSpec((SSD_CHUNK, w), lambda c, g: (c, g)), pl.BlockSpec((None, None, w, SSD_STATE), lambda c, g: (c, g, 0, 0))],
        out_shape=[jax.ShapeDtypeStruct((t, SSD_INNER), F32), jax.ShapeDtypeStruct((nc, SSD_GROUPS, w, SSD_STATE), F32)],
        scratch_shapes=[pltpu.VMEM((SSD_GROUPS, w, SSD_STATE), F32)],
        compiler_params=pltpu.CompilerParams(dimension_semantics=("arbitrary", "arbitrary")),
    )(xbc, pdt, xbc, xbc, dt_bias, a_log, dskip)


def _ssd_bwd(xbc, pdt, states, dy, dt_bias, a_log, dskip, *, name):
    t = xbc.shape[0]
    nc = t // SSD_CHUNK
    w = SSD_INNER // SSD_GROUPS
    rev = lambda c: nc - 1 - c

    def body(x_ref, dt_ref, b_ref, c_ref, s_ref, dy_ref, tb_ref, al_ref, d_ref,
             dx_ref, db_ref, dc_ref, ddt_ref, dtb_ref, dal_ref, dd_ref, dstate_ref):
        c, g = pl.program_id(0), pl.program_id(1)

        @pl.when(c == 0)
        def _():
            dstate_ref[g] = jnp.zeros((w, SSD_STATE), F32)

        @pl.when((c == 0) & (g == 0))
        def _():
            dtb_ref[...] = jnp.zeros_like(dtb_ref)
            dal_ref[...] = jnp.zeros_like(dal_ref)
            dd_ref[...] = jnp.zeros_like(dd_ref)
        sel = _head_select(g)
        f = functools.partial(_ssd_chunk, sel=sel)
        _, vjp = jax.vjp(f, x_ref[...], dt_ref[...], b_ref[...], c_ref[...], s_ref[...], tb_ref[...], al_ref[...], d_ref[...])
        dx, ddt, db, dc, ds_in, dtb, dal, dd = vjp((dy_ref[...].astype(F32), dstate_ref[g]))
        dx_ref[...] = dx
        db_ref[...] = db
        dc_ref[...] = dc
        dstate_ref[g] = ds_in

        @pl.when(g == 0)
        def _():
            ddt_ref[...] = ddt

        @pl.when(g > 0)
        def _():
            ddt_ref[...] += ddt
        dtb_ref[...] += dtb
        dal_ref[...] += dal
        dd_ref[...] += dd

    vec = pl.BlockSpec((1, LANES), lambda c, g: (0, 0))
    in_specs = _ssd_in_specs(rev) + [pl.BlockSpec((None, None, w, SSD_STATE), lambda c, g: (rev(c), g, 0, 0)),
                                     pl.BlockSpec((SSD_CHUNK, w), lambda c, g: (rev(c), g)), vec, vec, vec]
    out_specs = [pl.BlockSpec((SSD_CHUNK, w), lambda c, g: (rev(c), g)),
                 pl.BlockSpec((SSD_CHUNK, SSD_STATE), lambda c, g: (rev(c), g)),
                 pl.BlockSpec((SSD_CHUNK, SSD_STATE), lambda c, g: (rev(c), g)),
                 pl.BlockSpec((SSD_CHUNK, LANES), lambda c, g: (rev(c), 0)), vec, vec, vec]
    gs = SSD_GROUPS * SSD_STATE
    out_shape = [jax.ShapeDtypeStruct((t, SSD_INNER), F32), jax.ShapeDtypeStruct((t, gs), F32), jax.ShapeDtypeStruct((t, gs), F32),
                 jax.ShapeDtypeStruct((t, LANES), F32)] + [jax.ShapeDtypeStruct((1, LANES), F32)] * 3
    return pl.pallas_call(
        body, name=name, grid=(nc, SSD_GROUPS), in_specs=in_specs, out_specs=out_specs, out_shape=out_shape,
        scratch_shapes=[pltpu.VMEM((SSD_GROUPS, w, SSD_STATE), F32)],
        compiler_params=pltpu.CompilerParams(dimension_semantics=("arbitrary", "arbitrary")),
    )(xbc, pdt, xbc, xbc, states, dy, dt_bias, a_log, dskip)


def _rel_buckets():
    qi = np.arange(ATTN_BLOCK)[:, None] + ATTN_BLOCK
    kj = np.arange(2 * ATTN_BLOCK)[None, :]
    dist = qi - kj
    max_exact = REL_BUCKETS // 2
    d = np.maximum(dist, 1).astype(np.float32)
    large = max_exact + (np.log(d / np.float32(max_exact)) / np.float32(math.log(REL_MAX_DIST / max_exact))
                         * np.float32(REL_BUCKETS - max_exact)).astype(np.int32)
    large = np.minimum(large, REL_BUCKETS - 1)
    return np.where(dist < max_exact, np.maximum(dist, 0), large).astype(np.int32)


def _onehot_buckets(bucket_ref):
    n = bucket_ref.shape[-1]
    return (lax.broadcasted_iota(jnp.int32, (REL_BUCKETS, n), 0) == bucket_ref[...]).astype(F32)


def _band_bias(rel_table_t, buckets):
    n = buckets.shape[-1]

    def body(rt_ref, bk_ref, o_ref):
        o_ref[...] = _hdot(rt_ref[...], _onehot_buckets(bk_ref), ((1,), (0,)))
    return pl.pallas_call(body, name="band_bias", out_shape=jax.ShapeDtypeStruct((ATTN_HEADS, n), F32))(rel_table_t, buckets)


def _band_bias_bwd(dbias, buckets):
    def body(db_ref, bk_ref, o_ref):
        d = db_ref[0]
        for layer in range(1, db_ref.shape[0]):
            d = d + db_ref[layer]
        o_ref[...] = _hdot(d, _onehot_buckets(bk_ref), ((1,), (1,)))
    return pl.pallas_call(body, name="band_bias_bwd", out_shape=jax.ShapeDtypeStruct((ATTN_HEADS, REL_BUCKETS), F32))(dbias, buckets)


def _attn_block(q, kvp, kvc, bias, sinks, qg, kg, first):
    qn = q.shape[0]
    rows = ATTN_REP * qn
    ri = lax.broadcasted_iota(jnp.int32, (rows, 2 * qn), 0) & (qn - 1)
    cj = lax.broadcasted_iota(jnp.int32, (rows, 2 * qn), 1)
    jj = cj & (qn - 1)
    no_prev = jnp.where(first, qn, 0)
    mask = ((cj < qn) & (jj > ri + no_prev)) | ((cj >= qn) & (jj <= ri))
    kvd = ATTN_KV_HEADS * ATTN_HEAD_DIM
    outs = []
    for g in range(ATTN_KV_HEADS):
        sl = slice(g * ATTN_HEAD_DIM, (g + 1) * ATTN_HEAD_DIM)
        vsl = slice(kvd + g * ATTN_HEAD_DIM, kvd + (g + 1) * ATTN_HEAD_DIM)
        qs = jnp.concatenate([_rms(q[:, (g * ATTN_REP + j) * ATTN_HEAD_DIM:(g * ATTN_REP + j + 1) * ATTN_HEAD_DIM], qg)
                              for j in range(ATTN_REP)], axis=0)
        kb = jnp.concatenate([_rms(kvp[:, sl], kg), _rms(kvc[:, sl], kg)], axis=0)
        vb = jnp.concatenate([kvp[:, vsl], kvc[:, vsl]], axis=0)
        logits = _bdot(qs, kb, ((1,), (1,))) * (ATTN_HEAD_DIM ** -0.5) + bias[g]
        logits = jnp.where(mask, logits, NEG_INF)
        sink = jnp.concatenate([jnp.broadcast_to(sinks[:, g * ATTN_REP + j:g * ATTN_REP + j + 1], (qn, 1)) for j in range(ATTN_REP)], axis=0)
        m = jnp.maximum(jnp.max(logits, axis=-1, keepdims=True), sink)
        pexp = jnp.exp(logits - m)
        probs = pexp / (jnp.sum(pexp, axis=-1, keepdims=True) + jnp.exp(sink - m))
        o = _bdot(probs, vb, ((1,), (0,)))
        outs += [o[j * qn:(j + 1) * qn, :] for j in range(ATTN_REP)]
    return jnp.concatenate(outs, axis=-1)


def _attn_specs(nmap):
    qd, kvw = ATTN_HEADS * ATTN_HEAD_DIM, 2 * ATTN_KV_HEADS * ATTN_HEAD_DIM
    full = lambda shp: pl.BlockSpec(shp, lambda i: (0,) * len(shp))
    return [
        pl.BlockSpec((ATTN_BLOCK, qd), lambda i: (nmap(i), 0)),
        pl.BlockSpec((ATTN_BLOCK, kvw), lambda i: (jnp.maximum(nmap(i) - 1, 0), qd // kvw)),
        pl.BlockSpec((ATTN_BLOCK, kvw), lambda i: (nmap(i), qd // kvw)),
        full((ATTN_KV_HEADS, ATTN_REP * ATTN_BLOCK, 2 * ATTN_BLOCK)), full((1, ATTN_HEADS)), full((1, ATTN_HEAD_DIM)), full((1, ATTN_HEAD_DIM)),
    ]


def _attn_fwd(pqkv, bias, sinks, qg, kg, *, name):
    t = pqkv.shape[0]
    qd = ATTN_HEADS * ATTN_HEAD_DIM

    def body(q_ref, kvp_ref, kvc_ref, bias_ref, sk_ref, qg_ref, kg_ref, o_ref):
        first = pl.program_id(0) == 0
        o_ref[...] = _attn_block(q_ref[...], kvp_ref[...], kvc_ref[...], bias_ref[...], sk_ref[...], qg_ref[...], kg_ref[...], first).astype(o_ref.dtype)

    return pl.pallas_call(
        body, name=name, grid=(t // ATTN_BLOCK,), in_specs=_attn_specs(lambda i: i),
        out_specs=pl.BlockSpec((ATTN_BLOCK, qd), lambda i: (i, 0)), out_shape=jax.ShapeDtypeStruct((t, qd), BF16),
        compiler_params=pltpu.CompilerParams(dimension_semantics=("parallel",)),
    )(pqkv, pqkv, pqkv, bias, sinks, qg, kg)


def _attn_bwd(pqkv, do, bias, sinks, qg, kg, *, name):
    t = pqkv.shape[0]
    nb = t // ATTN_BLOCK
    qd, kvw = ATTN_HEADS * ATTN_HEAD_DIM, 2 * ATTN_KV_HEADS * ATTN_HEAD_DIM
    rev = lambda i: nb - 1 - i

    def body(q_ref, kvp_ref, kvc_ref, bias_ref, sk_ref, qg_ref, kg_ref, do_ref, dqkv_ref, dbias_ref, dsk_ref, dqg_ref, dkg_ref, carry_ref):
        i = pl.program_id(0)
        first = rev(i) == 0
        f = functools.partial(_attn_block, first=first)
        _, vjp = jax.vjp(f, q_ref[...], kvp_ref[...], kvc_ref[...], bias_ref[...], sk_ref[...], qg_ref[...], kg_ref[...])
        dq, dkvp, dkvc, dbias, dsk, dqg, dkg = vjp(do_ref[...].astype(F32))

        @pl.when(i == 0)
        def _():
            carry_ref[...] = jnp.zeros_like(carry_ref)
            dbias_ref[...] = jnp.zeros_like(dbias_ref)
            dsk_ref[...] = jnp.zeros_like(dsk_ref)
            dqg_ref[...] = jnp.zeros_like(dqg_ref)
            dkg_ref[...] = jnp.zeros_like(dkg_ref)
        dqkv_ref[:, 0:qd] = dq.astype(dqkv_ref.dtype)
        dqkv_ref[:, qd:] = (dkvc + carry_ref[...]).astype(dqkv_ref.dtype)
        carry_ref[...] = dkvp
        dbias_ref[...] += dbias
        dsk_ref[...] += dsk
        dqg_ref[...] += dqg
        dkg_ref[...] += dkg

    full = lambda shp: pl.BlockSpec(shp, lambda i: (0,) * len(shp))
    bshape = (ATTN_KV_HEADS, ATTN_REP * ATTN_BLOCK, 2 * ATTN_BLOCK)
    return pl.pallas_call(
        body, name=name, grid=(nb,), in_specs=_attn_specs(rev) + [pl.BlockSpec((ATTN_BLOCK, qd), lambda i: (rev(i), 0))],
        out_specs=[pl.BlockSpec((ATTN_BLOCK, qd + kvw), lambda i: (rev(i), 0)), full(bshape), full((1, ATTN_HEADS)),
                   full((1, ATTN_HEAD_DIM)), full((1, ATTN_HEAD_DIM))],
        out_shape=[jax.ShapeDtypeStruct((t, qd + kvw), BF16), jax.ShapeDtypeStruct(bshape, F32), jax.ShapeDtypeStruct((1, ATTN_HEADS), F32),
                   jax.ShapeDtypeStruct((1, ATTN_HEAD_DIM), F32), jax.ShapeDtypeStruct((1, ATTN_HEAD_DIM), F32)],
        scratch_shapes=[pltpu.VMEM((ATTN_BLOCK, kvw), F32)],
        compiler_params=pltpu.CompilerParams(dimension_semantics=("arbitrary",)),
    )(pqkv, pqkv, pqkv, bias, sinks, qg, kg, do)


def _dev_index(dev):
    return 4 * dev[0] + 2 * dev[1] + dev[2]


def _all_gather(xs, axes, *, name):
    n = len(xs)
    hbm = pl.BlockSpec(memory_space=pltpu.HBM)

    def body(*refs):
        x_refs, o_refs = refs[:n], refs[n:2 * n]
        send_sems, recv_sems, local_sems = refs[2 * n:]
        x, y, c = lax.axis_index("x"), lax.axis_index("y"), lax.axis_index("c")
        me, sibling = (x, y, c), (x, y, 1 - c)
        chips = [(1 - x, y), (x, 1 - y), (1 - x, 1 - y)]

        def slot(i, dev):
            idx = _dev_index(dev)
            return o_refs[i].at[idx] if axes[i] == 0 else o_refs[i].at[:, idx]

        def copy(i, k, block, to, src=None):
            return pltpu.make_async_remote_copy(
                src_ref=slot(i, block) if src is None else src, dst_ref=slot(i, block),
                send_sem=send_sems.at[i, k], recv_sem=recv_sems.at[i, k], device_id=to, device_id_type=MESH)

        mine = [pltpu.make_async_copy(x_refs[i], slot(i, me), local_sems.at[i]) for i in range(n)]
        for cp in mine:
            cp.start()
        first = []
        for i in range(n):
            first.append(copy(i, 0, me, sibling, src=x_refs[i]))
            first += [copy(i, 1 + j, me, (*chip, c), src=x_refs[i]) for j, chip in enumerate(chips)]
        for cp in first:
            cp.start()
        passed = []
        for j, chip in enumerate(chips):
            for i in range(n):
                copy(i, 1 + j, (*chip, c), me).wait_recv()
                cp = copy(i, 4 + j, (*chip, c), sibling)
                cp.start()
                passed.append(cp)
        for i in range(n):
            copy(i, 0, sibling, me).wait_recv()
        for j, chip in enumerate(chips):
            for i in range(n):
                copy(i, 4 + j, (*chip, 1 - c), me).wait_recv()
        for cp in first + passed:
            cp.wait_send()
        for cp in mine:
            cp.wait()

    def oshape(a, ax):
        return a.shape[:ax] + (N_DEV,) + a.shape[ax:]

    return pl.pallas_call(
        body, name=name, in_specs=[hbm] * n, out_specs=[hbm] * n,
        out_shape=[jax.ShapeDtypeStruct(oshape(a, ax), a.dtype) for a, ax in zip(xs, axes)],
        scratch_shapes=[pltpu.SemaphoreType.DMA((n, 7)), pltpu.SemaphoreType.DMA((n, 7)), pltpu.SemaphoreType.DMA((n,))],
        compiler_params=pltpu.CompilerParams(has_side_effects=True),
    )(*xs)


def _exchange(gs, *, name):
    n = len(gs)
    hbm = pl.BlockSpec(memory_space=pltpu.HBM)

    def body(*refs):
        g_refs, o_refs = refs[:n], refs[n:2 * n]
        send_sems, recv_sems, local_sems = refs[2 * n:]
        x, y, c = lax.axis_index("x"), lax.axis_index("y"), lax.axis_index("c")
        me = _dev_index((x, y, c))
        peers = [(x ^ ((k >> 2) & 1), y ^ ((k >> 1) & 1), c ^ (k & 1)) for k in range(1, N_DEV)]

        def copy(i, k):
            peer = peers[k]
            return pltpu.make_async_remote_copy(
                src_ref=g_refs[i].at[_dev_index(peer)], dst_ref=o_refs[i].at[me],
                send_sem=send_sems.at[i, k], recv_sem=recv_sems.at[i, k], device_id=peer, device_id_type=MESH)

        def arrival(i, k):
            peer = peers[k]
            return pltpu.make_async_remote_copy(
                src_ref=g_refs[i].at[me], dst_ref=o_refs[i].at[_dev_index(peer)],
                send_sem=send_sems.at[i, k], recv_sem=recv_sems.at[i, k], device_id=peer, device_id_type=MESH)

        mine = [pltpu.make_async_copy(g_refs[i].at[me], o_refs[i].at[me], local_sems.at[i]) for i in range(n)]
        for cp in mine:
            cp.start()
        sends = [copy(i, k) for i in range(n) for k in range(N_DEV - 1)]
        for cp in sends:
            cp.start()
        for i in range(n):
            for k in range(N_DEV - 1):
                arrival(i, k).wait_recv()
        for cp in sends:
            cp.wait_send()
        for cp in mine:
            cp.wait()

    return pl.pallas_call(
        body, name=name, in_specs=[hbm] * n, out_specs=[hbm] * n,
        out_shape=[jax.ShapeDtypeStruct(g.shape, g.dtype) for g in gs],
        scratch_shapes=[pltpu.SemaphoreType.DMA((n, 7)), pltpu.SemaphoreType.DMA((n, 7)), pltpu.SemaphoreType.DMA((n,))],
        compiler_params=pltpu.CompilerParams(has_side_effects=True),
    )(*gs)


def _pack(arrays):
    parts = []
    for a in arrays:
        flat = a.reshape(-1)
        pad = (-flat.shape[0]) % LANES
        if pad:
            flat = jnp.concatenate([flat, jnp.zeros((pad,), flat.dtype)])
        parts.append(flat.reshape(-1, LANES))
    return jnp.concatenate(parts, axis=0)


def _unpack(buf, shapes):
    out, row = [], 0
    for shp in shapes:
        size = int(np.prod(shp))
        rows = -(-size // LANES)
        out.append(buf[row:row + rows].reshape(-1)[:size].reshape(shp))
        row += rows
    return out


def _row_tile(rows, width, n_bufs):
    padded = -(-width // LANES) * LANES
    cap = max(16, (12 << 20) // (padded * 4 * n_bufs))
    if rows <= cap:
        return rows
    tr = (cap // 16) * 16
    while tr > 16 and rows % tr:
        tr -= 16
    return tr if rows % tr == 0 else rows


def _step(p, m, v, x, mem, loss_target):
    t = x.shape[1]
    h0 = x.reshape(t, D_MODEL)
    mem2 = mem.reshape(MEM_LEN, D_MODEL)
    tgt = loss_target.reshape(t, D_MODEL)
    tr = 256 if t % 256 == 0 else t
    my = _dev_index((lax.axis_index("x"), lax.axis_index("y"), lax.axis_index("c")))

    small_sh_shapes = [p[n].shape for n in SMALL_SHARDED]
    gathered_small = _all_gather([_pack([p[n] for n in SMALL_SHARDED])], [0], name="gather_small")[0]
    full = {}
    for n, a in zip(SMALL_SHARDED, zip(*[_unpack(gathered_small[d], small_sh_shapes) for d in range(N_DEV)])):
        full[n] = jnp.concatenate(a, axis=-1)
    big_names = ROW_SHARDED + COL_SHARDED
    gathered = _all_gather([p[n].astype(BF16) for n in big_names], [1] * len(big_names), name="gather_weights")
    wg = dict(zip(big_names, gathered))
    for n in ROW_SHARDED:
        a = wg[n]
        wg[n] = a.reshape(a.shape[0], a.shape[1] * a.shape[2], a.shape[3])
    w_in_full = jnp.transpose(wg['w_in'], (0, 2, 1, 3)).reshape(DEPTH, D_MODEL, IN_COLS)
    w_seg = {
        'a': w_in_full[:, :, 0:OFF_Z], 'z': w_in_full[:, :, OFF_Z:OFF_XBC], 'x': w_in_full[:, :, OFF_XBC:OFF_DT],
        'dt': jnp.pad(w_in_full[:, :, OFF_DT:OFF_Q], ((0, 0), (0, 0), (0, LANES - SSD_HEADS))),
        'qkv': w_in_full[:, :, OFF_Q:OFF_GATE], 'g': w_in_full[:, :, OFF_GATE:IN_COLS],
    }
    seg_order = ['a', 'z', 'x', 'dt', 'qkv', 'g']

    def vec(name, layer, width=None):
        a = p[name][layer].reshape(1, -1)
        if width is not None and a.shape[1] < width:
            a = jnp.pad(a, ((0, 0), (0, width - a.shape[1])))
        return a

    buckets = jnp.asarray(_rel_buckets().reshape(1, -1))
    bias = _band_bias(jnp.transpose(p['rel_table']), buckets).reshape(ATTN_KV_HEADS, ATTN_REP * ATTN_BLOCK, 2 * ATTN_BLOCK)

    saved = []
    h = h0
    for l in range(DEPTH):
        s = {'h0': h}
        u = _rowwise(lambda a, g: (_f_rms(a, g), ()), [(h, 0, D_MODEL)], [vec('norm_mix', l)], [(D_MODEL, BF16)], tr=tr, name="rms_mix")[0]
        s['u'] = u
        pr = {k: _mm(u, (w_seg[k], 'plain', l), mode='nn', name="proj_" + k) for k in seg_order}
        s['pr'] = pr
        dw_w, dw_b = full['conv_dw_w'][l], vec('conv_dw_b', l)
        ca = _conv_fwd([(pr['a'], 0), (pr['a'], D_MODEL // LANES)], dw_w, dw_b, kk=CONV_KERNEL, pre_glu=True, post_silu=False, name="conv31")
        s['ca'] = ca
        ya_in = _rowwise(lambda a, g, b: (_f_lnsilu(a, g, b), ()), [(ca, 0, D_MODEL)], [vec('conv_ln_g', l), vec('conv_ln_b', l)],
                         [(D_MODEL, BF16)], tr=tr, name="ln_silu")[0]
        s['ya_in'] = ya_in
        y_a = _mm(ya_in, (wg['w_conv_out'], 'plain', l), mode='nn', name="conv_out")
        xbc = _conv_fwd([(pr['x'], 0)], full['ssd_conv_w'][l], vec('ssd_conv_b', l), kk=SSD_CONV, pre_glu=False, post_silu=True, name="conv4")
        s['xbc'] = xbc
        ssd_vecs = [vec('ssd_dt_bias', l, LANES), vec('ssd_A_log', l, LANES), vec('ssd_D', l, LANES)]
        y_ssd, states = _ssd_fwd(xbc, pr['dt'], *ssd_vecs, name="ssd")
        s['y_ssd'], s['states'] = y_ssd, states
        yb_in = _rowwise(lambda a, z, g: (_f_ssdgate(a, z, g), ()), [(y_ssd, 0, SSD_INNER), (pr['z'], 0, SSD_INNER)], [vec('ssd_norm_g', l)],
                         [(SSD_INNER, BF16)], tr=tr, name="ssd_gate")[0]
        s['yb_in'] = yb_in
        y_b = _mm(yb_in, (wg['w_ssd_out'], 'plain', l), mode='nn', name="ssd_out")
        att = _attn_fwd(pr['qkv'], bias, vec('attn_sinks', l), vec('attn_q_norm', l), vec('attn_k_norm', l), name="swa")
        s['att'] = att
        y_c = _mm(att, (wg['w_attn_out'], 'plain', l), mode='nn', name="attn_out")
        s['y_a'], s['y_b'], s['y_c'] = y_a, y_b, y_c
        merged = _rowwise(lambda pg, a, b, c, gb: (_f_merge(pg, a, b, c, gb), ()),
                          [(pr['g'], 0, 3 * D_MODEL), (y_a, 0, D_MODEL), (y_b, 0, D_MODEL), (y_c, 0, D_MODEL)], [full['gate_bias'][l]],
                          [(D_MODEL, BF16)], tr=tr, name="merge")[0]
        s['merged'] = merged
        h = _mm(merged, (wg['w_mix_out'], 'plain', l), mode='nn', add=h, name="mix_out")
        s['h1'] = h
        un = _rowwise(lambda a, g: (_f_rms(a, g), ()), [(h, 0, D_MODEL)], [vec('norm_xattn', l)], [(D_MODEL, BF16)], tr=tr, name="rms_xattn")[0]
        memn = _rowwise(lambda a, g: (_f_rms(a, g), ()), [(mem2, 0, D_MODEL)], [vec('norm_mem', l)], [(D_MODEL, BF16)], tr=MEM_LEN, name="rms_mem")[0]
        s['un'], s['memn'] = un, memn
        xq = _mm(un, (wg['w_xq'], 'plain', l), mode='nn', name="xq")
        kv = _mm(memn, (wg['w_xkv'], 'col', l), mode='nn', name="xkv", tn=256)
        s['xq'], s['kv'] = xq, kv
        xo = _rowwise(lambda q, kvv, qg, kg: (_f_xattn(q, kvv, qg, kg), ()), [(xq, 0, D_MODEL)], [kv, vec('xattn_q_norm', l), vec('xattn_k_norm', l)],
                      [(D_MODEL, BF16)], tr=tr, name="xattn")[0]
        s['xo'] = xo
        h = _mm(xo, (wg['w_xo'], 'plain', l), mode='nn', add=h, name="xattn_out")
        s['h2'] = h
        um = _rowwise(lambda a, g: (_f_rms(a, g), ()), [(h, 0, D_MODEL)], [vec('norm_mlp', l)], [(D_MODEL, BF16)], tr=tr, name="rms_mlp")[0]
        s['um'] = um
        up = _mm(um, (wg['w_mlp_up'], 'col', l), mode='nn', name="mlp_up", tn=512)
        s['up'] = up
        act = _rowwise(lambda a: (_f_relu2(a), ()), [(up, 0, MLP_HIDDEN)], [], [(MLP_HIDDEN, BF16)], tr=tr, name="relu2")[0]
        s['act'] = act
        h = _mm(act, (wg['w_mlp_down'], 'plain', l), mode='nn', add=h, name="mlp_down")
        saved.append(s)

    dh, loss_part = _rowwise(_f_loss, [(h, 0, D_MODEL), (tgt, 0, D_MODEL)], [], [(D_MODEL, F32)], [(1, LANES)], tr=tr, name="loss")

    sg = {n: [None] * DEPTH for n in SMALL if n != 'rel_table'}
    big_grads = [None] * DEPTH
    dbias_layers = [None] * DEPTH
    for l in reversed(range(DEPTH)):
        s = saved[l]
        bg = {}
        dact = _mm(dh, (wg['w_mlp_down'], 'plain', l), mode='nt', name="d_act")
        bg['w_mlp_down'] = _mm(s['act'], dh, mode='tn', out_dtype=BF16, name="dw_mlp_down")
        dup = _rowwise(lambda a, d: (_vjp_rows(_f_relu2, 1, 1)(a, d)[0], ()), [(s['up'], 0, MLP_HIDDEN), (dact, 0, MLP_HIDDEN)], [],
                       [(MLP_HIDDEN, BF16)], tr=tr, name="d_relu2")[0]
        bg['w_mlp_up'] = _mm(s['um'], dup, mode='tn', out_dtype=BF16, out_col=True, name="dw_mlp_up", tn=512)
        dum = _mm(dup, (wg['w_mlp_up'], 'col', l), mode='nt', name="d_um", tk=512)
        dh, dg = _rms_bwd_call(s['h2'], vec('norm_mlp', l), dum, dh, tr, "d_rms_mlp")
        sg['norm_mlp'][l] = dg
        dxo = _mm(dh, (wg['w_xo'], 'plain', l), mode='nt', out_dtype=BF16, name="d_xo")
        bg['w_xo'] = _mm(s['xo'], dh, mode='tn', out_dtype=BF16, name="dw_xo")
        qg, kg = vec('xattn_q_norm', l), vec('xattn_k_norm', l)
        dxq, dkv, dqg, dkg = _xattn_bwd_call(s['xq'], s['kv'], qg, kg, dxo, tr)
        sg['xattn_q_norm'][l], sg['xattn_k_norm'][l] = dqg, dkg
        bg['w_xq'] = _mm(s['un'], dxq, mode='tn', out_dtype=BF16, name="dw_xq")
        dun = _mm(dxq, (wg['w_xq'], 'plain', l), mode='nt', name="d_un")
        dh, dg = _rms_bwd_call(s['h1'], vec('norm_xattn', l), dun, dh, tr, "d_rms_xattn")
        sg['norm_xattn'][l] = dg
        bg['w_xkv'] = _mm(s['memn'], dkv, mode='tn', out_dtype=BF16, out_col=True, name="dw_xkv", tn=256)
        dmemn = _mm(dkv, (wg['w_xkv'], 'col', l), mode='nt', name="d_memn", tk=256)
        _, dg = _rms_bwd_call(mem2, vec('norm_mem', l), dmemn, jnp.zeros_like(mem2), MEM_LEN, "d_rms_mem")
        sg['norm_mem'][l] = dg
        dmerged = _mm(dh, (wg['w_mix_out'], 'plain', l), mode='nt', out_dtype=BF16, name="d_merged")
        bg['w_mix_out'] = _mm(s['merged'], dh, mode='tn', out_dtype=BF16, name="dw_mix_out")
        pr = s['pr']
        gb = full['gate_bias'][l]
        dpg, dya, dyb, dyc, dgb = _merge_bwd_call(pr['g'], s['y_a'], s['y_b'], s['y_c'], gb, dmerged, tr)
        sg['gate_bias'][l] = dgb
        dseg = {'g': dpg}
        datt = _mm(dyc, (wg['w_attn_out'], 'plain', l), mode='nt', out_dtype=BF16, name="d_att")
        bg['w_attn_out'] = _mm(s['att'], dyc, mode='tn', out_dtype=BF16, name="dw_attn_out")
        dqkv, dbias_l, dsk, dqn, dkn = _attn_bwd(pr['qkv'], datt, bias, vec('attn_sinks', l), vec('attn_q_norm', l), vec('attn_k_norm', l), name="d_swa")
        dseg['qkv'] = dqkv
        dbias_layers[l] = dbias_l
        sg['attn_sinks'][l], sg['attn_q_norm'][l], sg['attn_k_norm'][l] = dsk, dqn, dkn
        dyb_in = _mm(dyb, (wg['w_ssd_out'], 'plain', l), mode='nt', out_dtype=BF16, name="d_yb_in")
        bg['w_ssd_out'] = _mm(s['yb_in'], dyb, mode='tn', out_dtype=BF16, name="dw_ssd_out")
        ng = vec('ssd_norm_g', l)
        dy_ssd, dz, dng = _ssdgate_bwd_call(s['y_ssd'], pr['z'], ng, dyb_in, tr)
        sg['ssd_norm_g'][l] = dng
        dseg['z'] = dz
        ssd_vecs = [vec('ssd_dt_bias', l, LANES), vec('ssd_A_log', l, LANES), vec('ssd_D', l, LANES)]
        dxs, dbm, dcm, ddt, dtb, dal, ddsk = _ssd_bwd(s['xbc'], pr['dt'], s['states'], dy_ssd, *ssd_vecs, name="d_ssd")
        dseg['dt'] = ddt
        sg['ssd_dt_bias'][l], sg['ssd_A_log'][l], sg['ssd_D'][l] = dtb[:, :SSD_HEADS], dal[:, :SSD_HEADS], ddsk[:, :SSD_HEADS]
        nxb = SSD_INNER // LANES
        dxbc, dcw, dcb = _conv_bwd([(pr['x'], 0)], full['ssd_conv_w'][l], vec('ssd_conv_b', l),
                                   [(dxs, 0, nxb), (dbm, nxb, SSD_GROUPS), (dcm, nxb + SSD_GROUPS, SSD_GROUPS)],
                                   kk=SSD_CONV, pre_glu=False, post_silu=True, name="d_conv4", dx_dtype=BF16)
        dseg['x'] = dxbc
        sg['ssd_conv_w'][l], sg['ssd_conv_b'][l] = dcw, dcb
        dya_in = _mm(dya, (wg['w_conv_out'], 'plain', l), mode='nt', out_dtype=BF16, name="d_ya_in")
        bg['w_conv_out'] = _mm(s['ya_in'], dya, mode='tn', out_dtype=BF16, name="dw_conv_out")
        lg, lb = vec('conv_ln_g', l), vec('conv_ln_b', l)
        dca, dlg, dlb = _lnsilu_bwd_call(s['ca'], lg, lb, dya_in, tr)
        sg['conv_ln_g'][l], sg['conv_ln_b'][l] = dlg, dlb
        da, dgate, dww, dwb = _conv_bwd([(pr['a'], 0), (pr['a'], D_MODEL // LANES)], full['conv_dw_w'][l], vec('conv_dw_b', l),
                                        [(dca, 0, D_MODEL // LANES)], kk=CONV_KERNEL, pre_glu=True, post_silu=False, name="d_conv31", dx_dtype=BF16)
        dseg['a'] = jnp.concatenate([da, dgate], axis=-1)
        sg['conv_dw_w'][l], sg['conv_dw_b'][l] = dww, dwb
        du = None
        dw_parts = []
        for k in seg_order:
            du = _mm(dseg[k], (w_seg[k], 'plain', l), mode='nt', add=du, name="d_u_" + k)
            dw_parts.append(_mm(s['u'], dseg[k], mode='tn', out_dtype=BF16, name="dw_in_" + k))
        dw_parts[3] = dw_parts[3][:, :SSD_HEADS]
        dw_in = jnp.concatenate(dw_parts, axis=-1)
        bg['w_in'] = jnp.transpose(dw_in.reshape(D_MODEL, N_DEV, IN_COLS // N_DEV), (1, 0, 2))
        dh, dg = _rms_bwd_call(s['h0'], vec('norm_mix', l), du, dh, tr, "d_rms_mix")
        sg['norm_mix'][l] = dg
        for n in ROW_SHARDED:
            g_ = bg[n]
            bg[n] = g_.reshape(N_DEV, g_.shape[0] // N_DEV, g_.shape[1])
        big_grads[l] = bg

    grad_x = dh.reshape(x.shape)
    d_rel = jnp.transpose(_band_bias_bwd(jnp.stack(dbias_layers).reshape(DEPTH, ATTN_HEADS, -1), buckets))

    small_full = {'rel_table': d_rel}
    for n in SMALL:
        if n != 'rel_table':
            small_full[n] = jnp.stack(sg[n]).reshape((DEPTH,) + (full[n].shape[1:] if n in SMALL_SHARDED else p[n].shape[1:]))
    small_shapes = [(1, LANES)] + [small_full[n].shape for n in SMALL]
    packed = _pack([loss_part] + [small_full[n] for n in SMALL])
    slots = _all_gather([packed], [0], name="gather_small_grads")[0]
    rows = packed.shape[0]
    reduced = _rowwise(_f_sum_slots, [(slots, 0, LANES)], [], [(LANES, F32)], tr=_row_tile(rows, LANES, 12), name="sum_small")[0]
    red = _unpack(reduced, small_shapes)
    loss = red[0][0, 0]
    small_grad = {}
    for n, g_ in zip(SMALL, red[1:]):
        if n in SMALL_SHARDED:
            wdt = p[n].shape[-1]
            g_ = lax.dynamic_slice_in_dim(g_, my * wdt, wdt, axis=g_.ndim - 1)
        small_grad[n] = g_
    local_shapes = [p[n].shape for n in SMALL]
    pk = lambda d: _pack([d[n] for n in SMALL])
    pg_, pw_, pm_, pv_ = pk(small_grad), pk(p), pk(m), pk(v)
    srows = pg_.shape[0]
    sd, sm, sv = _rowwise(_f_adam, [(pg_, 0, LANES), (pw_, 0, LANES), (pm_, 0, LANES), (pv_, 0, LANES)], [],
                          [(LANES, F32)] * 3, tr=_row_tile(srows, LANES, 16), name="adam_small")
    out_delta = dict(zip(SMALL, _unpack(sd, local_shapes)))
    out_m = dict(zip(SMALL, _unpack(sm, local_shapes)))
    out_v = dict(zip(SMALL, _unpack(sv, local_shapes)))
    out_grad = dict(small_grad)

    per_layer = {n: [] for n in big_names}
    for l in range(DEPTH):
        recv = _exchange([big_grads[l][n] for n in big_names], name="exchange_grads")
        for n, r in zip(big_names, recv):
            rws, wdt = r.shape[1], r.shape[2]
            tr_w = _row_tile(rws, wdt, 24)
            outs = _rowwise(_f_adam_slots, [(r, 0, wdt), (p[n], 0, wdt, l), (m[n], 0, wdt, l), (v[n], 0, wdt, l)], [],
                            [(wdt, F32)] * 4, tr=tr_w, name="adam_" + n)
            per_layer[n].append(outs)
    for n in big_names:
        for k, dst in enumerate((out_grad, out_delta, out_m, out_v)):
            dst[n] = jnp.stack([per_layer[n][l][k] for l in range(DEPTH)])
    return (loss, grad_x, *[out_grad[n] for n in WEIGHTS], *[out_delta[n] for n in WEIGHTS],
            *[out_m[n] for n in WEIGHTS], *[out_v[n] for n in WEIGHTS])


def _rms_bwd_call(h, g, du, dres, tr, name):
    return _rowwise(lambda a, d, r, gg: _f_rms_bwd(a, gg, d, r), [(h, 0, D_MODEL), (du, 0, D_MODEL), (dres, 0, D_MODEL)], [g],
                    [(D_MODEL, F32)], [g.shape], tr=tr, name=name)


def _xattn_bwd_call(xq, kv, qg, kg, dxo, tr):
    def f(q, d, kvv, qgv, kgv):
        return _vjp_rows(lambda a, b, c, e: _f_xattn(a, b, c, e), 4, 1)(q, kvv, qgv, kgv, d)
    return _rowwise(f, [(xq, 0, D_MODEL), (dxo, 0, D_MODEL)], [kv, qg, kg], [(D_MODEL, BF16)], [kv.shape, qg.shape, kg.shape], tr=tr, name="d_xattn")


def _merge_bwd_call(pg, ya, yb, yc, gb, dmerged, tr):
    def f(a, b, c, e, d, gbv):
        return _vjp_rows(_f_merge, 5, 4)(a, b, c, e, gbv, d)
    return _rowwise(f, [(pg, 0, 3 * D_MODEL), (ya, 0, D_MODEL), (yb, 0, D_MODEL), (yc, 0, D_MODEL), (dmerged, 0, D_MODEL)], [gb],
                    [(3 * D_MODEL, BF16)] + [(D_MODEL, BF16)] * 3, [gb.shape], tr=tr, name="d_merge")


def _ssdgate_bwd_call(y, z, ng, dy, tr):
    def f(a, b, d, g):
        return _vjp_rows(_f_ssdgate, 3, 2)(a, b, g, d)
    return _rowwise(f, [(y, 0, SSD_INNER), (z, 0, SSD_INNER), (dy, 0, SSD_INNER)], [ng], [(SSD_INNER, F32), (SSD_INNER, BF16)], [ng.shape], tr=tr, name="d_ssd_gate")


def _lnsilu_bwd_call(ca, lg, lb, dy, tr):
    def f(a, d, g, b):
        return _vjp_rows(_f_lnsilu, 3, 1)(a, g, b, d)
    return _rowwise(f, [(ca, 0, D_MODEL), (dy, 0, D_MODEL)], [lg, lb], [(D_MODEL, F32)], [lg.shape, lb.shape], tr=tr, name="d_ln_silu")


def kernel(x, mem, rel_table, norm_mix, w_in, gate_bias, conv_dw_w, conv_dw_b, conv_ln_g, conv_ln_b, w_conv_out, ssd_conv_w, ssd_conv_b, ssd_dt_bias, ssd_A_log, ssd_D, ssd_norm_g, w_ssd_out, attn_q_norm, attn_k_norm, attn_sinks, w_attn_out, w_mix_out, norm_xattn, norm_mem, w_xq, w_xkv, xattn_q_norm, xattn_k_norm, w_xo, norm_mlp, w_mlp_up, w_mlp_down, loss_target, m_rel_table, m_norm_mix, m_w_in, m_gate_bias, m_conv_dw_w, m_conv_dw_b, m_conv_ln_g, m_conv_ln_b, m_w_conv_out, m_ssd_conv_w, m_ssd_conv_b, m_ssd_dt_bias, m_ssd_A_log, m_ssd_D, m_ssd_norm_g, m_w_ssd_out, m_attn_q_norm, m_attn_k_norm, m_attn_sinks, m_w_attn_out, m_w_mix_out, m_norm_xattn, m_norm_mem, m_w_xq, m_w_xkv, m_xattn_q_norm, m_xattn_k_norm, m_w_xo, m_norm_mlp, m_w_mlp_up, m_w_mlp_down, v_rel_table, v_norm_mix, v_w_in, v_gate_bias, v_conv_dw_w, v_conv_dw_b, v_conv_ln_g, v_conv_ln_b, v_w_conv_out, v_ssd_conv_w, v_ssd_conv_b, v_ssd_dt_bias, v_ssd_A_log, v_ssd_D, v_ssd_norm_g, v_w_ssd_out, v_attn_q_norm, v_attn_k_norm, v_attn_sinks, v_w_attn_out, v_w_mix_out, v_norm_xattn, v_norm_mem, v_w_xq, v_w_xkv, v_xattn_q_norm, v_xattn_k_norm, v_w_xo, v_norm_mlp, v_w_mlp_up, v_w_mlp_down):
    args = locals()
    p = {n: args[n] for n in WEIGHTS}
    m = {n: args["m_" + n] for n in WEIGHTS}
    v = {n: args["v_" + n] for n in WEIGHTS}
    return _step(p, m, v, x, mem, loss_target)
```

```python
import functools
import math

import numpy as np
import jax
import jax.numpy as jnp
from jax import lax
from jax.experimental import pallas as pl
from jax.experimental.pallas import tpu as pltpu

F32 = jnp.float32
BF16 = jnp.bfloat16
HI = lax.Precision.HIGHEST
MESH = pl.DeviceIdType.MESH

N_DEV = 8
D_MODEL = 1024
DEPTH = 4
MEM_LEN = 256
EPS = 1e-6
NEG_INF = -1e30
CONV_KERNEL = 31
SSD_INNER = 2048
SSD_HEAD_DIM = 64
SSD_HEADS = 32
SSD_GROUPS = 4
SSD_STATE = 128
SSD_CONV = 4
SSD_CHUNK = 128
SSD_XBC = SSD_INNER + 2 * SSD_GROUPS * SSD_STATE
HEADS_PER_GROUP = SSD_HEADS // SSD_GROUPS
ATTN_HEADS = 16
ATTN_KV_HEADS = 4
ATTN_HEAD_DIM = 64
ATTN_BLOCK = 128
ATTN_REP = ATTN_HEADS // ATTN_KV_HEADS
REL_BUCKETS = 32
REL_MAX_DIST = 128
XATTN_HEADS = 4
XATTN_HEAD_DIM = 256
MLP_HIDDEN = 4096
OFF_Z = 2048
OFF_XBC = 4096
OFF_DT = 7168
OFF_Q = 7200
OFF_GATE = 8736
IN_COLS = 11808
LANES = 128
CONV_PAD = 32

ADAM_LR, ADAM_B1, ADAM_B2, ADAM_EPS, ADAM_WD, ADAM_STEP = 0.001, 0.9, 0.999, 1e-08, 0.01, 10

WEIGHTS = ['rel_table', 'norm_mix', 'w_in', 'gate_bias', 'conv_dw_w', 'conv_dw_b', 'conv_ln_g', 'conv_ln_b', 'w_conv_out',
           'ssd_conv_w', 'ssd_conv_b', 'ssd_dt_bias', 'ssd_A_log', 'ssd_D', 'ssd_norm_g', 'w_ssd_out', 'attn_q_norm',
           'attn_k_norm', 'attn_sinks', 'w_attn_out', 'w_mix_out', 'norm_xattn', 'norm_mem', 'w_xq', 'w_xkv', 'xattn_q_norm',
           'xattn_k_norm', 'w_xo', 'norm_mlp', 'w_mlp_up', 'w_mlp_down']
ROW_SHARDED = ['w_conv_out', 'w_ssd_out', 'w_attn_out', 'w_mix_out', 'w_xq', 'w_xo', 'w_mlp_down']
COL_SHARDED = ['w_in', 'w_xkv', 'w_mlp_up']
BIG = ROW_SHARDED + COL_SHARDED
SMALL_SHARDED = ['gate_bias', 'conv_dw_w', 'ssd_conv_w']
SMALL = [n for n in WEIGHTS if n not in BIG]


def _bdot(a, b, dims):
    return lax.dot_general(a.astype(BF16), b.astype(BF16), (dims, ((), ())), preferred_element_type=F32)


def _hdot(a, b, dims):
    return lax.dot_general(a, b, (dims, ((), ())), precision=HI, preferred_element_type=F32)


def _pick(dim, pref):
    if dim <= pref:
        return dim
    t = (pref // LANES) * LANES
    while dim % t:
        t -= LANES
    return t


def _logical(op):
    arr, kind, _ = op
    r, c = arr.shape[-2:]
    return (r, c * N_DEV) if kind == 'col' else (r, c)


def _opspec(op, br, bc, rc):
    arr, kind, layer = op
    lead = () if layer is None else (layer,)
    none = (None,) * len(lead)
    if kind == 'plain':
        return pl.BlockSpec(none + (br, bc), lambda i, j, k: lead + rc(i, j, k))
    per = arr.shape[-1] // bc

    def imap(i, j, k):
        r, c = rc(i, j, k)
        if per == 1:
            return lead + (c, r, 0)
        return lead + (lax.div(c, per), r, lax.rem(c, per))
    return pl.BlockSpec(none + (None, br, bc), imap)


def _mm(a, b, *, mode, name, add=None, out_dtype=F32, out_col=False, tm=512, tn=1024, tk=1024):
    def operand(op):
        op = op if isinstance(op, tuple) else (op, 'plain', None)
        return (op[0][op[2]], op[1], None) if isinstance(op[0], list) else op

    a, b = operand(a), operand(b)
    ar, ac = _logical(a)
    br_, bc_ = _logical(b)
    if mode == 'nn':
        m, kd, n = ar, ac, bc_
        assert br_ == kd
    elif mode == 'nt':
        m, kd, n = ar, ac, br_
        assert bc_ == kd
    else:
        m, kd, n = ac, ar, bc_
        assert br_ == kd

    def lim(op, is_col_dim):
        return op[0].shape[-1] if (op[1] == 'col' and is_col_dim) else 1 << 30

    tm = _pick(m, min(tm, lim(a, mode == 'tn')))
    tn = _pick(n, min(tn, lim(b, mode != 'nt'), (n // N_DEV) if out_col else 1 << 30))
    tk = _pick(kd, min(tk, lim(a, mode != 'tn'), lim(b, mode == 'nt')))
    nk = kd // tk
    if mode == 'nn':
        a_spec = _opspec(a, tm, tk, lambda i, j, k: (i, k))
        b_spec = _opspec(b, tk, tn, lambda i, j, k: (k, j))
        dims = ((1,), (0,))
    elif mode == 'nt':
        a_spec = _opspec(a, tm, tk, lambda i, j, k: (i, k))
        b_spec = _opspec(b, tn, tk, lambda i, j, k: (j, k))
        dims = ((1,), (1,))
    else:
        a_spec = _opspec(a, tk, tm, lambda i, j, k: (k, i))
        b_spec = _opspec(b, tk, tn, lambda i, j, k: (k, j))
        dims = ((0,), (0,))
    if out_col:
        out_shape = jax.ShapeDtypeStruct((N_DEV, m, n // N_DEV), out_dtype)
        out_spec = _opspec((out_shape, 'col', None), tm, tn, lambda i, j, k: (i, j))
    else:
        out_shape = jax.ShapeDtypeStruct((m, n), out_dtype)
        out_spec = pl.BlockSpec((tm, tn), lambda i, j, k: (i, j))
    has_add = add is not None

    def body(*refs):
        a_ref, b_ref = refs[0], refs[1]
        add_ref = refs[2] if has_add else None
        o_ref = refs[2 + has_add]
        part = _bdot(a_ref[...], b_ref[...], dims)
        if nk == 1:
            o_ref[...] = (part + add_ref[...].astype(F32) if has_add else part).astype(o_ref.dtype)
            return
        acc_ref = refs[3 + has_add]
        k = pl.program_id(2)

        @pl.when(k == 0)
        def _():
            acc_ref[...] = part + add_ref[...].astype(F32) if has_add else part

        @pl.when(k > 0)
        def _():
            acc_ref[...] += part

        @pl.when(k == nk - 1)
        def _():
            o_ref[...] = acc_ref[...].astype(o_ref.dtype)

    in_specs = [a_spec, b_spec]
    args = [a[0], b[0]]
    if has_add:
        in_specs.append(pl.BlockSpec((tm, tn), lambda i, j, k: (i, j)))
        args.append(add)
    return pl.pallas_call(
        body, name=name, grid=(m // tm, n // tn, nk), in_specs=in_specs, out_specs=out_spec, out_shape=out_shape,
        scratch_shapes=[pltpu.VMEM((tm, tn), F32)] if nk > 1 else [],
        compiler_params=pltpu.CompilerParams(dimension_semantics=("parallel", "parallel", "arbitrary")),
    )(*args)


def _rowwise(f, rows, consts, row_outs, acc_outs=(), *, tr, name):
    nr, nc, nro = len(rows), len(consts), len(row_outs)
    first = rows[0][0]
    t = first.shape[-2]
    assert t % tr == 0
    in_specs = []
    for spec in rows:
        arr, cb, w = spec[:3]
        lead = spec[3] if len(spec) > 3 else None
        if arr.ndim == 2:
            in_specs.append(pl.BlockSpec((tr, w), functools.partial(lambda i, cb: (i, cb), cb=cb)))
        elif lead is not None:
            in_specs.append(pl.BlockSpec((None, tr, w), functools.partial(lambda i, cb, lead: (lead, i, cb), cb=cb, lead=lead)))
        else:
            in_specs.append(pl.BlockSpec((arr.shape[0], tr, w), functools.partial(lambda i, cb: (0, i, cb), cb=cb)))
    for cst in consts:
        in_specs.append(pl.BlockSpec(cst.shape, functools.partial(lambda i, nd: (0,) * nd, nd=cst.ndim)))
    out_specs = [pl.BlockSpec((tr, w), lambda i: (i, 0)) for w, _ in row_outs]
    out_shape = [jax.ShapeDtypeStruct((t, w), dt) for w, dt in row_outs]
    for shp in acc_outs:
        out_specs.append(pl.BlockSpec(shp, functools.partial(lambda i, nd: (0,) * nd, nd=len(shp))))
        out_shape.append(jax.ShapeDtypeStruct(shp, F32))

    def body(*refs):
        ins = [r[...] for r in refs[:nr + nc]]
        ro = refs[nr + nc:nr + nc + nro]
        ao = refs[nr + nc + nro:]
        outs, accs = f(*ins)
        for o_ref, o in zip(ro, outs):
            o_ref[...] = o.astype(o_ref.dtype)
        if ao:
            i = pl.program_id(0)

            @pl.when(i == 0)
            def _():
                for a_ref, acc in zip(ao, accs):
                    a_ref[...] = acc

            @pl.when(i > 0)
            def _():
                for a_ref, acc in zip(ao, accs):
                    a_ref[...] += acc

    res = pl.pallas_call(
        body, name=name, grid=(t // tr,), in_specs=in_specs, out_specs=out_specs, out_shape=out_shape,
        compiler_params=pltpu.CompilerParams(dimension_semantics=("arbitrary",)),
    )(*[s[0] for s in rows], *consts)
    return res


def _vjp_rows(f, n_prim, n_rows_grad):
    def g(*args):
        prim, cots = args[:n_prim], args[n_prim:]
        outs, vjp = jax.vjp(f, *prim)
        grads = vjp(tuple(c.astype(o.dtype) for c, o in zip(cots, outs)))
        return tuple(grads[:n_rows_grad]), tuple(grads[n_rows_grad:])
    return g


def _rms(x, g):
    return x * lax.rsqrt(jnp.mean(x * x, axis=-1, keepdims=True) + EPS) * g


def _f_rms(h, g):
    return (_rms(h, g),)


def _f_rms_bwd(h, g, du, dres):
    _, vjp = jax.vjp(_f_rms, h, g)
    dh, dg = vjp((du.astype(F32),))
    return (dh + dres,), (dg,)


def _f_lnsilu(x, g, b):
    mu = jnp.mean(x, axis=-1, keepdims=True)
    xc = x - mu
    y = xc * lax.rsqrt(jnp.mean(xc * xc, axis=-1, keepdims=True) + EPS) * g + b
    return (jax.nn.silu(y),)


def _f_ssdgate(y, z, g):
    y = y * jax.nn.silu(z)
    w = SSD_INNER // SSD_GROUPS
    return (jnp.concatenate([_rms(y[:, i * w:(i + 1) * w], g[:, i * w:(i + 1) * w]) for i in range(SSD_GROUPS)], axis=-1),)


def _f_merge(pg, ya, yb, yc, gb):
    out = 0.0
    for i, yi in enumerate((ya, yb, yc)):
        out = out + jax.nn.sigmoid(pg[:, i * D_MODEL:(i + 1) * D_MODEL] + gb[i:i + 1, :]) * yi
    return (out,)


def _f_relu2(a):
    return (jnp.square(jnp.maximum(a, 0.0)),)


def _f_xattn(q, kv, qg, kg):
    outs = []
    for h in range(XATTN_HEADS):
        sl = slice(h * XATTN_HEAD_DIM, (h + 1) * XATTN_HEAD_DIM)
        qh = _rms(q[:, sl], qg)
        kh = _rms(kv[:, sl], kg)
        vh = kv[:, D_MODEL + h * XATTN_HEAD_DIM:D_MODEL + (h + 1) * XATTN_HEAD_DIM]
        s = _bdot(qh, kh, ((1,), (1,))) * (XATTN_HEAD_DIM ** -0.5)
        p = jnp.exp(s - jnp.max(s, axis=-1, keepdims=True))
        p = p * (1.0 / jnp.sum(p, axis=-1, keepdims=True))
        outs.append(_bdot(p, vh, ((1,), (0,))))
    return (jnp.concatenate(outs, axis=-1),)


def _f_loss(y, tgt):
    err = y - tgt
    per_row = jnp.sum(err * err, axis=-1, keepdims=True) * (0.5 / D_MODEL)
    loss = jnp.sum(per_row, axis=0, keepdims=True)
    return (err * (1.0 / D_MODEL),), (jnp.broadcast_to(loss, (1, LANES)),)


def _adam_core(w, g, m, v):
    m = ADAM_B1 * m + (1.0 - ADAM_B1) * g
    v = ADAM_B2 * v + (1.0 - ADAM_B2) * jnp.square(g)
    m_hat = m / (1.0 - ADAM_B1 ** ADAM_STEP)
    v_hat = v / (1.0 - ADAM_B2 ** ADAM_STEP)
    delta = -ADAM_LR * (m_hat / (jnp.sqrt(v_hat) + ADAM_EPS) + ADAM_WD * w)
    return delta, m, v


def _sum_slots(g8):
    g = g8[0].astype(F32)
    for s in range(1, N_DEV):
        g = g + g8[s].astype(F32)
    return g


def _f_adam_slots(g8, w, m, v):
    g = _sum_slots(g8)
    return (g,) + _adam_core(w, g, m, v), ()


def _f_sum_slots(g8):
    return (_sum_slots(g8),), ()


def _f_adam(g, w, m, v):
    return _adam_core(w, g, m, v), ()


def _conv_chunk(t):
    return 256 if t % 256 == 0 else t


def _conv_fwd(srcs, w, b, *, kk, pre_glu, post_silu, name):
    t = srcs[0][0].shape[0]
    c = w.shape[-1]
    tt = _conv_chunk(t)
    ns = len(srcs)

    def body(*refs):
        w_ref, b_ref, o_ref, pad_ref = refs[ns:]
        if pre_glu:
            xin = refs[0][...] * jax.nn.sigmoid(refs[1][...])
        else:
            xin = refs[0][...]
        pad_ref[0:CONV_PAD, :] = jnp.zeros((CONV_PAD, LANES), F32)
        pad_ref[CONV_PAD:, :] = xin

        def chunk(i, carry):
            base = pl.multiple_of(i * tt, tt)
            acc = jnp.broadcast_to(b_ref[...], (tt, LANES))
            for j in range(kk):
                acc = acc + pad_ref[pl.ds(base + CONV_PAD - (kk - 1) + j, tt), :] * w_ref[j:j + 1, :]
            o_ref[pl.ds(base, tt), :] = jax.nn.silu(acc) if post_silu else acc
            return carry
        lax.fori_loop(0, t // tt, chunk, 0)

    in_specs = [pl.BlockSpec((t, LANES), functools.partial(lambda i, off: (0, off + i), off=off)) for _, off in srcs]
    in_specs += [pl.BlockSpec((kk, LANES), lambda i: (0, i)), pl.BlockSpec((1, LANES), lambda i: (0, i))]
    return pl.pallas_call(
        body, name=name, grid=(c // LANES,), in_specs=in_specs, out_specs=pl.BlockSpec((t, LANES), lambda i: (0, i)),
        out_shape=jax.ShapeDtypeStruct((t, c), F32), scratch_shapes=[pltpu.VMEM((t + CONV_PAD, LANES), F32)],
        compiler_params=pltpu.CompilerParams(dimension_semantics=("parallel",)),
    )(*[s[0] for s in srcs], w, b)


def _conv_bwd(srcs, w, b, dys, *, kk, pre_glu, post_silu, name, dx_dtype):
    t = srcs[0][0].shape[0]
    c = w.shape[-1]
    tt = _conv_chunk(t)
    ns, nd = len(srcs), len(dys)

    def body(*refs):
        src_refs = refs[:ns]
        dy_refs = refs[ns:ns + nd]
        w_ref, b_ref = refs[ns + nd:ns + nd + 2]
        outs = refs[ns + nd + 2:]
        dx_refs, dw_ref, db_ref = outs[:ns], outs[ns], outs[ns + 1]
        pad_ref, dpad_ref = outs[ns + 2:]
        cb = pl.program_id(0)
        if pre_glu:
            a_in = src_refs[0][...]
            sg = jax.nn.sigmoid(src_refs[1][...])
            xin = a_in * sg
        else:
            xin = src_refs[0][...]
        pad_ref[0:CONV_PAD, :] = jnp.zeros((CONV_PAD, LANES), F32)
        pad_ref[CONV_PAD:, :] = xin
        dpad_ref[t:, :] = jnp.zeros((CONV_PAD, LANES), F32)
        dw_ref[...] = jnp.zeros_like(dw_ref)
        db_ref[...] = jnp.zeros_like(db_ref)

        def load_dy(base):
            dy = dy_refs[0][pl.ds(base, tt), :].astype(F32)
            for (_, first, _n), r in zip(dys[1:], dy_refs[1:]):
                dy = jnp.where(cb >= first, r[pl.ds(base, tt), :].astype(F32), dy)
            return dy

        def chunk1(i, carry):
            base = pl.multiple_of(i * tt, tt)
            dy = load_dy(base)
            if post_silu:
                acc = jnp.broadcast_to(b_ref[...], (tt, LANES))
                for j in range(kk):
                    acc = acc + pad_ref[pl.ds(base + CONV_PAD - (kk - 1) + j, tt), :] * w_ref[j:j + 1, :]
                s = jax.nn.sigmoid(acc)
                dy = dy * (s * (1.0 + acc * (1.0 - s)))
            dpad_ref[pl.ds(base, tt), :] = dy
            db_ref[...] += jnp.sum(dy, axis=0, keepdims=True)
            for j in range(kk):
                dw_ref[j:j + 1, :] += jnp.sum(dy * pad_ref[pl.ds(base + CONV_PAD - (kk - 1) + j, tt), :], axis=0, keepdims=True)
            return carry
        lax.fori_loop(0, t // tt, chunk1, 0)

        def chunk2(i, carry):
            base = pl.multiple_of(i * tt, tt)
            acc = jnp.zeros((tt, LANES), F32)
            for j in range(kk):
                acc = acc + dpad_ref[pl.ds(base + (kk - 1) - j, tt), :] * w_ref[j:j + 1, :]
            if pre_glu:
                a_c = src_refs[0][pl.ds(base, tt), :]
                s_c = jax.nn.sigmoid(src_refs[1][pl.ds(base, tt), :])
                dx_refs[0][pl.ds(base, tt), :] = (acc * s_c).astype(dx_dtype)
                dx_refs[1][pl.ds(base, tt), :] = (acc * a_c * s_c * (1.0 - s_c)).astype(dx_dtype)
            else:
                dx_refs[0][pl.ds(base, tt), :] = acc.astype(dx_dtype)
            return carry
        lax.fori_loop(0, t // tt, chunk2, 0)

    in_specs = [pl.BlockSpec((t, LANES), functools.partial(lambda i, off: (0, off + i), off=off)) for _, off in srcs]
    for _, first, n in dys:
        in_specs.append(pl.BlockSpec((t, LANES), functools.partial(lambda i, first, n: (0, jnp.clip(i - first, 0, n - 1)), first=first, n=n)))
    in_specs += [pl.BlockSpec((kk, LANES), lambda i: (0, i)), pl.BlockSpec((1, LANES), lambda i: (0, i))]
    out_specs = [pl.BlockSpec((t, LANES), lambda i: (0, i)) for _ in srcs]
    out_specs += [pl.BlockSpec((kk, LANES), lambda i: (0, i)), pl.BlockSpec((1, LANES), lambda i: (0, i))]
    out_shape = [jax.ShapeDtypeStruct((t, c), dx_dtype) for _ in srcs]
    out_shape += [jax.ShapeDtypeStruct((kk, c), F32), jax.ShapeDtypeStruct((1, c), F32)]
    return pl.pallas_call(
        body, name=name, grid=(c // LANES,), in_specs=in_specs, out_specs=out_specs, out_shape=out_shape,
        scratch_shapes=[pltpu.VMEM((t + CONV_PAD, LANES), F32), pltpu.VMEM((t + CONV_PAD, LANES), F32)],
        compiler_params=pltpu.CompilerParams(dimension_semantics=("parallel",)),
    )(*[s[0] for s in srcs], *[d[0] for d in dys], w, b)


def _ssd_chunk(xbc, dtfull, s_in, dt_bias, a_log, dskip):
    q = xbc.shape[0]
    gw = SSD_INNER // SSD_GROUPS
    x = xbc[:, :SSD_INNER]
    dt_all = jax.nn.softplus(dtfull + dt_bias)
    da_all = dt_all * (-jnp.exp(a_log))
    row = lax.broadcasted_iota(jnp.int32, (q, q), 0)
    col = lax.broadcasted_iota(jnp.int32, (q, q), 1)
    causal = row >= col
    cs = _hdot(causal.astype(F32), da_all, ((1,), (0,)))
    cs_last = cs[q - 1:q, :]
    head_of_lane = lax.shift_right_logical(lax.broadcasted_iota(jnp.int32, (LANES, SSD_INNER), 1), int(math.log2(SSD_HEAD_DIM)))
    spread = (lax.broadcasted_iota(jnp.int32, (LANES, SSD_INNER), 0) == head_of_lane).astype(F32)
    per_head = jnp.concatenate([dt_all, jnp.exp(cs), jnp.exp(cs_last - cs), jnp.broadcast_to(dskip, (8, LANES))], axis=0)
    wide = _hdot(per_head, spread, ((1,), (0,)))
    xdt = x * wide[0:q]
    from_start, to_end, d_wide = wide[q:2 * q], wide[2 * q:3 * q], wide[3 * q:3 * q + 1]
    xdt_end = xdt * to_end
    chunk_decay = jnp.exp(cs_last)
    decay_rows = jnp.concatenate([jnp.broadcast_to(chunk_decay[:, h:h + 1], (SSD_HEAD_DIM, 1)) for h in range(SSD_HEADS)], axis=0)
    cs_t = jnp.transpose(cs)
    y_off, new_states, y_diag = [], [], []
    for g in range(SSD_GROUPS):
        bm = xbc[:, SSD_INNER + g * SSD_STATE:SSD_INNER + (g + 1) * SSD_STATE]
        cm = xbc[:, SSD_INNER + (SSD_GROUPS + g) * SSD_STATE:SSD_INNER + (SSD_GROUPS + g + 1) * SSD_STATE]
        cb = _bdot(cm, bm, ((1,), (1,)))
        y_off.append(_bdot(cm, s_in[g * gw:(g + 1) * gw, :], ((1,), (1,))))
        new_states.append(_bdot(xdt_end[:, g * gw:(g + 1) * gw], bm, ((0,), (0,))))
        for j in range(HEADS_PER_GROUP):
            h = g * HEADS_PER_GROUP + j
            diff = cs[:, h:h + 1] - cs_t[h:h + 1, :]
            decay = jnp.where(causal, jnp.exp(jnp.where(causal, diff, 0.0)), 0.0)
            y_diag.append(_bdot(cb * decay, xdt[:, h * SSD_HEAD_DIM:(h + 1) * SSD_HEAD_DIM], ((1,), (0,))))
    s_out = s_in * decay_rows + jnp.concatenate(new_states, axis=0)
    y = jnp.concatenate(y_diag, axis=-1) + jnp.concatenate(y_off, axis=-1) * from_start + x * d_wide
    return y, s_out


def _ssd_fwd(xbc, pdt, dt_bias, a_log, dskip, *, name, rider=None):
    t = xbc.shape[0]
    nc = t // SSD_CHUNK
    ride = _Ride(rider, 5, 2, 1)

    def body(*refs):
        (x_ref, dt_ref, tb_ref, al_ref, d_ref), (y_ref, s_ref), (state_ref,) = ride.split(refs)
        c = pl.program_id(0)
        ride.start(refs, c == 0)

        @pl.when(c == 0)
        def _():
            state_ref[...] = jnp.zeros_like(state_ref)
        s_in = state_ref[...]
        s_ref[...] = s_in
        y, s_out = _ssd_chunk(x_ref[...], dt_ref[...], s_in, tb_ref[...], al_ref[...], d_ref[...])
        y_ref[...] = y
        state_ref[...] = s_out
        ride.finish(refs, c == nc - 1)

    vec = pl.BlockSpec((1, LANES), lambda c: (0, 0))
    res = pl.pallas_call(
        body, name=name, grid=(nc,),
        in_specs=[pl.BlockSpec((SSD_CHUNK, SSD_XBC), lambda c: (c, 0)), pl.BlockSpec((SSD_CHUNK, LANES), lambda c: (c, 0)), vec, vec, vec] + ride.in_specs,
        out_specs=[pl.BlockSpec((SSD_CHUNK, SSD_INNER), lambda c: (c, 0)), pl.BlockSpec((None, SSD_INNER, SSD_STATE), lambda c: (c, 0, 0))] + ride.out_specs,
        out_shape=[jax.ShapeDtypeStruct((t, SSD_INNER), F32), jax.ShapeDtypeStruct((nc, SSD_INNER, SSD_STATE), F32)] + ride.out_shape,
        scratch_shapes=[pltpu.VMEM((SSD_INNER, SSD_STATE), F32)] + ride.scratch,
        compiler_params=pltpu.CompilerParams(dimension_semantics=("arbitrary",)),
    )(xbc, pdt, dt_bias, a_log, dskip, *ride.args)
    return res[:2], res[2:]


def _ssd_bwd(xbc, pdt, states, dy, dt_bias, a_log, dskip, *, name, rider=None):
    t = xbc.shape[0]
    nc = t // SSD_CHUNK
    rev = lambda c: nc - 1 - c
    ride = _Ride(rider, 7, 5, 1)

    def body(*refs):
        (x_ref, dt_ref, s_ref, dy_ref, tb_ref, al_ref, d_ref), (dx_ref, ddt_ref, dtb_ref, dal_ref, dd_ref), (dstate_ref,) = ride.split(refs)
        c = pl.program_id(0)
        ride.start(refs, c == 0)

        @pl.when(c == 0)
        def _():
            dstate_ref[...] = jnp.zeros_like(dstate_ref)
            dtb_ref[...] = jnp.zeros_like(dtb_ref)
            dal_ref[...] = jnp.zeros_like(dal_ref)
            dd_ref[...] = jnp.zeros_like(dd_ref)
        _, vjp = jax.vjp(_ssd_chunk, x_ref[...], dt_ref[...], s_ref[...], tb_ref[...], al_ref[...], d_ref[...])
        dx, ddt, ds_in, dtb, dal, dd = vjp((dy_ref[...].astype(F32), dstate_ref[...]))
        dx_ref[...] = dx
        ddt_ref[...] = ddt
        dstate_ref[...] = ds_in
        dtb_ref[...] += dtb
        dal_ref[...] += dal
        dd_ref[...] += dd
        ride.finish(refs, c == nc - 1)

    vec = pl.BlockSpec((1, LANES), lambda c: (0, 0))
    in_specs = [pl.BlockSpec((SSD_CHUNK, SSD_XBC), lambda c: (rev(c), 0)), pl.BlockSpec((SSD_CHUNK, LANES), lambda c: (rev(c), 0)),
                pl.BlockSpec((None, SSD_INNER, SSD_STATE), lambda c: (rev(c), 0, 0)), pl.BlockSpec((SSD_CHUNK, SSD_INNER), lambda c: (rev(c), 0)),
                vec, vec, vec]
    out_specs = [pl.BlockSpec((SSD_CHUNK, SSD_XBC), lambda c: (rev(c), 0)), pl.BlockSpec((SSD_CHUNK, LANES), lambda c: (rev(c), 0)), vec, vec, vec]
    out_shape = [jax.ShapeDtypeStruct((t, SSD_XBC), F32), jax.ShapeDtypeStruct((t, LANES), F32)] + [jax.ShapeDtypeStruct((1, LANES), F32)] * 3
    res = pl.pallas_call(
        body, name=name, grid=(nc,), in_specs=in_specs + ride.in_specs, out_specs=out_specs + ride.out_specs,
        out_shape=out_shape + ride.out_shape, scratch_shapes=[pltpu.VMEM((SSD_INNER, SSD_STATE), F32)] + ride.scratch,
        compiler_params=pltpu.CompilerParams(dimension_semantics=("arbitrary",)),
    )(xbc, pdt, states, dy, dt_bias, a_log, dskip, *ride.args)
    return res[:5], res[5:]


def _rel_buckets():
    qi = np.arange(ATTN_BLOCK)[:, None] + ATTN_BLOCK
    kj = np.arange(2 * ATTN_BLOCK)[None, :]
    dist = qi - kj
    max_exact = REL_BUCKETS // 2
    d = np.maximum(dist, 1).astype(np.float32)
    large = max_exact + (np.log(d / np.float32(max_exact)) / np.float32(math.log(REL_MAX_DIST / max_exact))
                         * np.float32(REL_BUCKETS - max_exact)).astype(np.int32)
    large = np.minimum(large, REL_BUCKETS - 1)
    return np.where(dist < max_exact, np.maximum(dist, 0), large).astype(np.int32)


def _onehot_buckets(bucket_ref):
    n = bucket_ref.shape[-1]
    return (lax.broadcasted_iota(jnp.int32, (REL_BUCKETS, n), 0) == bucket_ref[...]).astype(F32)


def _band_bias(rel_table_t, buckets):
    n = buckets.shape[-1]

    def body(rt_ref, bk_ref, o_ref):
        o_ref[...] = _hdot(rt_ref[...], _onehot_buckets(bk_ref), ((1,), (0,)))
    return pl.pallas_call(body, name="band_bias", out_shape=jax.ShapeDtypeStruct((ATTN_HEADS, n), F32))(rel_table_t, buckets)


def _band_bias_bwd(dbias, buckets):
    def body(db_ref, bk_ref, o_ref):
        d = db_ref[0]
        for layer in range(1, db_ref.shape[0]):
            d = d + db_ref[layer]
        o_ref[...] = _hdot(d, _onehot_buckets(bk_ref), ((1,), (1,)))
    return pl.pallas_call(body, name="band_bias_bwd", out_shape=jax.ShapeDtypeStruct((ATTN_HEADS, REL_BUCKETS), F32))(dbias, buckets)


def _attn_block(q, kvp, kvc, bias, sinks, qg, kg, first):
    qn = q.shape[0]
    rows = ATTN_REP * qn
    ri = lax.broadcasted_iota(jnp.int32, (rows, 2 * qn), 0) & (qn - 1)
    cj = lax.broadcasted_iota(jnp.int32, (rows, 2 * qn), 1)
    jj = cj & (qn - 1)
    no_prev = jnp.where(first, qn, 0)
    mask = ((cj < qn) & (jj > ri + no_prev)) | ((cj >= qn) & (jj <= ri))
    kvd = ATTN_KV_HEADS * ATTN_HEAD_DIM
    outs = []
    for g in range(ATTN_KV_HEADS):
        sl = slice(g * ATTN_HEAD_DIM, (g + 1) * ATTN_HEAD_DIM)
        vsl = slice(kvd + g * ATTN_HEAD_DIM, kvd + (g + 1) * ATTN_HEAD_DIM)
        qs = jnp.concatenate([_rms(q[:, (g * ATTN_REP + j) * ATTN_HEAD_DIM:(g * ATTN_REP + j + 1) * ATTN_HEAD_DIM], qg)
                              for j in range(ATTN_REP)], axis=0)
        kb = jnp.concatenate([_rms(kvp[:, sl], kg), _rms(kvc[:, sl], kg)], axis=0)
        vb = jnp.concatenate([kvp[:, vsl], kvc[:, vsl]], axis=0)
        logits = _bdot(qs, kb, ((1,), (1,))) * (ATTN_HEAD_DIM ** -0.5) + bias[g]
        logits = jnp.where(mask, logits, NEG_INF)
        sink = jnp.concatenate([jnp.broadcast_to(sinks[:, g * ATTN_REP + j:g * ATTN_REP + j + 1], (qn, 1)) for j in range(ATTN_REP)], axis=0)
        m = jnp.maximum(jnp.max(logits, axis=-1, keepdims=True), sink)
        pexp = jnp.exp(logits - m)
        probs = pexp * (1.0 / (jnp.sum(pexp, axis=-1, keepdims=True) + jnp.exp(sink - m)))
        o = _bdot(probs, vb, ((1,), (0,)))
        outs += [o[j * qn:(j + 1) * qn, :] for j in range(ATTN_REP)]
    return jnp.concatenate(outs, axis=-1)


def _attn_specs(nmap):
    qd, kvw = ATTN_HEADS * ATTN_HEAD_DIM, 2 * ATTN_KV_HEADS * ATTN_HEAD_DIM
    full = lambda shp: pl.BlockSpec(shp, lambda i: (0,) * len(shp))
    return [
        pl.BlockSpec((ATTN_BLOCK, qd), lambda i: (nmap(i), 0)),
        pl.BlockSpec((ATTN_BLOCK, kvw), lambda i: (jnp.maximum(nmap(i) - 1, 0), qd // kvw)),
        pl.BlockSpec((ATTN_BLOCK, kvw), lambda i: (nmap(i), qd // kvw)),
        full((ATTN_KV_HEADS, ATTN_REP * ATTN_BLOCK, 2 * ATTN_BLOCK)), full((1, ATTN_HEADS)), full((1, ATTN_HEAD_DIM)), full((1, ATTN_HEAD_DIM)),
    ]


def _attn_fwd(pqkv, bias, sinks, qg, kg, *, name, rider=None):
    t = pqkv.shape[0]
    nb = t // ATTN_BLOCK
    qd = ATTN_HEADS * ATTN_HEAD_DIM
    ride = _Ride(rider, 7, 1, 0)

    def body(*refs):
        (q_ref, kvp_ref, kvc_ref, bias_ref, sk_ref, qg_ref, kg_ref), (o_ref,), _ = ride.split(refs)
        i = pl.program_id(0)
        ride.start(refs, i == 0)
        o_ref[...] = _attn_block(q_ref[...], kvp_ref[...], kvc_ref[...], bias_ref[...], sk_ref[...], qg_ref[...], kg_ref[...], i == 0).astype(o_ref.dtype)
        ride.finish(refs, i == nb - 1)

    res = pl.pallas_call(
        body, name=name, grid=(nb,), in_specs=_attn_specs(lambda i: i) + ride.in_specs,
        out_specs=[pl.BlockSpec((ATTN_BLOCK, qd), lambda i: (i, 0))] + ride.out_specs,
        out_shape=[jax.ShapeDtypeStruct((t, qd), BF16)] + ride.out_shape, scratch_shapes=ride.scratch,
        compiler_params=pltpu.CompilerParams(dimension_semantics=("arbitrary",)),
    )(pqkv, pqkv, pqkv, bias, sinks, qg, kg, *ride.args)
    return res[0], res[1:]


def _attn_bwd(pqkv, do, bias, sinks, qg, kg, *, name, rider=None):
    t = pqkv.shape[0]
    nb = t // ATTN_BLOCK
    qd, kvw = ATTN_HEADS * ATTN_HEAD_DIM, 2 * ATTN_KV_HEADS * ATTN_HEAD_DIM
    rev = lambda i: nb - 1 - i
    ride = _Ride(rider, 8, 5, 1)

    def body(*refs):
        (q_ref, kvp_ref, kvc_ref, bias_ref, sk_ref, qg_ref, kg_ref, do_ref), (dqkv_ref, dbias_ref, dsk_ref, dqg_ref, dkg_ref), (carry_ref,) = ride.split(refs)
        i = pl.program_id(0)
        ride.start(refs, i == 0)
        first = rev(i) == 0
        f = functools.partial(_attn_block, first=first)
        _, vjp = jax.vjp(f, q_ref[...], kvp_ref[...], kvc_ref[...], bias_ref[...], sk_ref[...], qg_ref[...], kg_ref[...])
        dq, dkvp, dkvc, dbias, dsk, dqg, dkg = vjp(do_ref[...].astype(F32))

        @pl.when(i == 0)
        def _():
            carry_ref[...] = jnp.zeros_like(carry_ref)
            dbias_ref[...] = jnp.zeros_like(dbias_ref)
            dsk_ref[...] = jnp.zeros_like(dsk_ref)
            dqg_ref[...] = jnp.zeros_like(dqg_ref)
            dkg_ref[...] = jnp.zeros_like(dkg_ref)
        dqkv_ref[:, 0:qd] = dq.astype(dqkv_ref.dtype)
        dqkv_ref[:, qd:] = (dkvc + carry_ref[...]).astype(dqkv_ref.dtype)
        carry_ref[...] = dkvp
        dbias_ref[...] += dbias
        dsk_ref[...] += dsk
        dqg_ref[...] += dqg
        dkg_ref[...] += dkg
        ride.finish(refs, i == nb - 1)

    full = lambda shp: pl.BlockSpec(shp, lambda i: (0,) * len(shp))
    bshape = (ATTN_KV_HEADS, ATTN_REP * ATTN_BLOCK, 2 * ATTN_BLOCK)
    res = pl.pallas_call(
        body, name=name, grid=(nb,), in_specs=_attn_specs(rev) + [pl.BlockSpec((ATTN_BLOCK, qd), lambda i: (rev(i), 0))] + ride.in_specs,
        out_specs=[pl.BlockSpec((ATTN_BLOCK, qd + kvw), lambda i: (rev(i), 0)), full(bshape), full((1, ATTN_HEADS)),
                   full((1, ATTN_HEAD_DIM)), full((1, ATTN_HEAD_DIM))] + ride.out_specs,
        out_shape=[jax.ShapeDtypeStruct((t, qd + kvw), BF16), jax.ShapeDtypeStruct(bshape, F32), jax.ShapeDtypeStruct((1, ATTN_HEADS), F32),
                   jax.ShapeDtypeStruct((1, ATTN_HEAD_DIM), F32), jax.ShapeDtypeStruct((1, ATTN_HEAD_DIM), F32)] + ride.out_shape,
        scratch_shapes=[pltpu.VMEM((ATTN_BLOCK, kvw), F32)] + ride.scratch,
        compiler_params=pltpu.CompilerParams(dimension_semantics=("arbitrary",)),
    )(pqkv, pqkv, pqkv, bias, sinks, qg, kg, do, *ride.args)
    return res[:5], res[5:]


def _dev_index(dev):
    return 4 * dev[0] + 2 * dev[1] + dev[2]


def _gather_ops(x_refs, o_refs, send_sems, recv_sems, local_sems, axes):
    n = len(x_refs)
    x, y, c = lax.axis_index("x"), lax.axis_index("y"), lax.axis_index("c")
    me, sibling = (x, y, c), (x, y, 1 - c)
    chips = [(1 - x, y), (x, 1 - y), (1 - x, 1 - y)]

    def slot(i, dev):
        idx = _dev_index(dev)
        return o_refs[i].at[idx] if axes[i] == 0 else o_refs[i].at[:, idx]

    def copy(i, k, block, to, src=None):
        return pltpu.make_async_remote_copy(
            src_ref=slot(i, block) if src is None else src, dst_ref=slot(i, block),
            send_sem=send_sems.at[i, k], recv_sem=recv_sems.at[i, k], device_id=to, device_id_type=MESH)

    mine = [pltpu.make_async_copy(x_refs[i], slot(i, me), local_sems.at[i]) for i in range(n)]
    first = []
    for i in range(n):
        first.append(copy(i, 0, me, sibling, src=x_refs[i]))
        first += [copy(i, 1 + j, me, (*chip, c), src=x_refs[i]) for j, chip in enumerate(chips)]

    def start():
        for cp in mine + first:
            cp.start()

    def finish():
        passed = []
        for j, chip in enumerate(chips):
            for i in range(n):
                copy(i, 1 + j, (*chip, c), me).wait_recv()
                cp = copy(i, 4 + j, (*chip, c), sibling)
                cp.start()
                passed.append(cp)
        for i in range(n):
            copy(i, 0, sibling, me).wait_recv()
        for j, chip in enumerate(chips):
            for i in range(n):
                copy(i, 4 + j, (*chip, 1 - c), me).wait_recv()
        for cp in first + passed:
            cp.wait_send()
        for cp in mine:
            cp.wait()

    return start, finish


def _exchange_ops(g_refs, o_refs, send_sems, recv_sems, local_sems):
    n = len(g_refs)
    x, y, c = lax.axis_index("x"), lax.axis_index("y"), lax.axis_index("c")
    me = _dev_index((x, y, c))
    peers = [(x ^ ((k >> 2) & 1), y ^ ((k >> 1) & 1), c ^ (k & 1)) for k in range(1, N_DEV)]

    def copy(i, k):
        peer = peers[k]
        return pltpu.make_async_remote_copy(
            src_ref=g_refs[i].at[_dev_index(peer)], dst_ref=o_refs[i].at[me],
            send_sem=send_sems.at[i, k], recv_sem=recv_sems.at[i, k], device_id=peer, device_id_type=MESH)

    def arrival(i, k):
        peer = peers[k]
        return pltpu.make_async_remote_copy(
            src_ref=g_refs[i].at[me], dst_ref=o_refs[i].at[_dev_index(peer)],
            send_sem=send_sems.at[i, k], recv_sem=recv_sems.at[i, k], device_id=peer, device_id_type=MESH)

    mine = [pltpu.make_async_copy(g_refs[i].at[me], o_refs[i].at[me], local_sems.at[i]) for i in range(n)]
    sends = [copy(i, k) for i in range(n) for k in range(N_DEV - 1)]

    def start():
        for cp in mine + sends:
            cp.start()

    def finish():
        for i in range(n):
            for k in range(N_DEV - 1):
                arrival(i, k).wait_recv()
        for cp in sends:
            cp.wait_send()
        for cp in mine:
            cp.wait()

    return start, finish


class _Ride:
    def __init__(self, rider, n_in, n_out, n_scr):
        self.rider, self.n_in, self.n_out, self.n_scr = rider, n_in, n_out, n_scr
        self.args = [] if rider is None else list(rider[1])
        n = self.n = len(self.args)
        hbm = pl.BlockSpec(memory_space=pltpu.HBM)
        self.in_specs, self.out_specs = [hbm] * n, [hbm] * n
        if rider is None:
            self.out_shape, self.scratch = [], []
            return
        if rider[0] == 'gather':
            self.out_shape = [jax.ShapeDtypeStruct(a.shape[:ax] + (N_DEV,) + a.shape[ax:], a.dtype) for a, ax in zip(self.args, rider[2])]
        else:
            self.out_shape = [jax.ShapeDtypeStruct(a.shape, a.dtype) for a in self.args]
        self.scratch = [pltpu.SemaphoreType.DMA((n, 7)), pltpu.SemaphoreType.DMA((n, 7)), pltpu.SemaphoreType.DMA((n,))]

    def split(self, refs):
        a = self.n_in
        c = a + self.n + self.n_out
        e = c + self.n
        return refs[:a], refs[a + self.n:c], refs[e:e + self.n_scr]

    def _ops(self, refs):
        a = self.n_in
        c = a + self.n + self.n_out
        e = c + self.n + self.n_scr
        x_refs, o_refs, sems = refs[a:a + self.n], refs[c:c + self.n], refs[e:e + 3]
        if self.rider[0] == 'gather':
            return _gather_ops(x_refs, o_refs, *sems, self.rider[2])
        return _exchange_ops(x_refs, o_refs, *sems)

    def start(self, refs, cond):
        if self.rider is not None:
            pl.when(cond)(lambda: self._ops(refs)[0]())

    def finish(self, refs, cond):
        if self.rider is not None:
            pl.when(cond)(lambda: self._ops(refs)[1]())


def _comm_call(rider, *, name):
    ride = _Ride(rider, 0, 0, 0)

    def body(*refs):
        start, finish = ride._ops(refs)
        start()
        finish()

    return pl.pallas_call(body, name=name, in_specs=ride.in_specs, out_specs=ride.out_specs, out_shape=ride.out_shape,
                          scratch_shapes=ride.scratch)(*ride.args)


def _all_gather(xs, axes, *, name):
    return _comm_call(('gather', xs, axes), name=name)


def _exchange(gs, *, name):
    return _comm_call(('exchange', gs), name=name)


def _pack(arrays):
    parts = []
    for a in arrays:
        flat = a.reshape(-1)
        pad = (-flat.shape[0]) % LANES
        if pad:
            flat = jnp.concatenate([flat, jnp.zeros((pad,), flat.dtype)])
        parts.append(flat.reshape(-1, LANES))
    return jnp.concatenate(parts, axis=0)


def _unpack(buf, shapes):
    out, row = [], 0
    for shp in shapes:
        size = int(np.prod(shp))
        rows = -(-size // LANES)
        out.append(buf[row:row + rows].reshape(-1)[:size].reshape(shp))
        row += rows
    return out


def _row_tile(rows, width, n_bufs):
    padded = -(-width // LANES) * LANES
    cap = max(16, (12 << 20) // (padded * 4 * n_bufs))
    if rows <= cap:
        return rows
    tr = (cap // 16) * 16
    while tr > 16 and rows % tr:
        tr -= 16
    return tr if rows % tr == 0 else rows


def _step(p, m, v, x, mem, loss_target):
    t = x.shape[1]
    h0 = x.reshape(t, D_MODEL)
    mem2 = mem.reshape(MEM_LEN, D_MODEL)
    tgt = loss_target.reshape(t, D_MODEL)
    tr = 256 if t % 256 == 0 else t
    my = _dev_index((lax.axis_index("x"), lax.axis_index("y"), lax.axis_index("c")))

    small_sh_shapes = [p[n].shape for n in SMALL_SHARDED]
    gathered_small = _all_gather([_pack([p[n] for n in SMALL_SHARDED])], [0], name="gather_small")[0]
    full = {}
    for n, a in zip(SMALL_SHARDED, zip(*[_unpack(gathered_small[d], small_sh_shapes) for d in range(N_DEV)])):
        full[n] = jnp.concatenate(a, axis=-1)
    big_names = ROW_SHARDED + COL_SHARDED
    seg_order = ['a', 'z', 'x', 'dt', 'qkv', 'g']
    wg = {n: [] for n in big_names}
    w_seg = {k: [] for k in seg_order}
    with_ssd = ['w_in']
    with_swa = [n for n in big_names if n != 'w_in']

    def shards(layer, names):
        return [p[n][layer].astype(BF16) for n in names]

    def use_weights(got):
        for n in ROW_SHARDED:
            a = got[n]
            wg[n].append(a.reshape(a.shape[0] * a.shape[1], a.shape[2]))
        for n in ('w_xkv', 'w_mlp_up'):
            wg[n].append(got[n])
        w_in_full = jnp.transpose(got['w_in'], (1, 0, 2)).reshape(D_MODEL, IN_COLS)
        segs = {'a': w_in_full[:, 0:OFF_Z], 'z': w_in_full[:, OFF_Z:OFF_XBC], 'x': w_in_full[:, OFF_XBC:OFF_DT],
                'dt': jnp.pad(w_in_full[:, OFF_DT:OFF_Q], ((0, 0), (0, LANES - SSD_HEADS))),
                'qkv': w_in_full[:, OFF_Q:OFF_GATE], 'g': w_in_full[:, OFF_GATE:IN_COLS]}
        for k in seg_order:
            w_seg[k].append(segs[k])

    use_weights(dict(zip(big_names, _all_gather(shards(0, big_names), [0] * len(big_names), name="gather_weights"))))

    def vec(name, layer, width=None):
        a = p[name][layer].reshape(1, -1)
        if width is not None and a.shape[1] < width:
            a = jnp.pad(a, ((0, 0), (0, width - a.shape[1])))
        return a

    buckets = jnp.asarray(_rel_buckets().reshape(1, -1))
    bias = _band_bias(jnp.transpose(p['rel_table']), buckets).reshape(ATTN_KV_HEADS, ATTN_REP * ATTN_BLOCK, 2 * ATTN_BLOCK)

    saved = []
    h = h0
    for l in range(DEPTH):
        s = {'h0': h}
        u = _rowwise(lambda a, g: (_f_rms(a, g), ()), [(h, 0, D_MODEL)], [vec('norm_mix', l)], [(D_MODEL, BF16)], tr=tr, name="rms_mix")[0]
        s['u'] = u
        pr = {k: _mm(u, (w_seg[k], 'plain', l), mode='nn', name="proj_" + k) for k in seg_order}
        s['pr'] = pr
        dw_w, dw_b = full['conv_dw_w'][l], vec('conv_dw_b', l)
        ca = _conv_fwd([(pr['a'], 0), (pr['a'], D_MODEL // LANES)], dw_w, dw_b, kk=CONV_KERNEL, pre_glu=True, post_silu=False, name="conv31")
        s['ca'] = ca
        ya_in = _rowwise(lambda a, g, b: (_f_lnsilu(a, g, b), ()), [(ca, 0, D_MODEL)], [vec('conv_ln_g', l), vec('conv_ln_b', l)],
                         [(D_MODEL, BF16)], tr=tr, name="ln_silu")[0]
        s['ya_in'] = ya_in
        y_a = _mm(ya_in, (wg['w_conv_out'], 'plain', l), mode='nn', name="conv_out")
        xbc = _conv_fwd([(pr['x'], 0)], full['ssd_conv_w'][l], vec('ssd_conv_b', l), kk=SSD_CONV, pre_glu=False, post_silu=True, name="conv4")
        s['xbc'] = xbc
        ssd_vecs = [vec('ssd_dt_bias', l, LANES), vec('ssd_A_log', l, LANES), vec('ssd_D', l, LANES)]
        nxt = {}
        more = l + 1 < DEPTH
        (y_ssd, states), got = _ssd_fwd(xbc, pr['dt'], *ssd_vecs, name="ssd",
                                        rider=('gather', shards(l + 1, with_ssd), [0] * len(with_ssd)) if more else None)
        nxt.update(zip(with_ssd, got))
        s['y_ssd'], s['states'] = y_ssd, states
        yb_in = _rowwise(lambda a, z, g: (_f_ssdgate(a, z, g), ()), [(y_ssd, 0, SSD_INNER), (pr['z'], 0, SSD_INNER)], [vec('ssd_norm_g', l)],
                         [(SSD_INNER, BF16)], tr=tr, name="ssd_gate")[0]
        s['yb_in'] = yb_in
        y_b = _mm(yb_in, (wg['w_ssd_out'], 'plain', l), mode='nn', name="ssd_out")
        att, got = _attn_fwd(pr['qkv'], bias, vec('attn_sinks', l), vec('attn_q_norm', l), vec('attn_k_norm', l), name="swa",
                             rider=('gather', shards(l + 1, with_swa), [0] * len(with_swa)) if more else None)
        nxt.update(zip(with_swa, got))
        if more:
            use_weights(nxt)
        s['att'] = att
        y_c = _mm(att, (wg['w_attn_out'], 'plain', l), mode='nn', name="attn_out")
        s['y_a'], s['y_b'], s['y_c'] = y_a, y_b, y_c
        merged = _rowwise(lambda pg, a, b, c, gb: (_f_merge(pg, a, b, c, gb), ()),
                          [(pr['g'], 0, 3 * D_MODEL), (y_a, 0, D_MODEL), (y_b, 0, D_MODEL), (y_c, 0, D_MODEL)], [full['gate_bias'][l]],
                          [(D_MODEL, BF16)], tr=tr, name="merge")[0]
        s['merged'] = merged
        h = _mm(merged, (wg['w_mix_out'], 'plain', l), mode='nn', add=h, name="mix_out")
        s['h1'] = h
        un = _rowwise(lambda a, g: (_f_rms(a, g), ()), [(h, 0, D_MODEL)], [vec('norm_xattn', l)], [(D_MODEL, BF16)], tr=tr, name="rms_xattn")[0]
        memn = _rowwise(lambda a, g: (_f_rms(a, g), ()), [(mem2, 0, D_MODEL)], [vec('norm_mem', l)], [(D_MODEL, BF16)], tr=MEM_LEN, name="rms_mem")[0]
        s['un'], s['memn'] = un, memn
        xq = _mm(un, (wg['w_xq'], 'plain', l), mode='nn', name="xq")
        kv = _mm(memn, (wg['w_xkv'], 'col', l), mode='nn', name="xkv", tn=256)
        s['xq'], s['kv'] = xq, kv
        xo = _rowwise(lambda q, kvv, qg, kg: (_f_xattn(q, kvv, qg, kg), ()), [(xq, 0, D_MODEL)], [kv, vec('xattn_q_norm', l), vec('xattn_k_norm', l)],
                      [(D_MODEL, BF16)], tr=tr, name="xattn")[0]
        s['xo'] = xo
        h = _mm(xo, (wg['w_xo'], 'plain', l), mode='nn', add=h, name="xattn_out")
        s['h2'] = h
        um = _rowwise(lambda a, g: (_f_rms(a, g), ()), [(h, 0, D_MODEL)], [vec('norm_mlp', l)], [(D_MODEL, BF16)], tr=tr, name="rms_mlp")[0]
        s['um'] = um
        up = _mm(um, (wg['w_mlp_up'], 'col', l), mode='nn', name="mlp_up", tn=512)
        s['up'] = up
        act = _rowwise(lambda a: (_f_relu2(a), ()), [(up, 0, MLP_HIDDEN)], [], [(MLP_HIDDEN, BF16)], tr=tr, name="relu2")[0]
        s['act'] = act
        h = _mm(act, (wg['w_mlp_down'], 'plain', l), mode='nn', add=h, name="mlp_down")
        saved.append(s)

    dh, loss_part = _rowwise(_f_loss, [(h, 0, D_MODEL), (tgt, 0, D_MODEL)], [], [(D_MODEL, F32)], [(1, LANES)], tr=tr, name="loss")

    sg = {n: [None] * DEPTH for n in SMALL if n != 'rel_table'}
    dbias_layers = [None] * DEPTH
    recv = {}
    with_d_swa = ['w_mlp_down', 'w_mlp_up', 'w_xo', 'w_xq', 'w_xkv', 'w_mix_out', 'w_attn_out']
    left_over = []

    def by_device(g_):
        return g_.reshape(N_DEV, g_.shape[0] // N_DEV, g_.shape[1])

    for l in reversed(range(DEPTH)):
        s = saved[l]
        bg = {}
        dact = _mm(dh, (wg['w_mlp_down'], 'plain', l), mode='nt', name="d_act")
        bg['w_mlp_down'] = _mm(s['act'], dh, mode='tn', out_dtype=BF16, name="dw_mlp_down")
        dup = _rowwise(lambda a, d: (_vjp_rows(_f_relu2, 1, 1)(a, d)[0], ()), [(s['up'], 0, MLP_HIDDEN), (dact, 0, MLP_HIDDEN)], [],
                       [(MLP_HIDDEN, BF16)], tr=tr, name="d_relu2")[0]
        bg['w_mlp_up'] = _mm(s['um'], dup, mode='tn', out_dtype=BF16, out_col=True, name="dw_mlp_up", tn=512)
        dum = _mm(dup, (wg['w_mlp_up'], 'col', l), mode='nt', name="d_um", tk=512)
        dh, dg = _rms_bwd_call(s['h2'], vec('norm_mlp', l), dum, dh, tr, "d_rms_mlp")
        sg['norm_mlp'][l] = dg
        dxo = _mm(dh, (wg['w_xo'], 'plain', l), mode='nt', out_dtype=BF16, name="d_xo")
        bg['w_xo'] = _mm(s['xo'], dh, mode='tn', out_dtype=BF16, name="dw_xo")
        qg, kg = vec('xattn_q_norm', l), vec('xattn_k_norm', l)
        dxq, dkv, dqg, dkg = _xattn_bwd_call(s['xq'], s['kv'], qg, kg, dxo, tr)
        sg['xattn_q_norm'][l], sg['xattn_k_norm'][l] = dqg, dkg
        bg['w_xq'] = _mm(s['un'], dxq, mode='tn', out_dtype=BF16, name="dw_xq")
        dun = _mm(dxq, (wg['w_xq'], 'plain', l), mode='nt', name="d_un")
        dh, dg = _rms_bwd_call(s['h1'], vec('norm_xattn', l), dun, dh, tr, "d_rms_xattn")
        sg['norm_xattn'][l] = dg
        bg['w_xkv'] = _mm(s['memn'], dkv, mode='tn', out_dtype=BF16, out_col=True, name="dw_xkv", tn=256)
        dmemn = _mm(dkv, (wg['w_xkv'], 'col', l), mode='nt', name="d_memn", tk=256)
        _, dg = _rms_bwd_call(mem2, vec('norm_mem', l), dmemn, jnp.zeros_like(mem2), MEM_LEN, "d_rms_mem")
        sg['norm_mem'][l] = dg
        dmerged = _mm(dh, (wg['w_mix_out'], 'plain', l), mode='nt', out_dtype=BF16, name="d_merged")
        bg['w_mix_out'] = _mm(s['merged'], dh, mode='tn', out_dtype=BF16, name="dw_mix_out")
        pr = s['pr']
        gb = full['gate_bias'][l]
        dpg, dya, dyb, dyc, dgb = _merge_bwd_call(pr['g'], s['y_a'], s['y_b'], s['y_c'], gb, dmerged, tr)
        sg['gate_bias'][l] = dgb
        dseg = {'g': dpg}
        datt = _mm(dyc, (wg['w_attn_out'], 'plain', l), mode='nt', out_dtype=BF16, name="d_att")
        bg['w_attn_out'] = _mm(s['att'], dyc, mode='tn', out_dtype=BF16, name="dw_attn_out")
        early = [bg[n] if n in COL_SHARDED else by_device(bg[n]) for n in with_d_swa]
        (dqkv, dbias_l, dsk, dqn, dkn), got = _attn_bwd(pr['qkv'], datt, bias, vec('attn_sinks', l), vec('attn_q_norm', l), vec('attn_k_norm', l),
                                                     name="d_swa", rider=('exchange', early))
        recv.update({(n, l): r for n, r in zip(with_d_swa, got)})
        dseg['qkv'] = dqkv
        dbias_layers[l] = dbias_l
        sg['attn_sinks'][l], sg['attn_q_norm'][l], sg['attn_k_norm'][l] = dsk, dqn, dkn
        dyb_in = _mm(dyb, (wg['w_ssd_out'], 'plain', l), mode='nt', out_dtype=BF16, name="d_yb_in")
        bg['w_ssd_out'] = _mm(s['yb_in'], dyb, mode='tn', out_dtype=BF16, name="dw_ssd_out")
        ng = vec('ssd_norm_g', l)
        dy_ssd, dz, dng = _ssdgate_bwd_call(s['y_ssd'], pr['z'], ng, dyb_in, tr)
        sg['ssd_norm_g'][l] = dng
        dseg['z'] = dz
        ssd_vecs = [vec('ssd_dt_bias', l, LANES), vec('ssd_A_log', l, LANES), vec('ssd_D', l, LANES)]
        late = [('w_ssd_out', l, by_device(bg['w_ssd_out']))] + left_over
        (dxbc_act, ddt, dtb, dal, ddsk), got = _ssd_bwd(s['xbc'], pr['dt'], s['states'], dy_ssd, *ssd_vecs, name="d_ssd",
                                                        rider=('exchange', [a for _, _, a in late]))
        recv.update({(n, ll): r for (n, ll, _), r in zip(late, got)})
        dseg['dt'] = ddt
        sg['ssd_dt_bias'][l], sg['ssd_A_log'][l], sg['ssd_D'][l] = dtb[:, :SSD_HEADS], dal[:, :SSD_HEADS], ddsk[:, :SSD_HEADS]
        dxbc, dcw, dcb = _conv_bwd([(pr['x'], 0)], full['ssd_conv_w'][l], vec('ssd_conv_b', l), [(dxbc_act, 0, SSD_XBC // LANES)],
                                   kk=SSD_CONV, pre_glu=False, post_silu=True, name="d_conv4", dx_dtype=BF16)
        dseg['x'] = dxbc
        sg['ssd_conv_w'][l], sg['ssd_conv_b'][l] = dcw, dcb
        dya_in = _mm(dya, (wg['w_conv_out'], 'plain', l), mode='nt', out_dtype=BF16, name="d_ya_in")
        bg['w_conv_out'] = _mm(s['ya_in'], dya, mode='tn', out_dtype=BF16, name="dw_conv_out")
        lg, lb = vec('conv_ln_g', l), vec('conv_ln_b', l)
        dca, dlg, dlb = _lnsilu_bwd_call(s['ca'], lg, lb, dya_in, tr)
        sg['conv_ln_g'][l], sg['conv_ln_b'][l] = dlg, dlb
        da, dgate, dww, dwb = _conv_bwd([(pr['a'], 0), (pr['a'], D_MODEL // LANES)], full['conv_dw_w'][l], vec('conv_dw_b', l),
                                        [(dca, 0, D_MODEL // LANES)], kk=CONV_KERNEL, pre_glu=True, post_silu=False, name="d_conv31", dx_dtype=BF16)
        dseg['a'] = jnp.concatenate([da, dgate], axis=-1)
        sg['conv_dw_w'][l], sg['conv_dw_b'][l] = dww, dwb
        du = None
        dw_parts = []
        for k in seg_order:
            du = _mm(dseg[k], (w_seg[k], 'plain', l), mode='nt', add=du, name="d_u_" + k)
            dw_parts.append(_mm(s['u'], dseg[k], mode='tn', out_dtype=BF16, name="dw_in_" + k))
        dw_parts[3] = dw_parts[3][:, :SSD_HEADS]
        dw_in = jnp.concatenate(dw_parts, axis=-1)
        dw_in = jnp.transpose(dw_in.reshape(D_MODEL, N_DEV, IN_COLS // N_DEV), (1, 0, 2))
        dh, dg = _rms_bwd_call(s['h0'], vec('norm_mix', l), du, dh, tr, "d_rms_mix")
        sg['norm_mix'][l] = dg
        left_over = [('w_conv_out', l, by_device(bg['w_conv_out'])), ('w_in', l, dw_in)]

    got = _exchange([a for _, _, a in left_over], name="exchange_grads")
    recv.update({(n, ll): r for (n, ll, _), r in zip(left_over, got)})
    grad_x = dh.reshape(x.shape)
    d_rel = jnp.transpose(_band_bias_bwd(jnp.stack(dbias_layers).reshape(DEPTH, ATTN_HEADS, -1), buckets))

    small_full = {'rel_table': d_rel}
    for n in SMALL:
        if n != 'rel_table':
            small_full[n] = jnp.stack(sg[n]).reshape((DEPTH,) + (full[n].shape[1:] if n in SMALL_SHARDED else p[n].shape[1:]))
    small_shapes = [(1, LANES)] + [small_full[n].shape for n in SMALL]
    packed = _pack([loss_part] + [small_full[n] for n in SMALL])
    slots = _all_gather([packed], [0], name="gather_small_grads")[0]
    rows = packed.shape[0]
    reduced = _rowwise(_f_sum_slots, [(slots, 0, LANES)], [], [(LANES, F32)], tr=_row_tile(rows, LANES, 12), name="sum_small")[0]
    red = _unpack(reduced, small_shapes)
    loss = red[0][0, 0]
    small_grad = {}
    for n, g_ in zip(SMALL, red[1:]):
        if n in SMALL_SHARDED:
            wdt = p[n].shape[-1]
            g_ = lax.dynamic_slice_in_dim(g_, my * wdt, wdt, axis=g_.ndim - 1)
        small_grad[n] = g_
    local_shapes = [p[n].shape for n in SMALL]
    pk = lambda d: _pack([d[n] for n in SMALL])
    pg_, pw_, pm_, pv_ = pk(small_grad), pk(p), pk(m), pk(v)
    srows = pg_.shape[0]
    sd, sm, sv = _rowwise(_f_adam, [(pg_, 0, LANES), (pw_, 0, LANES), (pm_, 0, LANES), (pv_, 0, LANES)], [],
                          [(LANES, F32)] * 3, tr=_row_tile(srows, LANES, 16), name="adam_small")
    out_delta = dict(zip(SMALL, _unpack(sd, local_shapes)))
    out_m = dict(zip(SMALL, _unpack(sm, local_shapes)))
    out_v = dict(zip(SMALL, _unpack(sv, local_shapes)))
    out_grad = dict(small_grad)

    per_layer = {n: [] for n in big_names}
    for l in range(DEPTH):
        for n in big_names:
            r = recv[(n, l)]
            rws, wdt = r.shape[1], r.shape[2]
            tr_w = _row_tile(rws, wdt, 24)
            outs = _rowwise(_f_adam_slots, [(r, 0, wdt), (p[n], 0, wdt, l), (m[n], 0, wdt, l), (v[n], 0, wdt, l)], [],
                            [(wdt, F32)] * 4, tr=tr_w, name="adam_" + n)
            per_layer[n].append(outs)
    for n in big_names:
        for k, dst in enumerate((out_grad, out_delta, out_m, out_v)):
            dst[n] = jnp.stack([per_layer[n][l][k] for l in range(DEPTH)])
    return (loss, grad_x, *[out_grad[n] for n in WEIGHTS], *[out_delta[n] for n in WEIGHTS],
            *[out_m[n] for n in WEIGHTS], *[out_v[n] for n in WEIGHTS])


def _rms_bwd_call(h, g, du, dres, tr, name):
    return _rowwise(lambda a, d, r, gg: _f_rms_bwd(a, gg, d, r), [(h, 0, D_MODEL), (du, 0, D_MODEL), (dres, 0, D_MODEL)], [g],
                    [(D_MODEL, F32)], [g.shape], tr=tr, name=name)


def _xattn_bwd_call(xq, kv, qg, kg, dxo, tr):
    def f(q, d, kvv, qgv, kgv):
        return _vjp_rows(lambda a, b, c, e: _f_xattn(a, b, c, e), 4, 1)(q, kvv, qgv, kgv, d)
    return _rowwise(f, [(xq, 0, D_MODEL), (dxo, 0, D_MODEL)], [kv, qg, kg], [(D_MODEL, BF16)], [kv.shape, qg.shape, kg.shape], tr=tr, name="d_xattn")


def _merge_bwd_call(pg, ya, yb, yc, gb, dmerged, tr):
    def f(a, b, c, e, d, gbv):
        return _vjp_rows(_f_merge, 5, 4)(a, b, c, e, gbv, d)
    return _rowwise(f, [(pg, 0, 3 * D_MODEL), (ya, 0, D_MODEL), (yb, 0, D_MODEL), (yc, 0, D_MODEL), (dmerged, 0, D_MODEL)], [gb],
                    [(3 * D_MODEL, BF16)] + [(D_MODEL, BF16)] * 3, [gb.shape], tr=tr, name="d_merge")


def _ssdgate_bwd_call(y, z, ng, dy, tr):
    def f(a, b, d, g):
        return _vjp_rows(_f_ssdgate, 3, 2)(a, b, g, d)
    return _rowwise(f, [(y, 0, SSD_INNER), (z, 0, SSD_INNER), (dy, 0, SSD_INNER)], [ng], [(SSD_INNER, F32), (SSD_INNER, BF16)], [ng.shape], tr=tr, name="d_ssd_gate")


def _lnsilu_bwd_call(ca, lg, lb, dy, tr):
    def f(a, d, g, b):
        return _vjp_rows(_f_lnsilu, 3, 1)(a, g, b, d)
    return _rowwise(f, [(ca, 0, D_MODEL), (dy, 0, D_MODEL)], [lg, lb], [(D_MODEL, F32)], [lg.shape, lb.shape], tr=tr, name="d_ln_silu")


def kernel(x, mem, rel_table, norm_mix, w_in, gate_bias, conv_dw_w, conv_dw_b, conv_ln_g, conv_ln_b, w_conv_out, ssd_conv_w, ssd_conv_b, ssd_dt_bias, ssd_A_log, ssd_D, ssd_norm_g, w_ssd_out, attn_q_norm, attn_k_norm, attn_sinks, w_attn_out, w_mix_out, norm_xattn, norm_mem, w_xq, w_xkv, xattn_q_norm, xattn_k_norm, w_xo, norm_mlp, w_mlp_up, w_mlp_down, loss_target, m_rel_table, m_norm_mix, m_w_in, m_gate_bias, m_conv_dw_w, m_conv_dw_b, m_conv_ln_g, m_conv_ln_b, m_w_conv_out, m_ssd_conv_w, m_ssd_conv_b, m_ssd_dt_bias, m_ssd_A_log, m_ssd_D, m_ssd_norm_g, m_w_ssd_out, m_attn_q_norm, m_attn_k_norm, m_attn_sinks, m_w_attn_out, m_w_mix_out, m_norm_xattn, m_norm_mem, m_w_xq, m_w_xkv, m_xattn_q_norm, m_xattn_k_norm, m_w_xo, m_norm_mlp, m_w_mlp_up, m_w_mlp_down, v_rel_table, v_norm_mix, v_w_in, v_gate_bias, v_conv_dw_w, v_conv_dw_b, v_conv_ln_g, v_conv_ln_b, v_w_conv_out, v_ssd_conv_w, v_ssd_conv_b, v_ssd_dt_bias, v_ssd_A_log, v_ssd_D, v_ssd_norm_g, v_w_ssd_out, v_attn_q_norm, v_attn_k_norm, v_attn_sinks, v_w_attn_out, v_w_mix_out, v_norm_xattn, v_norm_mem, v_w_xq, v_w_xkv, v_xattn_q_norm, v_xattn_k_norm, v_w_xo, v_norm_mlp, v_w_mlp_up, v_w_mlp_down):
    args = locals()
    p = {n: args[n] for n in WEIGHTS}
    m = {n: args["m_" + n] for n in WEIGHTS}
    v = {n: args["v_" + n] for n in WEIGHTS}
    return _step(p, m, v, x, mem, loss_target)
```

```python
import functools
import math

import numpy as np
import jax
import jax.numpy as jnp
from jax import lax
from jax.experimental import pallas as pl
from jax.experimental.pallas import tpu as pltpu

F32 = jnp.float32
BF16 = jnp.bfloat16
HI = lax.Precision.HIGHEST
MESH = pl.DeviceIdType.MESH

N_DEV = 8
D_MODEL = 1024
DEPTH = 4
MEM_LEN = 256
EPS = 1e-6
NEG_INF = -1e30
CONV_KERNEL = 31
SSD_INNER = 2048
SSD_HEAD_DIM = 64
SSD_HEADS = 32
SSD_GROUPS = 4
SSD_STATE = 128
SSD_CONV = 4
SSD_CHUNK = 128
SSD_XBC = SSD_INNER + 2 * SSD_GROUPS * SSD_STATE
HEADS_PER_GROUP = SSD_HEADS // SSD_GROUPS
ATTN_HEADS = 16
ATTN_KV_HEADS = 4
ATTN_HEAD_DIM = 64
ATTN_BLOCK = 128
ATTN_REP = ATTN_HEADS // ATTN_KV_HEADS
REL_BUCKETS = 32
REL_MAX_DIST = 128
XATTN_HEADS = 4
XATTN_HEAD_DIM = 256
MLP_HIDDEN = 4096
OFF_Z = 2048
OFF_XBC = 4096
OFF_DT = 7168
OFF_Q = 7200
OFF_GATE = 8736
IN_COLS = 11808
LANES = 128
CONV_PAD = 32
MM_VMEM_BUDGET = 20 << 20

ADAM_LR, ADAM_B1, ADAM_B2, ADAM_EPS, ADAM_WD, ADAM_STEP = 0.001, 0.9, 0.999, 1e-08, 0.01, 10

WEIGHTS = ['rel_table', 'norm_mix', 'w_in', 'gate_bias', 'conv_dw_w', 'conv_dw_b', 'conv_ln_g', 'conv_ln_b', 'w_conv_out',
           'ssd_conv_w', 'ssd_conv_b', 'ssd_dt_bias', 'ssd_A_log', 'ssd_D', 'ssd_norm_g', 'w_ssd_out', 'attn_q_norm',
           'attn_k_norm', 'attn_sinks', 'w_attn_out', 'w_mix_out', 'norm_xattn', 'norm_mem', 'w_xq', 'w_xkv', 'xattn_q_norm',
           'xattn_k_norm', 'w_xo', 'norm_mlp', 'w_mlp_up', 'w_mlp_down']
ROW_SHARDED = ['w_conv_out', 'w_ssd_out', 'w_attn_out', 'w_mix_out', 'w_xq', 'w_xo', 'w_mlp_down']
COL_SHARDED = ['w_in', 'w_xkv', 'w_mlp_up']
BIG = ROW_SHARDED + COL_SHARDED
SMALL_SHARDED = ['gate_bias', 'conv_dw_w', 'ssd_conv_w']
SMALL = [n for n in WEIGHTS if n not in BIG]


def _bdot(a, b, dims):
    return lax.dot_general(a.astype(BF16), b.astype(BF16), (dims, ((), ())), preferred_element_type=F32)


def _hdot(a, b, dims):
    return lax.dot_general(a, b, (dims, ((), ())), precision=HI, preferred_element_type=F32)


def _pick(dim, pref):
    if dim <= pref:
        return dim
    t = (pref // LANES) * LANES
    while dim % t:
        t -= LANES
    return t


def _logical(op):
    arr, kind, _ = op
    r, c = arr.shape[-2:]
    return (r, c * N_DEV) if kind == 'col' else (r, c)


def _opspec(op, br, bc, rc):
    arr, kind, layer = op
    lead = () if layer is None else (layer,)
    none = (None,) * len(lead)
    if kind == 'plain':
        return pl.BlockSpec(none + (br, bc), lambda i, j, k: lead + rc(i, j, k))
    per = arr.shape[-1] // bc

    def imap(i, j, k):
        r, c = rc(i, j, k)
        if per == 1:
            return lead + (c, r, 0)
        return lead + (lax.div(c, per), r, lax.rem(c, per))
    return pl.BlockSpec(none + (None, br, bc), imap)


def _mm(a, b, *, mode, name, add=None, out_dtype=F32, out_col=False, epi=None, tm=2048, tn=512, tk=1024):
    def operand(op):
        op = op if isinstance(op, tuple) else (op, 'plain', None)
        return (op[0][op[2]], op[1], None) if isinstance(op[0], list) else op

    a, b = operand(a), operand(b)
    ar, ac = _logical(a)
    br_, bc_ = _logical(b)
    if mode == 'nn':
        m, kd, n = ar, ac, bc_
        assert br_ == kd
    elif mode == 'nt':
        m, kd, n = ar, ac, br_
        assert bc_ == kd
    else:
        m, kd, n = ac, ar, bc_
        assert br_ == kd

    def lim(op, is_col_dim):
        return op[0].shape[-1] if (op[1] == 'col' and is_col_dim) else 1 << 30

    tm = _pick(m, min(tm, lim(a, mode == 'tn')))
    tn = _pick(n, min(tn, lim(b, mode != 'nt'), (n // N_DEV) if out_col else 1 << 30))
    tk = _pick(kd, min(tk, lim(a, mode != 'tn'), lim(b, mode == 'nt')))
    nk = kd // tk

    def vmem_bytes(tm_):
        out_bytes = sum(jnp.dtype(dt).itemsize for dt in (epi[2] if epi is not None else [out_dtype]))
        extra_bytes = sum(e.dtype.itemsize for e in (epi[1] if epi is not None else [])) + (add.dtype.itemsize if add is not None else 0)
        blocks = tm_ * tk * a[0].dtype.itemsize + tk * tn * b[0].dtype.itemsize + tm_ * tn * (out_bytes + extra_bytes)
        return 2 * blocks + tm_ * tn * 4 * (2 if nk > 1 else 1)

    while vmem_bytes(tm) > MM_VMEM_BUDGET and tm % 256 == 0 and tm > 256:
        tm //= 2
    if mode == 'nn':
        a_spec = _opspec(a, tm, tk, lambda i, j, k: (i, k))
        b_spec = _opspec(b, tk, tn, lambda i, j, k: (k, j))
        dims = ((1,), (0,))
    elif mode == 'nt':
        a_spec = _opspec(a, tm, tk, lambda i, j, k: (i, k))
        b_spec = _opspec(b, tn, tk, lambda i, j, k: (j, k))
        dims = ((1,), (1,))
    else:
        a_spec = _opspec(a, tk, tm, lambda i, j, k: (k, i))
        b_spec = _opspec(b, tk, tn, lambda i, j, k: (k, j))
        dims = ((0,), (0,))
    if out_col:
        out_shape = jax.ShapeDtypeStruct((N_DEV, m, n // N_DEV), out_dtype)
        out_spec = _opspec((out_shape, 'col', None), tm, tn, lambda i, j, k: (i, j))
    else:
        out_shape = jax.ShapeDtypeStruct((m, n), out_dtype)
        out_spec = pl.BlockSpec((tm, tn), lambda i, j, k: (i, j))
    has_add = add is not None
    epi_fn, epi_extra, epi_dtypes = epi if epi is not None else (None, [], [out_dtype])
    n_in = 2 + has_add + len(epi_extra)
    n_out = len(epi_dtypes)

    def body(*refs):
        a_ref, b_ref = refs[0], refs[1]
        add_ref = refs[2] if has_add else None
        extra_refs = refs[2 + has_add:n_in]
        o_refs = refs[n_in:n_in + n_out]

        def emit(acc):
            outs = (acc,) if epi_fn is None else epi_fn(acc, *[r[...] for r in extra_refs])
            for o_ref, o in zip(o_refs, outs):
                o_ref[...] = o.astype(o_ref.dtype)

        part = _bdot(a_ref[...], b_ref[...], dims)
        if nk == 1:
            emit(part + add_ref[...].astype(F32) if has_add else part)
            return
        acc_ref = refs[n_in + n_out]
        k = pl.program_id(2)

        @pl.when(k == 0)
        def _():
            acc_ref[...] = part + add_ref[...].astype(F32) if has_add else part

        @pl.when(k > 0)
        def _():
            acc_ref[...] += part

        @pl.when(k == nk - 1)
        def _():
            emit(acc_ref[...])

    block = pl.BlockSpec((tm, tn), lambda i, j, k: (i, j))
    in_specs = [a_spec, b_spec] + [block] * (has_add + len(epi_extra))
    args = [a[0], b[0]] + ([add] if has_add else []) + list(epi_extra)
    if epi is not None:
        assert not out_col
        out_spec = [block] * n_out
        out_shape = [jax.ShapeDtypeStruct((m, n), dt) for dt in epi_dtypes]
    return pl.pallas_call(
        body, name=name, grid=(m // tm, n // tn, nk), in_specs=in_specs, out_specs=out_spec, out_shape=out_shape,
        scratch_shapes=[pltpu.VMEM((tm, tn), F32)] if nk > 1 else [],
        compiler_params=pltpu.CompilerParams(dimension_semantics=("parallel", "parallel", "arbitrary")),
    )(*args)


def _rowwise(f, rows, consts, row_outs, acc_outs=(), *, tr, name):
    nr, nc, nro = len(rows), len(consts), len(row_outs)
    first = rows[0][0]
    t = first.shape[-2]
    assert t % tr == 0
    in_specs = []
    for spec in rows:
        arr, cb, w = spec[:3]
        lead = spec[3] if len(spec) > 3 else None
        if arr.ndim == 2:
            in_specs.append(pl.BlockSpec((tr, w), functools.partial(lambda i, cb: (i, cb), cb=cb)))
        elif lead is not None:
            in_specs.append(pl.BlockSpec((None, tr, w), functools.partial(lambda i, cb, lead: (lead, i, cb), cb=cb, lead=lead)))
        else:
            in_specs.append(pl.BlockSpec((arr.shape[0], tr, w), functools.partial(lambda i, cb: (0, i, cb), cb=cb)))
    for cst in consts:
        in_specs.append(pl.BlockSpec(cst.shape, functools.partial(lambda i, nd: (0,) * nd, nd=cst.ndim)))
    out_specs = [pl.BlockSpec((tr, w), lambda i: (i, 0)) for w, _ in row_outs]
    out_shape = [jax.ShapeDtypeStruct((t, w), dt) for w, dt in row_outs]
    for shp in acc_outs:
        out_specs.append(pl.BlockSpec(shp, functools.partial(lambda i, nd: (0,) * nd, nd=len(shp))))
        out_shape.append(jax.ShapeDtypeStruct(shp, F32))

    def body(*refs):
        ins = [r[...] for r in refs[:nr + nc]]
        ro = refs[nr + nc:nr + nc + nro]
        ao = refs[nr + nc + nro:]
        outs, accs = f(*ins)
        for o_ref, o in zip(ro, outs):
            o_ref[...] = o.astype(o_ref.dtype)
        if ao:
            i = pl.program_id(0)

            @pl.when(i == 0)
            def _():
                for a_ref, acc in zip(ao, accs):
                    a_ref[...] = acc

            @pl.when(i > 0)
            def _():
                for a_ref, acc in zip(ao, accs):
                    a_ref[...] += acc

    res = pl.pallas_call(
        body, name=name, grid=(t // tr,), in_specs=in_specs, out_specs=out_specs, out_shape=out_shape,
        compiler_params=pltpu.CompilerParams(dimension_semantics=("arbitrary",)),
    )(*[s[0] for s in rows], *consts)
    return res


def _vjp_rows(f, n_prim, n_rows_grad):
    def g(*args):
        prim, cots = args[:n_prim], args[n_prim:]
        outs, vjp = jax.vjp(f, *prim)
        grads = vjp(tuple(c.astype(o.dtype) for c, o in zip(cots, outs)))
        return tuple(grads[:n_rows_grad]), tuple(grads[n_rows_grad:])
    return g


def _rms(x, g):
    return x * lax.rsqrt(jnp.mean(x * x, axis=-1, keepdims=True) + EPS) * g


def _f_rms(h, g):
    return (_rms(h, g),)


def _f_rms_bwd(h, g, du, dres):
    _, vjp = jax.vjp(_f_rms, h, g)
    dh, dg = vjp((du.astype(F32),))
    return (dh + dres, dh + dres), (dg,)


def _f_lnsilu(x, g, b):
    mu = jnp.mean(x, axis=-1, keepdims=True)
    xc = x - mu
    y = xc * lax.rsqrt(jnp.mean(xc * xc, axis=-1, keepdims=True) + EPS) * g + b
    return (jax.nn.silu(y),)


def _f_ssdgate(y, z, g):
    y = y * jax.nn.silu(z)
    w = SSD_INNER // SSD_GROUPS
    return (jnp.concatenate([_rms(y[:, i * w:(i + 1) * w], g[:, i * w:(i + 1) * w]) for i in range(SSD_GROUPS)], axis=-1),)


def _f_merge(pg, ya, yb, yc, gb):
    out = 0.0
    for i, yi in enumerate((ya, yb, yc)):
        out = out + jax.nn.sigmoid(pg[:, i * D_MODEL:(i + 1) * D_MODEL] + gb[i:i + 1, :]) * yi
    return (out,)


def _f_relu2(a):
    return (jnp.square(jnp.maximum(a, 0.0)),)


def _f_xattn(q, kv, qg, kg):
    outs = []
    for h in range(XATTN_HEADS):
        sl = slice(h * XATTN_HEAD_DIM, (h + 1) * XATTN_HEAD_DIM)
        qh = _rms(q[:, sl], qg)
        kh = _rms(kv[:, sl], kg)
        vh = kv[:, D_MODEL + h * XATTN_HEAD_DIM:D_MODEL + (h + 1) * XATTN_HEAD_DIM]
        s = _bdot(qh, kh, ((1,), (1,))) * (XATTN_HEAD_DIM ** -0.5)
        p = jnp.exp(s - jnp.max(s, axis=-1, keepdims=True))
        p = p * (1.0 / jnp.sum(p, axis=-1, keepdims=True))
        outs.append(_bdot(p, vh, ((1,), (0,))))
    return (jnp.concatenate(outs, axis=-1),)


def _f_loss(y, tgt):
    err = y - tgt
    per_row = jnp.sum(err * err, axis=-1, keepdims=True) * (0.5 / D_MODEL)
    loss = jnp.sum(per_row, axis=0, keepdims=True)
    dy = err * (1.0 / D_MODEL)
    return (dy, dy), (jnp.broadcast_to(loss, (1, LANES)),)


def _adam_core(w, g, m, v):
    m = ADAM_B1 * m + (1.0 - ADAM_B1) * g
    v = ADAM_B2 * v + (1.0 - ADAM_B2) * jnp.square(g)
    m_hat = m / (1.0 - ADAM_B1 ** ADAM_STEP)
    v_hat = v / (1.0 - ADAM_B2 ** ADAM_STEP)
    delta = -ADAM_LR * (m_hat / (jnp.sqrt(v_hat) + ADAM_EPS) + ADAM_WD * w)
    return delta, m, v


def _sum_slots(g8):
    g = g8[0].astype(F32)
    for s in range(1, N_DEV):
        g = g + g8[s].astype(F32)
    return g


def _f_adam_slots(g8, w, m, v):
    g = _sum_slots(g8)
    return (g,) + _adam_core(w, g, m, v), ()


def _f_sum_slots(g8):
    return (_sum_slots(g8),), ()


def _f_adam(g, w, m, v):
    return _adam_core(w, g, m, v), ()


def _conv_chunk(t):
    return 256 if t % 256 == 0 else t


def _conv_fwd(srcs, w, b, *, kk, pre_glu, post_silu, name):
    t = srcs[0][0].shape[0]
    c = w.shape[-1]
    tt = _conv_chunk(t)
    ns = len(srcs)

    def body(*refs):
        w_ref, b_ref, o_ref, pad_ref = refs[ns:]
        if pre_glu:
            xin = refs[0][...] * jax.nn.sigmoid(refs[1][...])
        else:
            xin = refs[0][...]
        pad_ref[0:CONV_PAD, :] = jnp.zeros((CONV_PAD, LANES), F32)
        pad_ref[CONV_PAD:, :] = xin

        def chunk(i, carry):
            base = pl.multiple_of(i * tt, tt)
            acc = jnp.broadcast_to(b_ref[...], (tt, LANES))
            for j in range(kk):
                acc = acc + pad_ref[pl.ds(base + CONV_PAD - (kk - 1) + j, tt), :] * w_ref[j:j + 1, :]
            o_ref[pl.ds(base, tt), :] = jax.nn.silu(acc) if post_silu else acc
            return carry
        lax.fori_loop(0, t // tt, chunk, 0)

    in_specs = [pl.BlockSpec((t, LANES), functools.partial(lambda i, off: (0, off + i), off=off)) for _, off in srcs]
    in_specs += [pl.BlockSpec((kk, LANES), lambda i: (0, i)), pl.BlockSpec((1, LANES), lambda i: (0, i))]
    return pl.pallas_call(
        body, name=name, grid=(c // LANES,), in_specs=in_specs, out_specs=pl.BlockSpec((t, LANES), lambda i: (0, i)),
        out_shape=jax.ShapeDtypeStruct((t, c), F32), scratch_shapes=[pltpu.VMEM((t + CONV_PAD, LANES), F32)],
        compiler_params=pltpu.CompilerParams(dimension_semantics=("parallel",)),
    )(*[s[0] for s in srcs], w, b)


def _conv_bwd(srcs, w, b, dys, *, kk, pre_glu, post_silu, name, dx_dtype):
    t = srcs[0][0].shape[0]
    c = w.shape[-1]
    tt = _conv_chunk(t)
    ns, nd = len(srcs), len(dys)

    def body(*refs):
        src_refs = refs[:ns]
        dy_refs = refs[ns:ns + nd]
        w_ref, b_ref = refs[ns + nd:ns + nd + 2]
        outs = refs[ns + nd + 2:]
        dx_refs, dw_ref, db_ref = outs[:ns], outs[ns], outs[ns + 1]
        pad_ref, dpad_ref = outs[ns + 2:]
        cb = pl.program_id(0)
        if pre_glu:
            a_in = src_refs[0][...]
            sg = jax.nn.sigmoid(src_refs[1][...])
            xin = a_in * sg
        else:
            xin = src_refs[0][...]
        pad_ref[0:CONV_PAD, :] = jnp.zeros((CONV_PAD, LANES), F32)
        pad_ref[CONV_PAD:, :] = xin
        dpad_ref[t:, :] = jnp.zeros((CONV_PAD, LANES), F32)
        dw_ref[...] = jnp.zeros_like(dw_ref)
        db_ref[...] = jnp.zeros_like(db_ref)

        def load_dy(base):
            dy = dy_refs[0][pl.ds(base, tt), :].astype(F32)
            for (_, first, _n), r in zip(dys[1:], dy_refs[1:]):
                dy = jnp.where(cb >= first, r[pl.ds(base, tt), :].astype(F32), dy)
            return dy

        def chunk1(i, carry):
            base = pl.multiple_of(i * tt, tt)
            dy = load_dy(base)
            if post_silu:
                acc = jnp.broadcast_to(b_ref[...], (tt, LANES))
                for j in range(kk):
                    acc = acc + pad_ref[pl.ds(base + CONV_PAD - (kk - 1) + j, tt), :] * w_ref[j:j + 1, :]
                s = jax.nn.sigmoid(acc)
                dy = dy * (s * (1.0 + acc * (1.0 - s)))
            dpad_ref[pl.ds(base, tt), :] = dy
            db_ref[...] += jnp.sum(dy, axis=0, keepdims=True)
            for j in range(kk):
                dw_ref[j:j + 1, :] += jnp.sum(dy * pad_ref[pl.ds(base + CONV_PAD - (kk - 1) + j, tt), :], axis=0, keepdims=True)
            return carry
        lax.fori_loop(0, t // tt, chunk1, 0)

        def chunk2(i, carry):
            base = pl.multiple_of(i * tt, tt)
            acc = jnp.zeros((tt, LANES), F32)
            for j in range(kk):
                acc = acc + dpad_ref[pl.ds(base + (kk - 1) - j, tt), :] * w_ref[j:j + 1, :]
            if pre_glu:
                a_c = src_refs[0][pl.ds(base, tt), :]
                s_c = jax.nn.sigmoid(src_refs[1][pl.ds(base, tt), :])
                dx_refs[0][pl.ds(base, tt), :] = (acc * s_c).astype(dx_dtype)
                dx_refs[1][pl.ds(base, tt), :] = (acc * a_c * s_c * (1.0 - s_c)).astype(dx_dtype)
            else:
                dx_refs[0][pl.ds(base, tt), :] = acc.astype(dx_dtype)
            return carry
        lax.fori_loop(0, t // tt, chunk2, 0)

    in_specs = [pl.BlockSpec((t, LANES), functools.partial(lambda i, off: (0, off + i), off=off)) for _, off in srcs]
    for _, first, n in dys:
        in_specs.append(pl.BlockSpec((t, LANES), functools.partial(lambda i, first, n: (0, jnp.clip(i - first, 0, n - 1)), first=first, n=n)))
    in_specs += [pl.BlockSpec((kk, LANES), lambda i: (0, i)), pl.BlockSpec((1, LANES), lambda i: (0, i))]
    out_specs = [pl.BlockSpec((t, LANES), lambda i: (0, i)) for _ in srcs]
    out_specs += [pl.BlockSpec((kk, LANES), lambda i: (0, i)), pl.BlockSpec((1, LANES), lambda i: (0, i))]
    out_shape = [jax.ShapeDtypeStruct((t, c), dx_dtype) for _ in srcs]
    out_shape += [jax.ShapeDtypeStruct((kk, c), F32), jax.ShapeDtypeStruct((1, c), F32)]
    return pl.pallas_call(
        body, name=name, grid=(c // LANES,), in_specs=in_specs, out_specs=out_specs, out_shape=out_shape,
        scratch_shapes=[pltpu.VMEM((t + CONV_PAD, LANES), F32), pltpu.VMEM((t + CONV_PAD, LANES), F32)],
        compiler_params=pltpu.CompilerParams(dimension_semantics=("parallel",)),
    )(*[s[0] for s in srcs], *[d[0] for d in dys], w, b)


def _ssd_chunk(xbc, dtfull, s_in, dt_bias, a_log, dskip):
    q = xbc.shape[0]
    gw = SSD_INNER // SSD_GROUPS
    x = xbc[:, :SSD_INNER]
    dt_all = jax.nn.softplus(dtfull + dt_bias)
    da_all = dt_all * (-jnp.exp(a_log))
    row = lax.broadcasted_iota(jnp.int32, (q, q), 0)
    col = lax.broadcasted_iota(jnp.int32, (q, q), 1)
    causal = row >= col
    cs = _hdot(causal.astype(F32), da_all, ((1,), (0,)))
    cs_last = cs[q - 1:q, :]
    head_of_lane = lax.shift_right_logical(lax.broadcasted_iota(jnp.int32, (LANES, SSD_INNER), 1), int(math.log2(SSD_HEAD_DIM)))
    spread = (lax.broadcasted_iota(jnp.int32, (LANES, SSD_INNER), 0) == head_of_lane).astype(F32)
    per_head = jnp.concatenate([dt_all, jnp.exp(cs), jnp.exp(cs_last - cs), jnp.broadcast_to(dskip, (8, LANES))], axis=0)
    wide = _hdot(per_head, spread, ((1,), (0,)))
    xdt = x * wide[0:q]
    from_start, to_end, d_wide = wide[q:2 * q], wide[2 * q:3 * q], wide[3 * q:3 * q + 1]
    xdt_end = xdt * to_end
    chunk_decay = jnp.exp(cs_last)
    decay_rows = jnp.concatenate([jnp.broadcast_to(chunk_decay[:, h:h + 1], (SSD_HEAD_DIM, 1)) for h in range(SSD_HEADS)], axis=0)
    cs_t = jnp.transpose(cs)
    y_off, new_states, y_diag = [], [], []
    for g in range(SSD_GROUPS):
        bm = xbc[:, SSD_INNER + g * SSD_STATE:SSD_INNER + (g + 1) * SSD_STATE]
        cm = xbc[:, SSD_INNER + (SSD_GROUPS + g) * SSD_STATE:SSD_INNER + (SSD_GROUPS + g + 1) * SSD_STATE]
        cb = _bdot(cm, bm, ((1,), (1,)))
        y_off.append(_bdot(cm, s_in[g * gw:(g + 1) * gw, :], ((1,), (1,))))
        new_states.append(_bdot(xdt_end[:, g * gw:(g + 1) * gw], bm, ((0,), (0,))))
        for j in range(HEADS_PER_GROUP):
            h = g * HEADS_PER_GROUP + j
            diff = cs[:, h:h + 1] - cs_t[h:h + 1, :]
            decay = jnp.where(causal, jnp.exp(jnp.where(causal, diff, 0.0)), 0.0)
            y_diag.append(_bdot(cb * decay, xdt[:, h * SSD_HEAD_DIM:(h + 1) * SSD_HEAD_DIM], ((1,), (0,))))
    s_out = s_in * decay_rows + jnp.concatenate(new_states, axis=0)
    y = jnp.concatenate(y_diag, axis=-1) + jnp.concatenate(y_off, axis=-1) * from_start + x * d_wide
    return y, s_out


def _ssd_fwd(xbc, pdt, dt_bias, a_log, dskip, *, name, rider=None):
    t = xbc.shape[0]
    nc = t // SSD_CHUNK
    ride = _Ride(rider, 5, 2, 1)

    def body(*refs):
        (x_ref, dt_ref, tb_ref, al_ref, d_ref), (y_ref, s_ref), (state_ref,) = ride.split(refs)
        c = pl.program_id(0)
        ride.start(refs, c == 0)

        @pl.when(c == 0)
        def _():
            state_ref[...] = jnp.zeros_like(state_ref)
        s_in = state_ref[...]
        s_ref[...] = s_in
        y, s_out = _ssd_chunk(x_ref[...], dt_ref[...], s_in, tb_ref[...], al_ref[...], d_ref[...])
        y_ref[...] = y
        state_ref[...] = s_out
        ride.finish(refs, c == nc - 1)

    vec = pl.BlockSpec((1, LANES), lambda c: (0, 0))
    res = pl.pallas_call(
        body, name=name, grid=(nc,),
        in_specs=[pl.BlockSpec((SSD_CHUNK, SSD_XBC), lambda c: (c, 0)), pl.BlockSpec((SSD_CHUNK, LANES), lambda c: (c, 0)), vec, vec, vec] + ride.in_specs,
        out_specs=[pl.BlockSpec((SSD_CHUNK, SSD_INNER), lambda c: (c, 0)), pl.BlockSpec((None, SSD_INNER, SSD_STATE), lambda c: (c, 0, 0))] + ride.out_specs,
        out_shape=[jax.ShapeDtypeStruct((t, SSD_INNER), F32), jax.ShapeDtypeStruct((nc, SSD_INNER, SSD_STATE), F32)] + ride.out_shape,
        scratch_shapes=[pltpu.VMEM((SSD_INNER, SSD_STATE), F32)] + ride.scratch,
        compiler_params=pltpu.CompilerParams(dimension_semantics=("arbitrary",)),
    )(xbc, pdt, dt_bias, a_log, dskip, *ride.args)
    return res[:2], res[2:]


def _ssd_bwd(xbc, pdt, states, dy, dt_bias, a_log, dskip, *, name, rider=None):
    t = xbc.shape[0]
    nc = t // SSD_CHUNK
    rev = lambda c: nc - 1 - c
    ride = _Ride(rider, 7, 5, 1)

    def body(*refs):
        (x_ref, dt_ref, s_ref, dy_ref, tb_ref, al_ref, d_ref), (dx_ref, ddt_ref, dtb_ref, dal_ref, dd_ref), (dstate_ref,) = ride.split(refs)
        c = pl.program_id(0)
        ride.start(refs, c == 0)

        @pl.when(c == 0)
        def _():
            dstate_ref[...] = jnp.zeros_like(dstate_ref)
            dtb_ref[...] = jnp.zeros_like(dtb_ref)
            dal_ref[...] = jnp.zeros_like(dal_ref)
            dd_ref[...] = jnp.zeros_like(dd_ref)
        _, vjp = jax.vjp(_ssd_chunk, x_ref[...], dt_ref[...], s_ref[...], tb_ref[...], al_ref[...], d_ref[...])
        dx, ddt, ds_in, dtb, dal, dd = vjp((dy_ref[...].astype(F32), dstate_ref[...]))
        dx_ref[...] = dx
        ddt_ref[...] = ddt
        dstate_ref[...] = ds_in
        dtb_ref[...] += dtb
        dal_ref[...] += dal
        dd_ref[...] += dd
        ride.finish(refs, c == nc - 1)

    vec = pl.BlockSpec((1, LANES), lambda c: (0, 0))
    in_specs = [pl.BlockSpec((SSD_CHUNK, SSD_XBC), lambda c: (rev(c), 0)), pl.BlockSpec((SSD_CHUNK, LANES), lambda c: (rev(c), 0)),
                pl.BlockSpec((None, SSD_INNER, SSD_STATE), lambda c: (rev(c), 0, 0)), pl.BlockSpec((SSD_CHUNK, SSD_INNER), lambda c: (rev(c), 0)),
                vec, vec, vec]
    out_specs = [pl.BlockSpec((SSD_CHUNK, SSD_XBC), lambda c: (rev(c), 0)), pl.BlockSpec((SSD_CHUNK, LANES), lambda c: (rev(c), 0)), vec, vec, vec]
    out_shape = [jax.ShapeDtypeStruct((t, SSD_XBC), F32), jax.ShapeDtypeStruct((t, LANES), F32)] + [jax.ShapeDtypeStruct((1, LANES), F32)] * 3
    res = pl.pallas_call(
        body, name=name, grid=(nc,), in_specs=in_specs + ride.in_specs, out_specs=out_specs + ride.out_specs,
        out_shape=out_shape + ride.out_shape, scratch_shapes=[pltpu.VMEM((SSD_INNER, SSD_STATE), F32)] + ride.scratch,
        compiler_params=pltpu.CompilerParams(dimension_semantics=("arbitrary",)),
    )(xbc, pdt, states, dy, dt_bias, a_log, dskip, *ride.args)
    return res[:5], res[5:]


def _rel_buckets():
    qi = np.arange(ATTN_BLOCK)[:, None] + ATTN_BLOCK
    kj = np.arange(2 * ATTN_BLOCK)[None, :]
    dist = qi - kj
    max_exact = REL_BUCKETS // 2
    d = np.maximum(dist, 1).astype(np.float32)
    large = max_exact + (np.log(d / np.float32(max_exact)) / np.float32(math.log(REL_MAX_DIST / max_exact))
                         * np.float32(REL_BUCKETS - max_exact)).astype(np.int32)
    large = np.minimum(large, REL_BUCKETS - 1)
    return np.where(dist < max_exact, np.maximum(dist, 0), large).astype(np.int32)


def _onehot_buckets(bucket_ref):
    n = bucket_ref.shape[-1]
    return (lax.broadcasted_iota(jnp.int32, (REL_BUCKETS, n), 0) == bucket_ref[...]).astype(F32)


def _band_bias(rel_table_t, buckets):
    n = buckets.shape[-1]

    def body(rt_ref, bk_ref, o_ref):
        o_ref[...] = _hdot(rt_ref[...], _onehot_buckets(bk_ref), ((1,), (0,)))
    return pl.pallas_call(body, name="band_bias", out_shape=jax.ShapeDtypeStruct((ATTN_HEADS, n), F32))(rel_table_t, buckets)


def _band_bias_bwd(dbias, buckets):
    def body(db_ref, bk_ref, o_ref):
        d = db_ref[0]
        for layer in range(1, db_ref.shape[0]):
            d = d + db_ref[layer]
        o_ref[...] = _hdot(d, _onehot_buckets(bk_ref), ((1,), (1,)))
    return pl.pallas_call(body, name="band_bias_bwd", out_shape=jax.ShapeDtypeStruct((ATTN_HEADS, REL_BUCKETS), F32))(dbias, buckets)


def _attn_block(q, kvp, kvc, bias, sinks, qg, kg, first):
    qn = q.shape[0]
    rows = ATTN_REP * qn
    ri = lax.broadcasted_iota(jnp.int32, (rows, 2 * qn), 0) & (qn - 1)
    cj = lax.broadcasted_iota(jnp.int32, (rows, 2 * qn), 1)
    jj = cj & (qn - 1)
    no_prev = jnp.where(first, qn, 0)
    mask = ((cj < qn) & (jj > ri + no_prev)) | ((cj >= qn) & (jj <= ri))
    kvd = ATTN_KV_HEADS * ATTN_HEAD_DIM
    outs = []
    for g in range(ATTN_KV_HEADS):
        sl = slice(g * ATTN_HEAD_DIM, (g + 1) * ATTN_HEAD_DIM)
        vsl = slice(kvd + g * ATTN_HEAD_DIM, kvd + (g + 1) * ATTN_HEAD_DIM)
        qs = jnp.concatenate([_rms(q[:, (g * ATTN_REP + j) * ATTN_HEAD_DIM:(g * ATTN_REP + j + 1) * ATTN_HEAD_DIM], qg)
                              for j in range(ATTN_REP)], axis=0)
        kb = jnp.concatenate([_rms(kvp[:, sl], kg), _rms(kvc[:, sl], kg)], axis=0)
        vb = jnp.concatenate([kvp[:, vsl], kvc[:, vsl]], axis=0)
        logits = _bdot(qs, kb, ((1,), (1,))) * (ATTN_HEAD_DIM ** -0.5) + bias[g]
        logits = jnp.where(mask, logits, NEG_INF)
        sink = jnp.concatenate([jnp.broadcast_to(sinks[:, g * ATTN_REP + j:g * ATTN_REP + j + 1], (qn, 1)) for j in range(ATTN_REP)], axis=0)
        m = jnp.maximum(jnp.max(logits, axis=-1, keepdims=True), sink)
        pexp = jnp.exp(logits - m)
        probs = pexp * (1.0 / (jnp.sum(pexp, axis=-1, keepdims=True) + jnp.exp(sink - m)))
        o = _bdot(probs, vb, ((1,), (0,)))
        outs += [o[j * qn:(j + 1) * qn, :] for j in range(ATTN_REP)]
    return jnp.concatenate(outs, axis=-1)


def _attn_specs(nmap):
    qd, kvw = ATTN_HEADS * ATTN_HEAD_DIM, 2 * ATTN_KV_HEADS * ATTN_HEAD_DIM
    full = lambda shp: pl.BlockSpec(shp, lambda i: (0,) * len(shp))
    return [
        pl.BlockSpec((ATTN_BLOCK, qd), lambda i: (nmap(i), 0)),
        pl.BlockSpec((ATTN_BLOCK, kvw), lambda i: (jnp.maximum(nmap(i) - 1, 0), qd // kvw)),
        pl.BlockSpec((ATTN_BLOCK, kvw), lambda i: (nmap(i), qd // kvw)),
        full((ATTN_KV_HEADS, ATTN_REP * ATTN_BLOCK, 2 * ATTN_BLOCK)), full((1, ATTN_HEADS)), full((1, ATTN_HEAD_DIM)), full((1, ATTN_HEAD_DIM)),
    ]


def _attn_fwd(pqkv, bias, sinks, qg, kg, *, name, rider=None):
    t = pqkv.shape[0]
    nb = t // ATTN_BLOCK
    qd = ATTN_HEADS * ATTN_HEAD_DIM
    ride = _Ride(rider, 7, 1, 0)

    def body(*refs):
        (q_ref, kvp_ref, kvc_ref, bias_ref, sk_ref, qg_ref, kg_ref), (o_ref,), _ = ride.split(refs)
        i = pl.program_id(0)
        ride.start(refs, i == 0)
        o_ref[...] = _attn_block(q_ref[...], kvp_ref[...], kvc_ref[...], bias_ref[...], sk_ref[...], qg_ref[...], kg_ref[...], i == 0).astype(o_ref.dtype)
        ride.finish(refs, i == nb - 1)

    res = pl.pallas_call(
        body, name=name, grid=(nb,), in_specs=_attn_specs(lambda i: i) + ride.in_specs,
        out_specs=[pl.BlockSpec((ATTN_BLOCK, qd), lambda i: (i, 0))] + ride.out_specs,
        out_shape=[jax.ShapeDtypeStruct((t, qd), BF16)] + ride.out_shape, scratch_shapes=ride.scratch,
        compiler_params=pltpu.CompilerParams(dimension_semantics=("arbitrary",)),
    )(pqkv, pqkv, pqkv, bias, sinks, qg, kg, *ride.args)
    return res[0], res[1:]


def _attn_bwd(pqkv, do, bias, sinks, qg, kg, *, name, rider=None):
    t = pqkv.shape[0]
    nb = t // ATTN_BLOCK
    qd, kvw = ATTN_HEADS * ATTN_HEAD_DIM, 2 * ATTN_KV_HEADS * ATTN_HEAD_DIM
    rev = lambda i: nb - 1 - i
    ride = _Ride(rider, 8, 5, 1)

    def body(*refs):
        (q_ref, kvp_ref, kvc_ref, bias_ref, sk_ref, qg_ref, kg_ref, do_ref), (dqkv_ref, dbias_ref, dsk_ref, dqg_ref, dkg_ref), (carry_ref,) = ride.split(refs)
        i = pl.program_id(0)
        ride.start(refs, i == 0)
        first = rev(i) == 0
        f = functools.partial(_attn_block, first=first)
        _, vjp = jax.vjp(f, q_ref[...], kvp_ref[...], kvc_ref[...], bias_ref[...], sk_ref[...], qg_ref[...], kg_ref[...])
        dq, dkvp, dkvc, dbias, dsk, dqg, dkg = vjp(do_ref[...].astype(F32))

        @pl.when(i == 0)
        def _():
            carry_ref[...] = jnp.zeros_like(carry_ref)
            dbias_ref[...] = jnp.zeros_like(dbias_ref)
            dsk_ref[...] = jnp.zeros_like(dsk_ref)
            dqg_ref[...] = jnp.zeros_like(dqg_ref)
            dkg_ref[...] = jnp.zeros_like(dkg_ref)
        dqkv_ref[:, 0:qd] = dq.astype(dqkv_ref.dtype)
        dqkv_ref[:, qd:] = (dkvc + carry_ref[...]).astype(dqkv_ref.dtype)
        carry_ref[...] = dkvp
        dbias_ref[...] += dbias
        dsk_ref[...] += dsk
        dqg_ref[...] += dqg
        dkg_ref[...] += dkg
        ride.finish(refs, i == nb - 1)

    full = lambda shp: pl.BlockSpec(shp, lambda i: (0,) * len(shp))
    bshape = (ATTN_KV_HEADS, ATTN_REP * ATTN_BLOCK, 2 * ATTN_BLOCK)
    res = pl.pallas_call(
        body, name=name, grid=(nb,), in_specs=_attn_specs(rev) + [pl.BlockSpec((ATTN_BLOCK, qd), lambda i: (rev(i), 0))] + ride.in_specs,
        out_specs=[pl.BlockSpec((ATTN_BLOCK, qd + kvw), lambda i: (rev(i), 0)), full(bshape), full((1, ATTN_HEADS)),
                   full((1, ATTN_HEAD_DIM)), full((1, ATTN_HEAD_DIM))] + ride.out_specs,
        out_shape=[jax.ShapeDtypeStruct((t, qd + kvw), BF16), jax.ShapeDtypeStruct(bshape, F32), jax.ShapeDtypeStruct((1, ATTN_HEADS), F32),
                   jax.ShapeDtypeStruct((1, ATTN_HEAD_DIM), F32), jax.ShapeDtypeStruct((1, ATTN_HEAD_DIM), F32)] + ride.out_shape,
        scratch_shapes=[pltpu.VMEM((ATTN_BLOCK, kvw), F32)] + ride.scratch,
        compiler_params=pltpu.CompilerParams(dimension_semantics=("arbitrary",)),
    )(pqkv, pqkv, pqkv, bias, sinks, qg, kg, do, *ride.args)
    return res[:5], res[5:]


def _dev_index(dev):
    return 4 * dev[0] + 2 * dev[1] + dev[2]


def _gather_ops(x_refs, o_refs, send_sems, recv_sems, local_sems, axes):
    n = len(x_refs)
    x, y, c = lax.axis_index("x"), lax.axis_index("y"), lax.axis_index("c")
    me, sibling = (x, y, c), (x, y, 1 - c)
    chips = [(1 - x, y), (x, 1 - y), (1 - x, 1 - y)]

    def slot(i, dev):
        idx = _dev_index(dev)
        return o_refs[i].at[idx] if axes[i] == 0 else o_refs[i].at[:, idx]

    def copy(i, k, block, to, src=None):
        return pltpu.make_async_remote_copy(
            src_ref=slot(i, block) if src is None else src, dst_ref=slot(i, block),
            send_sem=send_sems.at[i, k], recv_sem=recv_sems.at[i, k], device_id=to, device_id_type=MESH)

    mine = [pltpu.make_async_copy(x_refs[i], slot(i, me), local_sems.at[i]) for i in range(n)]
    first = []
    for i in range(n):
        first.append(copy(i, 0, me, sibling, src=x_refs[i]))
        first += [copy(i, 1 + j, me, (*chip, c), src=x_refs[i]) for j, chip in enumerate(chips)]

    def start():
        for cp in mine + first:
            cp.start()

    def finish():
        passed = []
        for j, chip in enumerate(chips):
            for i in range(n):
                copy(i, 1 + j, (*chip, c), me).wait_recv()
                cp = copy(i, 4 + j, (*chip, c), sibling)
                cp.start()
                passed.append(cp)
        for i in range(n):
            copy(i, 0, sibling, me).wait_recv()
        for j, chip in enumerate(chips):
            for i in range(n):
                copy(i, 4 + j, (*chip, 1 - c), me).wait_recv()
        for cp in first + passed:
            cp.wait_send()
        for cp in mine:
            cp.wait()

    return start, finish


def _exchange_ops(g_refs, o_refs, send_sems, recv_sems, local_sems):
    n = len(g_refs)
    x, y, c = lax.axis_index("x"), lax.axis_index("y"), lax.axis_index("c")
    me = _dev_index((x, y, c))
    peers = [(x ^ ((k >> 2) & 1), y ^ ((k >> 1) & 1), c ^ (k & 1)) for k in range(1, N_DEV)]

    def copy(i, k):
        peer = peers[k]
        return pltpu.make_async_remote_copy(
            src_ref=g_refs[i].at[_dev_index(peer)], dst_ref=o_refs[i].at[me],
            send_sem=send_sems.at[i, k], recv_sem=recv_sems.at[i, k], device_id=peer, device_id_type=MESH)

    def arrival(i, k):
        peer = peers[k]
        return pltpu.make_async_remote_copy(
            src_ref=g_refs[i].at[me], dst_ref=o_refs[i].at[_dev_index(peer)],
            send_sem=send_sems.at[i, k], recv_sem=recv_sems.at[i, k], device_id=peer, device_id_type=MESH)

    mine = [pltpu.make_async_copy(g_refs[i].at[me], o_refs[i].at[me], local_sems.at[i]) for i in range(n)]
    sends = [copy(i, k) for i in range(n) for k in range(N_DEV - 1)]

    def start():
        for cp in mine + sends:
            cp.start()

    def finish():
        for i in range(n):
            for k in range(N_DEV - 1):
                arrival(i, k).wait_recv()
        for cp in sends:
            cp.wait_send()
        for cp in mine:
            cp.wait()

    return start, finish


class _Ride:
    def __init__(self, rider, n_in, n_out, n_scr):
        self.rider, self.n_in, self.n_out, self.n_scr = rider, n_in, n_out, n_scr
        self.args = [] if rider is None else list(rider[1])
        n = self.n = len(self.args)
        hbm = pl.BlockSpec(memory_space=pltpu.HBM)
        self.in_specs, self.out_specs = [hbm] * n, [hbm] * n
        if rider is None:
            self.out_shape, self.scratch = [], []
            return
        if rider[0] == 'gather':
            self.out_shape = [jax.ShapeDtypeStruct(a.shape[:ax] + (N_DEV,) + a.shape[ax:], a.dtype) for a, ax in zip(self.args, rider[2])]
        else:
            self.out_shape = [jax.ShapeDtypeStruct(a.shape, a.dtype) for a in self.args]
        self.scratch = [pltpu.SemaphoreType.DMA((n, 7)), pltpu.SemaphoreType.DMA((n, 7)), pltpu.SemaphoreType.DMA((n,))]

    def split(self, refs):
        a = self.n_in
        c = a + self.n + self.n_out
        e = c + self.n
        return refs[:a], refs[a + self.n:c], refs[e:e + self.n_scr]

    def _ops(self, refs):
        a = self.n_in
        c = a + self.n + self.n_out
        e = c + self.n + self.n_scr
        x_refs, o_refs, sems = refs[a:a + self.n], refs[c:c + self.n], refs[e:e + 3]
        if self.rider[0] == 'gather':
            return _gather_ops(x_refs, o_refs, *sems, self.rider[2])
        return _exchange_ops(x_refs, o_refs, *sems)

    def start(self, refs, cond):
        if self.rider is not None:
            pl.when(cond)(lambda: self._ops(refs)[0]())

    def finish(self, refs, cond):
        if self.rider is not None:
            pl.when(cond)(lambda: self._ops(refs)[1]())


def _comm_call(rider, *, name):
    ride = _Ride(rider, 0, 0, 0)

    def body(*refs):
        start, finish = ride._ops(refs)
        start()
        finish()

    return pl.pallas_call(body, name=name, in_specs=ride.in_specs, out_specs=ride.out_specs, out_shape=ride.out_shape,
                          scratch_shapes=ride.scratch)(*ride.args)


def _all_gather(xs, axes, *, name):
    return _comm_call(('gather', xs, axes), name=name)


def _exchange(gs, *, name):
    return _comm_call(('exchange', gs), name=name)


def _pack(arrays):
    parts = []
    for a in arrays:
        flat = a.reshape(-1)
        pad = (-flat.shape[0]) % LANES
        if pad:
            flat = jnp.concatenate([flat, jnp.zeros((pad,), flat.dtype)])
        parts.append(flat.reshape(-1, LANES))
    return jnp.concatenate(parts, axis=0)


def _unpack(buf, shapes):
    out, row = [], 0
    for shp in shapes:
        size = int(np.prod(shp))
        rows = -(-size // LANES)
        out.append(buf[row:row + rows].reshape(-1)[:size].reshape(shp))
        row += rows
    return out


def _row_tile(rows, width, n_bufs):
    padded = -(-width // LANES) * LANES
    cap = max(16, (12 << 20) // (padded * 4 * n_bufs))
    if rows <= cap:
        return rows
    tr = (cap // 16) * 16
    while tr > 16 and rows % tr:
        tr -= 16
    return tr if rows % tr == 0 else rows


def _step(p, m, v, x, mem, loss_target):
    t = x.shape[1]
    h0 = x.reshape(t, D_MODEL)
    mem2 = mem.reshape(MEM_LEN, D_MODEL)
    tgt = loss_target.reshape(t, D_MODEL)
    tr = 256 if t % 256 == 0 else t
    my = _dev_index((lax.axis_index("x"), lax.axis_index("y"), lax.axis_index("c")))

    small_sh_shapes = [p[n].shape for n in SMALL_SHARDED]
    gathered_small = _all_gather([_pack([p[n] for n in SMALL_SHARDED])], [0], name="gather_small")[0]
    full = {}
    for n, a in zip(SMALL_SHARDED, zip(*[_unpack(gathered_small[d], small_sh_shapes) for d in range(N_DEV)])):
        full[n] = jnp.concatenate(a, axis=-1)
    big_names = ROW_SHARDED + COL_SHARDED
    seg_order = ['a', 'ag', 'z', 'x', 'dt', 'qkv', 'g']
    wg = {n: [] for n in big_names}
    w_seg = {k: [] for k in seg_order}
    with_ssd = ['w_in']
    with_swa = [n for n in big_names if n != 'w_in']

    def shards(layer, names):
        return [p[n][layer].astype(BF16) for n in names]

    def use_weights(got):
        for n in ROW_SHARDED:
            a = got[n]
            wg[n].append(a.reshape(a.shape[0] * a.shape[1], a.shape[2]))
        for n in ('w_xkv', 'w_mlp_up'):
            wg[n].append(got[n])
        w_in_full = jnp.transpose(got['w_in'], (1, 0, 2)).reshape(D_MODEL, IN_COLS)
        segs = {'a': w_in_full[:, 0:D_MODEL], 'ag': w_in_full[:, D_MODEL:OFF_Z], 'z': w_in_full[:, OFF_Z:OFF_XBC], 'x': w_in_full[:, OFF_XBC:OFF_DT],
                'dt': jnp.pad(w_in_full[:, OFF_DT:OFF_Q], ((0, 0), (0, LANES - SSD_HEADS))),
                'qkv': w_in_full[:, OFF_Q:OFF_GATE], 'g': w_in_full[:, OFF_GATE:IN_COLS]}
        for k in seg_order:
            w_seg[k].append(segs[k])

    use_weights(dict(zip(big_names, _all_gather(shards(0, big_names), [0] * len(big_names), name="gather_weights"))))

    def vec(name, layer, width=None):
        a = p[name][layer].reshape(1, -1)
        if width is not None and a.shape[1] < width:
            a = jnp.pad(a, ((0, 0), (0, width - a.shape[1])))
        return a

    buckets = jnp.asarray(_rel_buckets().reshape(1, -1))
    bias = _band_bias(jnp.transpose(p['rel_table']), buckets).reshape(ATTN_KV_HEADS, ATTN_REP * ATTN_BLOCK, 2 * ATTN_BLOCK)

    saved = []
    h = h0
    for l in range(DEPTH):
        s = {'h0': h}
        u = _rowwise(lambda a, g: (_f_rms(a, g), ()), [(h, 0, D_MODEL)], [vec('norm_mix', l)], [(D_MODEL, BF16)], tr=tr, name="rms_mix")[0]
        s['u'] = u
        pr = {k: _mm(u, (w_seg[k], 'plain', l), mode='nn', name="proj_" + k) for k in seg_order}
        s['pr'] = pr
        dw_w, dw_b = full['conv_dw_w'][l], vec('conv_dw_b', l)
        ca = _conv_fwd([(pr['a'], 0), (pr['ag'], 0)], dw_w, dw_b, kk=CONV_KERNEL, pre_glu=True, post_silu=False, name="conv31")
        s['ca'] = ca
        ya_in = _rowwise(lambda a, g, b: (_f_lnsilu(a, g, b), ()), [(ca, 0, D_MODEL)], [vec('conv_ln_g', l), vec('conv_ln_b', l)],
                         [(D_MODEL, BF16)], tr=tr, name="ln_silu")[0]
        s['ya_in'] = ya_in
        y_a = _mm(ya_in, (wg['w_conv_out'], 'plain', l), mode='nn', name="conv_out")
        xbc = _conv_fwd([(pr['x'], 0)], full['ssd_conv_w'][l], vec('ssd_conv_b', l), kk=SSD_CONV, pre_glu=False, post_silu=True, name="conv4")
        s['xbc'] = xbc
        ssd_vecs = [vec('ssd_dt_bias', l, LANES), vec('ssd_A_log', l, LANES), vec('ssd_D', l, LANES)]
        nxt = {}
        more = l + 1 < DEPTH
        (y_ssd, states), got = _ssd_fwd(xbc, pr['dt'], *ssd_vecs, name="ssd",
                                        rider=('gather', shards(l + 1, with_ssd), [0] * len(with_ssd)) if more else None)
        nxt.update(zip(with_ssd, got))
        s['y_ssd'], s['states'] = y_ssd, states
        yb_in = _rowwise(lambda a, z, g: (_f_ssdgate(a, z, g), ()), [(y_ssd, 0, SSD_INNER), (pr['z'], 0, SSD_INNER)], [vec('ssd_norm_g', l)],
                         [(SSD_INNER, BF16)], tr=tr, name="ssd_gate")[0]
        s['yb_in'] = yb_in
        y_b = _mm(yb_in, (wg['w_ssd_out'], 'plain', l), mode='nn', name="ssd_out")
        att, got = _attn_fwd(pr['qkv'], bias, vec('attn_sinks', l), vec('attn_q_norm', l), vec('attn_k_norm', l), name="swa",
                             rider=('gather', shards(l + 1, with_swa), [0] * len(with_swa)) if more else None)
        nxt.update(zip(with_swa, got))
        if more:
            use_weights(nxt)
        s['att'] = att
        y_c = _mm(att, (wg['w_attn_out'], 'plain', l), mode='nn', name="attn_out")
        s['y_a'], s['y_b'], s['y_c'] = y_a, y_b, y_c
        merged = _rowwise(lambda pg, a, b, c, gb: (_f_merge(pg, a, b, c, gb), ()),
                          [(pr['g'], 0, 3 * D_MODEL), (y_a, 0, D_MODEL), (y_b, 0, D_MODEL), (y_c, 0, D_MODEL)], [full['gate_bias'][l]],
                          [(D_MODEL, BF16)], tr=tr, name="merge")[0]
        s['merged'] = merged
        h = _mm(merged, (wg['w_mix_out'], 'plain', l), mode='nn', add=h, name="mix_out")
        s['h1'] = h
        un = _rowwise(lambda a, g: (_f_rms(a, g), ()), [(h, 0, D_MODEL)], [vec('norm_xattn', l)], [(D_MODEL, BF16)], tr=tr, name="rms_xattn")[0]
        memn = _rowwise(lambda a, g: (_f_rms(a, g), ()), [(mem2, 0, D_MODEL)], [vec('norm_mem', l)], [(D_MODEL, BF16)], tr=MEM_LEN, name="rms_mem")[0]
        s['un'], s['memn'] = un, memn
        xq = _mm(un, (wg['w_xq'], 'plain', l), mode='nn', name="xq")
        kv = _mm(memn, (wg['w_xkv'], 'col', l), mode='nn', name="xkv", tn=256)
        s['xq'], s['kv'] = xq, kv
        xo = _rowwise(lambda q, kvv, qg, kg: (_f_xattn(q, kvv, qg, kg), ()), [(xq, 0, D_MODEL)], [kv, vec('xattn_q_norm', l), vec('xattn_k_norm', l)],
                      [(D_MODEL, BF16)], tr=tr, name="xattn")[0]
        s['xo'] = xo
        h = _mm(xo, (wg['w_xo'], 'plain', l), mode='nn', add=h, name="xattn_out")
        s['h2'] = h
        um = _rowwise(lambda a, g: (_f_rms(a, g), ()), [(h, 0, D_MODEL)], [vec('norm_mlp', l)], [(D_MODEL, BF16)], tr=tr, name="rms_mlp")[0]
        s['um'] = um
        up, act = _mm(um, (wg['w_mlp_up'], 'col', l), mode='nn', name="mlp_up", epi=(lambda acc: (acc,) + _f_relu2(acc), [], [F32, BF16]))
        s['up'], s['act'] = up, act
        h = _mm(act, (wg['w_mlp_down'], 'plain', l), mode='nn', add=h, name="mlp_down")
        saved.append(s)

    dh, dh_b, loss_part = _rowwise(_f_loss, [(h, 0, D_MODEL), (tgt, 0, D_MODEL)], [], [(D_MODEL, F32), (D_MODEL, BF16)], [(1, LANES)], tr=tr, name="loss")

    sg = {n: [None] * DEPTH for n in SMALL if n != 'rel_table'}
    dbias_layers = [None] * DEPTH
    recv = {}
    with_d_swa = ['w_mlp_down', 'w_mlp_up', 'w_xo', 'w_xq', 'w_xkv', 'w_mix_out', 'w_attn_out']
    left_over = []

    def by_device(g_):
        return g_.reshape(N_DEV, g_.shape[0] // N_DEV, g_.shape[1])

    for l in reversed(range(DEPTH)):
        s = saved[l]
        bg = {}
        dup = _mm(dh_b, (wg['w_mlp_down'], 'plain', l), mode='nt', name="d_act",
                  epi=(lambda acc, up_: _vjp_rows(_f_relu2, 1, 1)(up_, acc)[0], [s['up']], [BF16]))[0]
        bg['w_mlp_down'] = _mm(s['act'], dh_b, mode='tn', out_dtype=BF16, name="dw_mlp_down")
        bg['w_mlp_up'] = _mm(s['um'], dup, mode='tn', out_dtype=BF16, out_col=True, name="dw_mlp_up")
        dum = _mm(dup, (wg['w_mlp_up'], 'col', l), mode='nt', name="d_um", tk=512)
        dh, dh_b, dg = _rms_bwd_call(s['h2'], vec('norm_mlp', l), dum, dh, tr, "d_rms_mlp")
        sg['norm_mlp'][l] = dg
        dxo = _mm(dh_b, (wg['w_xo'], 'plain', l), mode='nt', out_dtype=BF16, name="d_xo")
        bg['w_xo'] = _mm(s['xo'], dh_b, mode='tn', out_dtype=BF16, name="dw_xo")
        qg, kg = vec('xattn_q_norm', l), vec('xattn_k_norm', l)
        dxq, dkv, dqg, dkg = _xattn_bwd_call(s['xq'], s['kv'], qg, kg, dxo, tr)
        sg['xattn_q_norm'][l], sg['xattn_k_norm'][l] = dqg, dkg
        bg['w_xq'] = _mm(s['un'], dxq, mode='tn', out_dtype=BF16, name="dw_xq")
        dun = _mm(dxq, (wg['w_xq'], 'plain', l), mode='nt', name="d_un")
        dh, dh_b, dg = _rms_bwd_call(s['h1'], vec('norm_xattn', l), dun, dh, tr, "d_rms_xattn")
        sg['norm_xattn'][l] = dg
        bg['w_xkv'] = _mm(s['memn'], dkv, mode='tn', out_dtype=BF16, out_col=True, name="dw_xkv", tn=256)
        dmemn = _mm(dkv, (wg['w_xkv'], 'col', l), mode='nt', name="d_memn", tk=256)
        _, _, dg = _rms_bwd_call(mem2, vec('norm_mem', l), dmemn, jnp.zeros_like(mem2), MEM_LEN, "d_rms_mem")
        sg['norm_mem'][l] = dg
        dmerged = _mm(dh_b, (wg['w_mix_out'], 'plain', l), mode='nt', out_dtype=BF16, name="d_merged")
        bg['w_mix_out'] = _mm(s['merged'], dh_b, mode='tn', out_dtype=BF16, name="dw_mix_out")
        pr = s['pr']
        gb = full['gate_bias'][l]
        dpg, dya, dyb, dyc, dgb = _merge_bwd_call(pr['g'], s['y_a'], s['y_b'], s['y_c'], gb, dmerged, tr)
        sg['gate_bias'][l] = dgb
        dseg = {'g': dpg}
        datt = _mm(dyc, (wg['w_attn_out'], 'plain', l), mode='nt', out_dtype=BF16, name="d_att")
        bg['w_attn_out'] = _mm(s['att'], dyc, mode='tn', out_dtype=BF16, name="dw_attn_out")
        early = [bg[n] if n in COL_SHARDED else by_device(bg[n]) for n in with_d_swa]
        (dqkv, dbias_l, dsk, dqn, dkn), got = _attn_bwd(pr['qkv'], datt, bias, vec('attn_sinks', l), vec('attn_q_norm', l), vec('attn_k_norm', l),
                                                     name="d_swa", rider=('exchange', early))
        recv.update({(n, l): r for n, r in zip(with_d_swa, got)})
        dseg['qkv'] = dqkv
        dbias_layers[l] = dbias_l
        sg['attn_sinks'][l], sg['attn_q_norm'][l], sg['attn_k_norm'][l] = dsk, dqn, dkn
        dyb_in = _mm(dyb, (wg['w_ssd_out'], 'plain', l), mode='nt', out_dtype=BF16, name="d_yb_in")
        bg['w_ssd_out'] = _mm(s['yb_in'], dyb, mode='tn', out_dtype=BF16, name="dw_ssd_out")
        ng = vec('ssd_norm_g', l)
        dy_ssd, dz, dng = _ssdgate_bwd_call(s['y_ssd'], pr['z'], ng, dyb_in, tr)
        sg['ssd_norm_g'][l] = dng
        dseg['z'] = dz
        ssd_vecs = [vec('ssd_dt_bias', l, LANES), vec('ssd_A_log', l, LANES), vec('ssd_D', l, LANES)]
        late = [('w_ssd_out', l, by_device(bg['w_ssd_out']))] + left_over
        (dxbc_act, ddt, dtb, dal, ddsk), got = _ssd_bwd(s['xbc'], pr['dt'], s['states'], dy_ssd, *ssd_vecs, name="d_ssd",
                                                        rider=('exchange', [a for _, _, a in late]))
        recv.update({(n, ll): r for (n, ll, _), r in zip(late, got)})
        dseg['dt'] = ddt
        sg['ssd_dt_bias'][l], sg['ssd_A_log'][l], sg['ssd_D'][l] = dtb[:, :SSD_HEADS], dal[:, :SSD_HEADS], ddsk[:, :SSD_HEADS]
        dxbc, dcw, dcb = _conv_bwd([(pr['x'], 0)], full['ssd_conv_w'][l], vec('ssd_conv_b', l), [(dxbc_act, 0, SSD_XBC // LANES)],
                                   kk=SSD_CONV, pre_glu=False, post_silu=True, name="d_conv4", dx_dtype=BF16)
        dseg['x'] = dxbc
        sg['ssd_conv_w'][l], sg['ssd_conv_b'][l] = dcw, dcb
        dya_in = _mm(dya, (wg['w_conv_out'], 'plain', l), mode='nt', out_dtype=BF16, name="d_ya_in")
        bg['w_conv_out'] = _mm(s['ya_in'], dya, mode='tn', out_dtype=BF16, name="dw_conv_out")
        lg, lb = vec('conv_ln_g', l), vec('conv_ln_b', l)
        dca, dlg, dlb = _lnsilu_bwd_call(s['ca'], lg, lb, dya_in, tr)
        sg['conv_ln_g'][l], sg['conv_ln_b'][l] = dlg, dlb
        dseg['a'], dseg['ag'], dww, dwb = _conv_bwd([(pr['a'], 0), (pr['ag'], 0)], full['conv_dw_w'][l], vec('conv_dw_b', l),
                                                    [(dca, 0, D_MODEL // LANES)], kk=CONV_KERNEL, pre_glu=True, post_silu=False, name="d_conv31", dx_dtype=BF16)
        sg['conv_dw_w'][l], sg['conv_dw_b'][l] = dww, dwb
        du = None
        dw_parts = []
        for k in seg_order:
            du = _mm(dseg[k], (w_seg[k], 'plain', l), mode='nt', add=du, name="d_u_" + k)
            dw_k = _mm(s['u'], dseg[k], mode='tn', out_dtype=BF16, name="dw_in_" + k)
            dw_parts.append(dw_k[:, :SSD_HEADS] if k == 'dt' else dw_k)
        dw_in = jnp.concatenate(dw_parts, axis=-1)
        dw_in = jnp.transpose(dw_in.reshape(D_MODEL, N_DEV, IN_COLS // N_DEV), (1, 0, 2))
        dh, dh_b, dg = _rms_bwd_call(s['h0'], vec('norm_mix', l), du, dh, tr, "d_rms_mix")
        sg['norm_mix'][l] = dg
        left_over = [('w_conv_out', l, by_device(bg['w_conv_out'])), ('w_in', l, dw_in)]

    got = _exchange([a for _, _, a in left_over], name="exchange_grads")
    recv.update({(n, ll): r for (n, ll, _), r in zip(left_over, got)})
    grad_x = dh.reshape(x.shape)
    d_rel = jnp.transpose(_band_bias_bwd(jnp.stack(dbias_layers).reshape(DEPTH, ATTN_HEADS, -1), buckets))

    small_full = {'rel_table': d_rel}
    for n in SMALL:
        if n != 'rel_table':
            small_full[n] = jnp.stack(sg[n]).reshape((DEPTH,) + (full[n].shape[1:] if n in SMALL_SHARDED else p[n].shape[1:]))
    small_shapes = [(1, LANES)] + [small_full[n].shape for n in SMALL]
    packed = _pack([loss_part] + [small_full[n] for n in SMALL])
    slots = _all_gather([packed], [0], name="gather_small_grads")[0]
    rows = packed.shape[0]
    reduced = _rowwise(_f_sum_slots, [(slots, 0, LANES)], [], [(LANES, F32)], tr=_row_tile(rows, LANES, 12), name="sum_small")[0]
    red = _unpack(reduced, small_shapes)
    loss = red[0][0, 0]
    small_grad = {}
    for n, g_ in zip(SMALL, red[1:]):
        if n in SMALL_SHARDED:
            wdt = p[n].shape[-1]
            g_ = lax.dynamic_slice_in_dim(g_, my * wdt, wdt, axis=g_.ndim - 1)
        small_grad[n] = g_
    local_shapes = [p[n].shape for n in SMALL]
    pk = lambda d: _pack([d[n] for n in SMALL])
    pg_, pw_, pm_, pv_ = pk(small_grad), pk(p), pk(m), pk(v)
    srows = pg_.shape[0]
    sd, sm, sv = _rowwise(_f_adam, [(pg_, 0, LANES), (pw_, 0, LANES), (pm_, 0, LANES), (pv_, 0, LANES)], [],
                          [(LANES, F32)] * 3, tr=_row_tile(srows, LANES, 16), name="adam_small")
    out_delta = dict(zip(SMALL, _unpack(sd, local_shapes)))
    out_m = dict(zip(SMALL, _unpack(sm, local_shapes)))
    out_v = dict(zip(SMALL, _unpack(sv, local_shapes)))
    out_grad = dict(small_grad)

    per_layer = {n: [] for n in big_names}
    for l in range(DEPTH):
        for n in big_names:
            r = recv[(n, l)]
            rws, wdt = r.shape[1], r.shape[2]
            tr_w = _row_tile(rws, wdt, 24)
            outs = _rowwise(_f_adam_slots, [(r, 0, wdt), (p[n], 0, wdt, l), (m[n], 0, wdt, l), (v[n], 0, wdt, l)], [],
                            [(wdt, F32)] * 4, tr=tr_w, name="adam_" + n)
            per_layer[n].append(outs)
    for n in big_names:
        for k, dst in enumerate((out_grad, out_delta, out_m, out_v)):
            dst[n] = jnp.stack([per_layer[n][l][k] for l in range(DEPTH)])
    return (loss, grad_x, *[out_grad[n] for n in WEIGHTS], *[out_delta[n] for n in WEIGHTS],
            *[out_m[n] for n in WEIGHTS], *[out_v[n] for n in WEIGHTS])


def _rms_bwd_call(h, g, du, dres, tr, name):
    return _rowwise(lambda a, d, r, gg: _f_rms_bwd(a, gg, d, r), [(h, 0, D_MODEL), (du, 0, D_MODEL), (dres, 0, D_MODEL)], [g],
                    [(D_MODEL, F32), (D_MODEL, BF16)], [g.shape], tr=tr, name=name)


def _xattn_bwd_call(xq, kv, qg, kg, dxo, tr):
    def f(q, d, kvv, qgv, kgv):
        return _vjp_rows(lambda a, b, c, e: _f_xattn(a, b, c, e), 4, 1)(q, kvv, qgv, kgv, d)
    return _rowwise(f, [(xq, 0, D_MODEL), (dxo, 0, D_MODEL)], [kv, qg, kg], [(D_MODEL, BF16)], [kv.shape, qg.shape, kg.shape], tr=tr, name="d_xattn")


def _merge_bwd_call(pg, ya, yb, yc, gb, dmerged, tr):
    def f(a, b, c, e, d, gbv):
        return _vjp_rows(_f_merge, 5, 4)(a, b, c, e, gbv, d)
    return _rowwise(f, [(pg, 0, 3 * D_MODEL), (ya, 0, D_MODEL), (yb, 0, D_MODEL), (yc, 0, D_MODEL), (dmerged, 0, D_MODEL)], [gb],
                    [(3 * D_MODEL, BF16)] + [(D_MODEL, BF16)] * 3, [gb.shape], tr=tr, name="d_merge")


def _ssdgate_bwd_call(y, z, ng, dy, tr):
    def f(a, b, d, g):
        return _vjp_rows(_f_ssdgate, 3, 2)(a, b, g, d)
    return _rowwise(f, [(y, 0, SSD_INNER), (z, 0, SSD_INNER), (dy, 0, SSD_INNER)], [ng], [(SSD_INNER, F32), (SSD_INNER, BF16)], [ng.shape], tr=tr, name="d_ssd_gate")


def _lnsilu_bwd_call(ca, lg, lb, dy, tr):
    def f(a, d, g, b):
        return _vjp_rows(_f_lnsilu, 3, 1)(a, g, b, d)
    return _rowwise(f, [(ca, 0, D_MODEL), (dy, 0, D_MODEL)], [lg, lb], [(D_MODEL, F32)], [lg.shape, lb.shape], tr=tr, name="d_ln_silu")


def kernel(x, mem, rel_table, norm_mix, w_in, gate_bias, conv_dw_w, conv_dw_b, conv_ln_g, conv_ln_b, w_conv_out, ssd_conv_w, ssd_conv_b, ssd_dt_bias, ssd_A_log, ssd_D, ssd_norm_g, w_ssd_out, attn_q_norm, attn_k_norm, attn_sinks, w_attn_out, w_mix_out, norm_xattn, norm_mem, w_xq, w_xkv, xattn_q_norm, xattn_k_norm, w_xo, norm_mlp, w_mlp_up, w_mlp_down, loss_target, m_rel_table, m_norm_mix, m_w_in, m_gate_bias, m_conv_dw_w, m_conv_dw_b, m_conv_ln_g, m_conv_ln_b, m_w_conv_out, m_ssd_conv_w, m_ssd_conv_b, m_ssd_dt_bias, m_ssd_A_log, m_ssd_D, m_ssd_norm_g, m_w_ssd_out, m_attn_q_norm, m_attn_k_norm, m_attn_sinks, m_w_attn_out, m_w_mix_out, m_norm_xattn, m_norm_mem, m_w_xq, m_w_xkv, m_xattn_q_norm, m_xattn_k_norm, m_w_xo, m_norm_mlp, m_w_mlp_up, m_w_mlp_down, v_rel_table, v_norm_mix, v_w_in, v_gate_bias, v_conv_dw_w, v_conv_dw_b, v_conv_ln_g, v_conv_ln_b, v_w_conv_out, v_ssd_conv_w, v_ssd_conv_b, v_ssd_dt_bias, v_ssd_A_log, v_ssd_D, v_ssd_norm_g, v_w_ssd_out, v_attn_q_norm, v_attn_k_norm, v_attn_sinks, v_w_attn_out, v_w_mix_out, v_norm_xattn, v_norm_mem, v_w_xq, v_w_xkv, v_xattn_q_norm, v_xattn_k_norm, v_w_xo, v_norm_mlp, v_w_mlp_up, v_w_mlp_down):
    args = locals()
    p = {n: args[n] for n in WEIGHTS}
    m = {n: args["m_" + n] for n in WEIGHTS}
    v = {n: args["v_" + n] for n in WEIGHTS}
    return _step(p, m, v, x, mem, loss_target)
```

```python
import functools
import math

import numpy as np
import jax
import jax.numpy as jnp
from jax import lax
from jax.experimental import pallas as pl
from jax.experimental.pallas import tpu as pltpu

F32 = jnp.float32
BF16 = jnp.bfloat16
HI = lax.Precision.HIGHEST
MESH = pl.DeviceIdType.MESH

N_DEV = 8
D_MODEL = 1024
DEPTH = 4
MEM_LEN = 256
EPS = 1e-6
NEG_INF = -1e30
CONV_KERNEL = 31
SSD_INNER = 2048
SSD_HEAD_DIM = 64
SSD_HEADS = 32
SSD_GROUPS = 4
SSD_STATE = 128
SSD_CONV = 4
SSD_CHUNK = 128
SSD_XBC = SSD_INNER + 2 * SSD_GROUPS * SSD_STATE
HEADS_PER_GROUP = SSD_HEADS // SSD_GROUPS
ATTN_HEADS = 16
ATTN_KV_HEADS = 4
ATTN_HEAD_DIM = 64
ATTN_BLOCK = 128
ATTN_REP = ATTN_HEADS // ATTN_KV_HEADS
REL_BUCKETS = 32
REL_MAX_DIST = 128
XATTN_HEADS = 4
XATTN_HEAD_DIM = 256
MLP_HIDDEN = 4096
OFF_Z = 2048
OFF_XBC = 4096
OFF_DT = 7168
OFF_Q = 7200
OFF_GATE = 8736
IN_COLS = 11808
LANES = 128
CONV_PAD = 32
MM_VMEM_BUDGET = 20 << 20

ADAM_LR, ADAM_B1, ADAM_B2, ADAM_EPS, ADAM_WD, ADAM_STEP = 0.001, 0.9, 0.999, 1e-08, 0.01, 10

WEIGHTS = ['rel_table', 'norm_mix', 'w_in', 'gate_bias', 'conv_dw_w', 'conv_dw_b', 'conv_ln_g', 'conv_ln_b', 'w_conv_out',
           'ssd_conv_w', 'ssd_conv_b', 'ssd_dt_bias', 'ssd_A_log', 'ssd_D', 'ssd_norm_g', 'w_ssd_out', 'attn_q_norm',
           'attn_k_norm', 'attn_sinks', 'w_attn_out', 'w_mix_out', 'norm_xattn', 'norm_mem', 'w_xq', 'w_xkv', 'xattn_q_norm',
           'xattn_k_norm', 'w_xo', 'norm_mlp', 'w_mlp_up', 'w_mlp_down']
ROW_SHARDED = ['w_conv_out', 'w_ssd_out', 'w_attn_out', 'w_mix_out', 'w_xq', 'w_xo', 'w_mlp_down']
COL_SHARDED = ['w_in', 'w_xkv', 'w_mlp_up']
BIG = ROW_SHARDED + COL_SHARDED
SMALL_SHARDED = ['gate_bias', 'conv_dw_w', 'ssd_conv_w']
SMALL = [n for n in WEIGHTS if n not in BIG]


def _bdot(a, b, dims):
    return lax.dot_general(a.astype(BF16), b.astype(BF16), (dims, ((), ())), preferred_element_type=F32)


def _hdot(a, b, dims):
    return lax.dot_general(a, b, (dims, ((), ())), precision=HI, preferred_element_type=F32)


def _dot3(x, s, dims):
    hi = x.astype(jnp.bfloat16)
    rest = x - hi.astype(F32)
    mid = rest.astype(jnp.bfloat16)
    lo = (rest - mid.astype(F32)).astype(jnp.bfloat16)
    dot = lambda piece: lax.dot_general(piece, s, (dims, ((), ())), preferred_element_type=F32)
    return dot(hi) + dot(mid) + dot(lo)


@jax.custom_vjp
def _spread_dot(x, s):
    return _dot3(x, s, ((1,), (0,)))


def _spread_dot_fwd(x, s):
    return _spread_dot(x, s), s


def _spread_dot_bwd(s, g):
    return _dot3(g, s, ((1,), (1,))), jnp.zeros_like(s)


_spread_dot.defvjp(_spread_dot_fwd, _spread_dot_bwd)


def _one_hot_groups(rows, cols, group, transpose=False):
    r = lax.broadcasted_iota(jnp.int32, (rows, cols), 0)
    c = lax.broadcasted_iota(jnp.int32, (rows, cols), 1)
    shift = int(math.log2(group))
    hit = (lax.shift_right_logical(r, shift) == c) if transpose else (r == lax.shift_right_logical(c, shift))
    return hit.astype(jnp.bfloat16)


def _heads_rms(x, g):
    w = x.shape[-1]
    nh = w // ATTN_HEAD_DIM
    ss = _spread_dot(x * x, _one_hot_groups(w, nh, ATTN_HEAD_DIM, transpose=True))
    scale = lax.rsqrt(ss * (1.0 / ATTN_HEAD_DIM) + EPS)
    return x * _spread_dot(scale, _one_hot_groups(nh, w, ATTN_HEAD_DIM)) * jnp.concatenate([g] * nh, axis=-1)


def _pick(dim, pref):
    if dim <= pref:
        return dim
    t = (pref // LANES) * LANES
    while dim % t:
        t -= LANES
    return t


def _logical(op):
    arr, kind, _ = op
    r, c = arr.shape[-2:]
    return (r, c * N_DEV) if kind == 'col' else (r, c)


def _opspec(op, br, bc, rc):
    arr, kind, layer = op
    lead = () if layer is None else (layer,)
    none = (None,) * len(lead)
    if kind == 'plain':
        return pl.BlockSpec(none + (br, bc), lambda i, j, k: lead + rc(i, j, k))
    per = arr.shape[-1] // bc

    def imap(i, j, k):
        r, c = rc(i, j, k)
        if per == 1:
            return lead + (c, r, 0)
        return lead + (lax.div(c, per), r, lax.rem(c, per))
    return pl.BlockSpec(none + (None, br, bc), imap)


def _mm(a, b, *, mode, name, add=None, out_dtype=F32, out_col=False, epi=None, tm=2048, tn=512, tk=1024):
    def operand(op):
        op = op if isinstance(op, tuple) else (op, 'plain', None)
        return (op[0][op[2]], op[1], None) if isinstance(op[0], list) else op

    a, b = operand(a), operand(b)
    ar, ac = _logical(a)
    br_, bc_ = _logical(b)
    if mode == 'nn':
        m, kd, n = ar, ac, bc_
        assert br_ == kd
    elif mode == 'nt':
        m, kd, n = ar, ac, br_
        assert bc_ == kd
    else:
        m, kd, n = ac, ar, bc_
        assert br_ == kd

    def lim(op, is_col_dim):
        return op[0].shape[-1] if (op[1] == 'col' and is_col_dim) else 1 << 30

    tm = _pick(m, min(tm, lim(a, mode == 'tn')))
    tn = _pick(n, min(tn, lim(b, mode != 'nt'), (n // N_DEV) if out_col else 1 << 30))
    tk = _pick(kd, min(tk, lim(a, mode != 'tn'), lim(b, mode == 'nt')))
    nk = kd // tk

    def vmem_bytes(tm_):
        out_bytes = sum(jnp.dtype(dt).itemsize for dt in (epi[2] if epi is not None else [out_dtype]))
        extra_bytes = sum(e.dtype.itemsize for e in (epi[1] if epi is not None else [])) + (add.dtype.itemsize if add is not None else 0)
        blocks = tm_ * tk * a[0].dtype.itemsize + tk * tn * b[0].dtype.itemsize + tm_ * tn * (out_bytes + extra_bytes)
        return 2 * blocks + tm_ * tn * 4 * (2 if nk > 1 else 1)

    while vmem_bytes(tm) > MM_VMEM_BUDGET and tm % 256 == 0 and tm > 256:
        tm //= 2
    if mode == 'nn':
        a_spec = _opspec(a, tm, tk, lambda i, j, k: (i, k))
        b_spec = _opspec(b, tk, tn, lambda i, j, k: (k, j))
        dims = ((1,), (0,))
    elif mode == 'nt':
        a_spec = _opspec(a, tm, tk, lambda i, j, k: (i, k))
        b_spec = _opspec(b, tn, tk, lambda i, j, k: (j, k))
        dims = ((1,), (1,))
    else:
        a_spec = _opspec(a, tk, tm, lambda i, j, k: (k, i))
        b_spec = _opspec(b, tk, tn, lambda i, j, k: (k, j))
        dims = ((0,), (0,))
    if out_col:
        out_shape = jax.ShapeDtypeStruct((N_DEV, m, n // N_DEV), out_dtype)
        out_spec = _opspec((out_shape, 'col', None), tm, tn, lambda i, j, k: (i, j))
    else:
        out_shape = jax.ShapeDtypeStruct((m, n), out_dtype)
        out_spec = pl.BlockSpec((tm, tn), lambda i, j, k: (i, j))
    has_add = add is not None
    epi_fn, epi_extra, epi_dtypes = epi if epi is not None else (None, [], [out_dtype])
    n_in = 2 + has_add + len(epi_extra)
    n_out = len(epi_dtypes)

    def body(*refs):
        a_ref, b_ref = refs[0], refs[1]
        add_ref = refs[2] if has_add else None
        extra_refs = refs[2 + has_add:n_in]
        o_refs = refs[n_in:n_in + n_out]

        def emit(acc):
            outs = (acc,) if epi_fn is None else epi_fn(acc, *[r[...] for r in extra_refs])
            for o_ref, o in zip(o_refs, outs):
                o_ref[...] = o.astype(o_ref.dtype)

        part = _bdot(a_ref[...], b_ref[...], dims)
        if nk == 1:
            emit(part + add_ref[...].astype(F32) if has_add else part)
            return
        acc_ref = refs[n_in + n_out]
        k = pl.program_id(2)

        @pl.when(k == 0)
        def _():
            acc_ref[...] = part + add_ref[...].astype(F32) if has_add else part

        @pl.when(k > 0)
        def _():
            acc_ref[...] += part

        @pl.when(k == nk - 1)
        def _():
            emit(acc_ref[...])

    block = pl.BlockSpec((tm, tn), lambda i, j, k: (i, j))
    in_specs = [a_spec, b_spec] + [block] * (has_add + len(epi_extra))
    args = [a[0], b[0]] + ([add] if has_add else []) + list(epi_extra)
    if epi is not None:
        assert not out_col
        out_spec = [block] * n_out
        out_shape = [jax.ShapeDtypeStruct((m, n), dt) for dt in epi_dtypes]
    return pl.pallas_call(
        body, name=name, grid=(m // tm, n // tn, nk), in_specs=in_specs, out_specs=out_spec, out_shape=out_shape,
        scratch_shapes=[pltpu.VMEM((tm, tn), F32)] if nk > 1 else [],
        compiler_params=pltpu.CompilerParams(dimension_semantics=("parallel", "parallel", "arbitrary")),
    )(*args)


def _rowwise(f, rows, consts, row_outs, acc_outs=(), *, tr, name):
    nr, nc, nro = len(rows), len(consts), len(row_outs)
    first = rows[0][0]
    t = first.shape[-2]
    assert t % tr == 0
    in_specs = []
    for spec in rows:
        arr, cb, w = spec[:3]
        lead = spec[3] if len(spec) > 3 else None
        if arr.ndim == 2:
            in_specs.append(pl.BlockSpec((tr, w), functools.partial(lambda i, cb: (i, cb), cb=cb)))
        elif lead is not None:
            in_specs.append(pl.BlockSpec((None, tr, w), functools.partial(lambda i, cb, lead: (lead, i, cb), cb=cb, lead=lead)))
        else:
            in_specs.append(pl.BlockSpec((arr.shape[0], tr, w), functools.partial(lambda i, cb: (0, i, cb), cb=cb)))
    for cst in consts:
        in_specs.append(pl.BlockSpec(cst.shape, functools.partial(lambda i, nd: (0,) * nd, nd=cst.ndim)))
    out_specs = [pl.BlockSpec((tr, w), lambda i: (i, 0)) for w, _ in row_outs]
    out_shape = [jax.ShapeDtypeStruct((t, w), dt) for w, dt in row_outs]
    for shp in acc_outs:
        out_specs.append(pl.BlockSpec(shp, functools.partial(lambda i, nd: (0,) * nd, nd=len(shp))))
        out_shape.append(jax.ShapeDtypeStruct(shp, F32))

    def body(*refs):
        ins = [r[...] for r in refs[:nr + nc]]
        ro = refs[nr + nc:nr + nc + nro]
        ao = refs[nr + nc + nro:]
        outs, accs = f(*ins)
        for o_ref, o in zip(ro, outs):
            o_ref[...] = o.astype(o_ref.dtype)
        if ao:
            i = pl.program_id(0)

            @pl.when(i == 0)
            def _():
                for a_ref, acc in zip(ao, accs):
                    a_ref[...] = acc

            @pl.when(i > 0)
            def _():
                for a_ref, acc in zip(ao, accs):
                    a_ref[...] += acc

    res = pl.pallas_call(
        body, name=name, grid=(t // tr,), in_specs=in_specs, out_specs=out_specs, out_shape=out_shape,
        compiler_params=pltpu.CompilerParams(dimension_semantics=("arbitrary",)),
    )(*[s[0] for s in rows], *consts)
    return res


def _vjp_rows(f, n_prim, n_rows_grad):
    def g(*args):
        prim, cots = args[:n_prim], args[n_prim:]
        outs, vjp = jax.vjp(f, *prim)
        grads = vjp(tuple(c.astype(o.dtype) for c, o in zip(cots, outs)))
        return tuple(grads[:n_rows_grad]), tuple(grads[n_rows_grad:])
    return g


def _rms(x, g):
    return x * lax.rsqrt(jnp.mean(x * x, axis=-1, keepdims=True) + EPS) * g


def _f_rms(h, g):
    return (_rms(h, g),)


def _f_rms_bwd(h, g, du, dres):
    _, vjp = jax.vjp(_f_rms, h, g)
    dh, dg = vjp((du.astype(F32),))
    return (dh + dres, dh + dres), (dg,)


def _f_lnsilu(x, g, b):
    mu = jnp.mean(x, axis=-1, keepdims=True)
    xc = x - mu
    y = xc * lax.rsqrt(jnp.mean(xc * xc, axis=-1, keepdims=True) + EPS) * g + b
    return (jax.nn.silu(y),)


def _f_ssdgate(y, z, g):
    y = y * jax.nn.silu(z)
    w = SSD_INNER // SSD_GROUPS
    return (jnp.concatenate([_rms(y[:, i * w:(i + 1) * w], g[:, i * w:(i + 1) * w]) for i in range(SSD_GROUPS)], axis=-1),)


def _f_merge(pg, ya, yb, yc, gb):
    out = 0.0
    for i, yi in enumerate((ya, yb, yc)):
        out = out + jax.nn.sigmoid(pg[:, i * D_MODEL:(i + 1) * D_MODEL] + gb[i:i + 1, :]) * yi
    return (out,)


def _f_relu2(a):
    return (jnp.square(jnp.maximum(a, 0.0)),)


def _f_xattn(q, kv, qg, kg):
    outs = []
    for h in range(XATTN_HEADS):
        sl = slice(h * XATTN_HEAD_DIM, (h + 1) * XATTN_HEAD_DIM)
        qh = _rms(q[:, sl], qg)
        kh = _rms(kv[:, sl], kg)
        vh = kv[:, D_MODEL + h * XATTN_HEAD_DIM:D_MODEL + (h + 1) * XATTN_HEAD_DIM]
        s = _bdot(qh, kh, ((1,), (1,))) * (XATTN_HEAD_DIM ** -0.5)
        p = jnp.exp(s - jnp.max(s, axis=-1, keepdims=True))
        p = p * (1.0 / jnp.sum(p, axis=-1, keepdims=True))
        outs.append(_bdot(p, vh, ((1,), (0,))))
    return (jnp.concatenate(outs, axis=-1),)


def _f_loss(y, tgt):
    err = y - tgt
    per_row = jnp.sum(err * err, axis=-1, keepdims=True) * (0.5 / D_MODEL)
    loss = jnp.sum(per_row, axis=0, keepdims=True)
    dy = err * (1.0 / D_MODEL)
    return (dy, dy), (jnp.broadcast_to(loss, (1, LANES)),)


def _adam_core(w, g, m, v):
    m = ADAM_B1 * m + (1.0 - ADAM_B1) * g
    v = ADAM_B2 * v + (1.0 - ADAM_B2) * jnp.square(g)
    m_hat = m / (1.0 - ADAM_B1 ** ADAM_STEP)
    v_hat = v / (1.0 - ADAM_B2 ** ADAM_STEP)
    delta = -ADAM_LR * (m_hat / (jnp.sqrt(v_hat) + ADAM_EPS) + ADAM_WD * w)
    return delta, m, v


def _sum_slots(g8):
    g = g8[0].astype(F32)
    for s in range(1, N_DEV):
        g = g + g8[s].astype(F32)
    return g


def _f_adam_slots(g8, w, m, v):
    g = _sum_slots(g8)
    return (g,) + _adam_core(w, g, m, v), ()


def _f_sum_slots(g8):
    return (_sum_slots(g8),), ()


def _f_adam(g, w, m, v):
    return _adam_core(w, g, m, v), ()


def _conv_chunk(t):
    return 256 if t % 256 == 0 else t


def _conv_fwd(srcs, w, b, *, kk, pre_glu, post_silu, name):
    t = srcs[0][0].shape[0]
    c = w.shape[-1]
    tt = _conv_chunk(t)
    ns = len(srcs)

    def body(*refs):
        w_ref, b_ref, o_ref, pad_ref = refs[ns:]
        if pre_glu:
            xin = refs[0][...] * jax.nn.sigmoid(refs[1][...])
        else:
            xin = refs[0][...]
        pad_ref[0:CONV_PAD, :] = jnp.zeros((CONV_PAD, LANES), F32)
        pad_ref[CONV_PAD:, :] = xin

        def chunk(i, carry):
            base = pl.multiple_of(i * tt, tt)
            acc = jnp.broadcast_to(b_ref[...], (tt, LANES))
            for j in range(kk):
                acc = acc + pad_ref[pl.ds(base + CONV_PAD - (kk - 1) + j, tt), :] * w_ref[j:j + 1, :]
            o_ref[pl.ds(base, tt), :] = jax.nn.silu(acc) if post_silu else acc
            return carry
        lax.fori_loop(0, t // tt, chunk, 0)

    in_specs = [pl.BlockSpec((t, LANES), functools.partial(lambda i, off: (0, off + i), off=off)) for _, off in srcs]
    in_specs += [pl.BlockSpec((kk, LANES), lambda i: (0, i)), pl.BlockSpec((1, LANES), lambda i: (0, i))]
    return pl.pallas_call(
        body, name=name, grid=(c // LANES,), in_specs=in_specs, out_specs=pl.BlockSpec((t, LANES), lambda i: (0, i)),
        out_shape=jax.ShapeDtypeStruct((t, c), F32), scratch_shapes=[pltpu.VMEM((t + CONV_PAD, LANES), F32)],
        compiler_params=pltpu.CompilerParams(dimension_semantics=("parallel",)),
    )(*[s[0] for s in srcs], w, b)


def _conv_bwd(srcs, w, b, dys, *, kk, pre_glu, post_silu, name, dx_dtype):
    t = srcs[0][0].shape[0]
    c = w.shape[-1]
    tt = _conv_chunk(t)
    ns, nd = len(srcs), len(dys)

    def body(*refs):
        src_refs = refs[:ns]
        dy_refs = refs[ns:ns + nd]
        w_ref, b_ref = refs[ns + nd:ns + nd + 2]
        outs = refs[ns + nd + 2:]
        dx_refs, dw_ref, db_ref = outs[:ns], outs[ns], outs[ns + 1]
        pad_ref, dpad_ref = outs[ns + 2:]
        cb = pl.program_id(0)
        if pre_glu:
            a_in = src_refs[0][...]
            sg = jax.nn.sigmoid(src_refs[1][...])
            xin = a_in * sg
        else:
            xin = src_refs[0][...]
        pad_ref[0:CONV_PAD, :] = jnp.zeros((CONV_PAD, LANES), F32)
        pad_ref[CONV_PAD:, :] = xin
        dpad_ref[t:, :] = jnp.zeros((CONV_PAD, LANES), F32)
        dw_ref[...] = jnp.zeros_like(dw_ref)
        db_ref[...] = jnp.zeros_like(db_ref)

        def load_dy(base):
            dy = dy_refs[0][pl.ds(base, tt), :].astype(F32)
            for (_, first, _n), r in zip(dys[1:], dy_refs[1:]):
                dy = jnp.where(cb >= first, r[pl.ds(base, tt), :].astype(F32), dy)
            return dy

        def chunk1(i, carry):
            base = pl.multiple_of(i * tt, tt)
            dy = load_dy(base)
            if post_silu:
                acc = jnp.broadcast_to(b_ref[...], (tt, LANES))
                for j in range(kk):
                    acc = acc + pad_ref[pl.ds(base + CONV_PAD - (kk - 1) + j, tt), :] * w_ref[j:j + 1, :]
                s = jax.nn.sigmoid(acc)
                dy = dy * (s * (1.0 + acc * (1.0 - s)))
            dpad_ref[pl.ds(base, tt), :] = dy
            db_ref[...] += jnp.sum(dy, axis=0, keepdims=True)
            for j in range(kk):
                dw_ref[j:j + 1, :] += jnp.sum(dy * pad_ref[pl.ds(base + CONV_PAD - (kk - 1) + j, tt), :], axis=0, keepdims=True)
            return carry
        lax.fori_loop(0, t // tt, chunk1, 0)

        def chunk2(i, carry):
            base = pl.multiple_of(i * tt, tt)
            acc = jnp.zeros((tt, LANES), F32)
            for j in range(kk):
                acc = acc + dpad_ref[pl.ds(base + (kk - 1) - j, tt), :] * w_ref[j:j + 1, :]
            if pre_glu:
                a_c = src_refs[0][pl.ds(base, tt), :]
                s_c = jax.nn.sigmoid(src_refs[1][pl.ds(base, tt), :])
                dx_refs[0][pl.ds(base, tt), :] = (acc * s_c).astype(dx_dtype)
                dx_refs[1][pl.ds(base, tt), :] = (acc * a_c * s_c * (1.0 - s_c)).astype(dx_dtype)
            else:
                dx_refs[0][pl.ds(base, tt), :] = acc.astype(dx_dtype)
            return carry
        lax.fori_loop(0, t // tt, chunk2, 0)

    in_specs = [pl.BlockSpec((t, LANES), functools.partial(lambda i, off: (0, off + i), off=off)) for _, off in srcs]
    for _, first, n in dys:
        in_specs.append(pl.BlockSpec((t, LANES), functools.partial(lambda i, first, n: (0, jnp.clip(i - first, 0, n - 1)), first=first, n=n)))
    in_specs += [pl.BlockSpec((kk, LANES), lambda i: (0, i)), pl.BlockSpec((1, LANES), lambda i: (0, i))]
    out_specs = [pl.BlockSpec((t, LANES), lambda i: (0, i)) for _ in srcs]
    out_specs += [pl.BlockSpec((kk, LANES), lambda i: (0, i)), pl.BlockSpec((1, LANES), lambda i: (0, i))]
    out_shape = [jax.ShapeDtypeStruct((t, c), dx_dtype) for _ in srcs]
    out_shape += [jax.ShapeDtypeStruct((kk, c), F32), jax.ShapeDtypeStruct((1, c), F32)]
    return pl.pallas_call(
        body, name=name, grid=(c // LANES,), in_specs=in_specs, out_specs=out_specs, out_shape=out_shape,
        scratch_shapes=[pltpu.VMEM((t + CONV_PAD, LANES), F32), pltpu.VMEM((t + CONV_PAD, LANES), F32)],
        compiler_params=pltpu.CompilerParams(dimension_semantics=("parallel",)),
    )(*[s[0] for s in srcs], *[d[0] for d in dys], w, b)


def _ssd_chunk(xbc, dtfull, s_in, dt_bias, a_log, dskip):
    q = xbc.shape[0]
    gw = SSD_INNER // SSD_GROUPS
    x = xbc[:, :SSD_INNER]
    dt_all = jax.nn.softplus(dtfull + dt_bias)
    da_all = dt_all * (-jnp.exp(a_log))
    row = lax.broadcasted_iota(jnp.int32, (q, q), 0)
    col = lax.broadcasted_iota(jnp.int32, (q, q), 1)
    causal = row >= col
    cs = _hdot(causal.astype(F32), da_all, ((1,), (0,)))
    cs_last = cs[q - 1:q, :]
    per_head = jnp.concatenate([dt_all, jnp.exp(cs), jnp.exp(cs_last - cs), jnp.broadcast_to(dskip, (8, LANES))], axis=0)
    wide = _spread_dot(per_head, _one_hot_groups(LANES, SSD_INNER, SSD_HEAD_DIM))
    xdt = x * wide[0:q]
    from_start, to_end, d_wide = wide[q:2 * q], wide[2 * q:3 * q], wide[3 * q:3 * q + 1]
    xdt_end = xdt * to_end
    chunk_decay = jnp.exp(cs_last)
    decay_rows = jnp.concatenate([jnp.broadcast_to(chunk_decay[:, h:h + 1], (SSD_HEAD_DIM, 1)) for h in range(SSD_HEADS)], axis=0)
    cs_t = jnp.transpose(cs)
    y_off, new_states, y_diag = [], [], []
    for g in range(SSD_GROUPS):
        bm = xbc[:, SSD_INNER + g * SSD_STATE:SSD_INNER + (g + 1) * SSD_STATE]
        cm = xbc[:, SSD_INNER + (SSD_GROUPS + g) * SSD_STATE:SSD_INNER + (SSD_GROUPS + g + 1) * SSD_STATE]
        cb = _bdot(cm, bm, ((1,), (1,)))
        y_off.append(_bdot(cm, s_in[g * gw:(g + 1) * gw, :], ((1,), (1,))))
        new_states.append(_bdot(xdt_end[:, g * gw:(g + 1) * gw], bm, ((0,), (0,))))
        for j in range(HEADS_PER_GROUP):
            h = g * HEADS_PER_GROUP + j
            diff = cs[:, h:h + 1] - cs_t[h:h + 1, :]
            decay = jnp.where(causal, jnp.exp(jnp.where(causal, diff, 0.0)), 0.0)
            y_diag.append(_bdot(cb * decay, xdt[:, h * SSD_HEAD_DIM:(h + 1) * SSD_HEAD_DIM], ((1,), (0,))))
    s_out = s_in * decay_rows + jnp.concatenate(new_states, axis=0)
    y = jnp.concatenate(y_diag, axis=-1) + jnp.concatenate(y_off, axis=-1) * from_start + x * d_wide
    return y, s_out


def _ssd_fwd(xbc, pdt, dt_bias, a_log, dskip, *, name, rider=None):
    t = xbc.shape[0]
    nc = t // SSD_CHUNK
    ride = _Ride(rider, 5, 2, 1)

    def body(*refs):
        (x_ref, dt_ref, tb_ref, al_ref, d_ref), (y_ref, s_ref), (state_ref,) = ride.split(refs)
        c = pl.program_id(0)
        ride.start(refs, c == 0)

        @pl.when(c == 0)
        def _():
            state_ref[...] = jnp.zeros_like(state_ref)
        s_in = state_ref[...]
        s_ref[...] = s_in
        y, s_out = _ssd_chunk(x_ref[...], dt_ref[...], s_in, tb_ref[...], al_ref[...], d_ref[...])
        y_ref[...] = y
        state_ref[...] = s_out
        ride.finish(refs, c == nc - 1)

    vec = pl.BlockSpec((1, LANES), lambda c: (0, 0))
    res = pl.pallas_call(
        body, name=name, grid=(nc,),
        in_specs=[pl.BlockSpec((SSD_CHUNK, SSD_XBC), lambda c: (c, 0)), pl.BlockSpec((SSD_CHUNK, LANES), lambda c: (c, 0)), vec, vec, vec] + ride.in_specs,
        out_specs=[pl.BlockSpec((SSD_CHUNK, SSD_INNER), lambda c: (c, 0)), pl.BlockSpec((None, SSD_INNER, SSD_STATE), lambda c: (c, 0, 0))] + ride.out_specs,
        out_shape=[jax.ShapeDtypeStruct((t, SSD_INNER), F32), jax.ShapeDtypeStruct((nc, SSD_INNER, SSD_STATE), F32)] + ride.out_shape,
        scratch_shapes=[pltpu.VMEM((SSD_INNER, SSD_STATE), F32)] + ride.scratch,
        compiler_params=pltpu.CompilerParams(dimension_semantics=("arbitrary",)),
    )(xbc, pdt, dt_bias, a_log, dskip, *ride.args)
    return res[:2], res[2:]


def _ssd_bwd(xbc, pdt, states, dy, dt_bias, a_log, dskip, *, name, rider=None):
    t = xbc.shape[0]
    nc = t // SSD_CHUNK
    rev = lambda c: nc - 1 - c
    ride = _Ride(rider, 7, 5, 1)

    def body(*refs):
        (x_ref, dt_ref, s_ref, dy_ref, tb_ref, al_ref, d_ref), (dx_ref, ddt_ref, dtb_ref, dal_ref, dd_ref), (dstate_ref,) = ride.split(refs)
        c = pl.program_id(0)
        ride.start(refs, c == 0)

        @pl.when(c == 0)
        def _():
            dstate_ref[...] = jnp.zeros_like(dstate_ref)
            dtb_ref[...] = jnp.zeros_like(dtb_ref)
            dal_ref[...] = jnp.zeros_like(dal_ref)
            dd_ref[...] = jnp.zeros_like(dd_ref)
        _, vjp = jax.vjp(_ssd_chunk, x_ref[...], dt_ref[...], s_ref[...], tb_ref[...], al_ref[...], d_ref[...])
        dx, ddt, ds_in, dtb, dal, dd = vjp((dy_ref[...].astype(F32), dstate_ref[...]))
        dx_ref[...] = dx
        ddt_ref[...] = ddt
        dstate_ref[...] = ds_in
        dtb_ref[...] += dtb
        dal_ref[...] += dal
        dd_ref[...] += dd
        ride.finish(refs, c == nc - 1)

    vec = pl.BlockSpec((1, LANES), lambda c: (0, 0))
    in_specs = [pl.BlockSpec((SSD_CHUNK, SSD_XBC), lambda c: (rev(c), 0)), pl.BlockSpec((SSD_CHUNK, LANES), lambda c: (rev(c), 0)),
                pl.BlockSpec((None, SSD_INNER, SSD_STATE), lambda c: (rev(c), 0, 0)), pl.BlockSpec((SSD_CHUNK, SSD_INNER), lambda c: (rev(c), 0)),
                vec, vec, vec]
    out_specs = [pl.BlockSpec((SSD_CHUNK, SSD_XBC), lambda c: (rev(c), 0)), pl.BlockSpec((SSD_CHUNK, LANES), lambda c: (rev(c), 0)), vec, vec, vec]
    out_shape = [jax.ShapeDtypeStruct((t, SSD_XBC), F32), jax.ShapeDtypeStruct((t, LANES), F32)] + [jax.ShapeDtypeStruct((1, LANES), F32)] * 3
    res = pl.pallas_call(
        body, name=name, grid=(nc,), in_specs=in_specs + ride.in_specs, out_specs=out_specs + ride.out_specs,
        out_shape=out_shape + ride.out_shape, scratch_shapes=[pltpu.VMEM((SSD_INNER, SSD_STATE), F32)] + ride.scratch,
        compiler_params=pltpu.CompilerParams(dimension_semantics=("arbitrary",)),
    )(xbc, pdt, states, dy, dt_bias, a_log, dskip, *ride.args)
    return res[:5], res[5:]


def _rel_buckets():
    qi = np.arange(ATTN_BLOCK)[:, None] + ATTN_BLOCK
    kj = np.arange(2 * ATTN_BLOCK)[None, :]
    dist = qi - kj
    max_exact = REL_BUCKETS // 2
    d = np.maximum(dist, 1).astype(np.float32)
    large = max_exact + (np.log(d / np.float32(max_exact)) / np.float32(math.log(REL_MAX_DIST / max_exact))
                         * np.float32(REL_BUCKETS - max_exact)).astype(np.int32)
    large = np.minimum(large, REL_BUCKETS - 1)
    return np.where(dist < max_exact, np.maximum(dist, 0), large).astype(np.int32)


def _onehot_buckets(bucket_ref):
    n = bucket_ref.shape[-1]
    return (lax.broadcasted_iota(jnp.int32, (REL_BUCKETS, n), 0) == bucket_ref[...]).astype(F32)


def _band_bias(rel_table_t, buckets):
    n = buckets.shape[-1]

    def body(rt_ref, bk_ref, o_ref):
        o_ref[...] = _hdot(rt_ref[...], _onehot_buckets(bk_ref), ((1,), (0,)))
    return pl.pallas_call(body, name="band_bias", out_shape=jax.ShapeDtypeStruct((ATTN_HEADS, n), F32))(rel_table_t, buckets)


def _band_bias_bwd(dbias, buckets):
    def body(db_ref, bk_ref, o_ref):
        d = db_ref[0]
        for layer in range(1, db_ref.shape[0]):
            d = d + db_ref[layer]
        o_ref[...] = _hdot(d, _onehot_buckets(bk_ref), ((1,), (1,)))
    return pl.pallas_call(body, name="band_bias_bwd", out_shape=jax.ShapeDtypeStruct((ATTN_HEADS, REL_BUCKETS), F32))(dbias, buckets)


def _attn_block(q, kvp, kvc, bias, sinks, qg, kg, first):
    qn = q.shape[0]
    rows = ATTN_REP * qn
    ri = lax.broadcasted_iota(jnp.int32, (rows, 2 * qn), 0) & (qn - 1)
    cj = lax.broadcasted_iota(jnp.int32, (rows, 2 * qn), 1)
    jj = cj & (qn - 1)
    no_prev = jnp.where(first, qn, 0)
    mask = ((cj < qn) & (jj > ri + no_prev)) | ((cj >= qn) & (jj <= ri))
    kvd = ATTN_KV_HEADS * ATTN_HEAD_DIM
    q_normed = _heads_rms(q, qg)
    kn_prev, kn_cur = _heads_rms(kvp[:, :kvd], kg), _heads_rms(kvc[:, :kvd], kg)
    outs = []
    for g in range(ATTN_KV_HEADS):
        sl = slice(g * ATTN_HEAD_DIM, (g + 1) * ATTN_HEAD_DIM)
        vsl = slice(kvd + g * ATTN_HEAD_DIM, kvd + (g + 1) * ATTN_HEAD_DIM)
        qs = jnp.concatenate([q_normed[:, (g * ATTN_REP + j) * ATTN_HEAD_DIM:(g * ATTN_REP + j + 1) * ATTN_HEAD_DIM] for j in range(ATTN_REP)], axis=0)
        kb = jnp.concatenate([kn_prev[:, sl], kn_cur[:, sl]], axis=0)
        vb = jnp.concatenate([kvp[:, vsl], kvc[:, vsl]], axis=0)
        logits = _bdot(qs, kb, ((1,), (1,))) * (ATTN_HEAD_DIM ** -0.5) + bias[g]
        logits = jnp.where(mask, logits, NEG_INF)
        sink = jnp.concatenate([jnp.broadcast_to(sinks[:, g * ATTN_REP + j:g * ATTN_REP + j + 1], (qn, 1)) for j in range(ATTN_REP)], axis=0)
        m = jnp.maximum(jnp.max(logits, axis=-1, keepdims=True), sink)
        pexp = jnp.exp(logits - m)
        probs = pexp * (1.0 / (jnp.sum(pexp, axis=-1, keepdims=True) + jnp.exp(sink - m)))
        o = _bdot(probs, vb, ((1,), (0,)))
        outs += [o[j * qn:(j + 1) * qn, :] for j in range(ATTN_REP)]
    return jnp.concatenate(outs, axis=-1)


def _attn_specs(nmap):
    qd, kvw = ATTN_HEADS * ATTN_HEAD_DIM, 2 * ATTN_KV_HEADS * ATTN_HEAD_DIM
    full = lambda shp: pl.BlockSpec(shp, lambda i: (0,) * len(shp))
    return [
        pl.BlockSpec((ATTN_BLOCK, qd), lambda i: (nmap(i), 0)),
        pl.BlockSpec((ATTN_BLOCK, kvw), lambda i: (jnp.maximum(nmap(i) - 1, 0), qd // kvw)),
        pl.BlockSpec((ATTN_BLOCK, kvw), lambda i: (nmap(i), qd // kvw)),
        full((ATTN_KV_HEADS, ATTN_REP * ATTN_BLOCK, 2 * ATTN_BLOCK)), full((1, ATTN_HEADS)), full((1, ATTN_HEAD_DIM)), full((1, ATTN_HEAD_DIM)),
    ]


def _attn_fwd(pqkv, bias, sinks, qg, kg, *, name, rider=None):
    t = pqkv.shape[0]
    nb = t // ATTN_BLOCK
    qd = ATTN_HEADS * ATTN_HEAD_DIM
    ride = _Ride(rider, 7, 1, 0)

    def body(*refs):
        (q_ref, kvp_ref, kvc_ref, bias_ref, sk_ref, qg_ref, kg_ref), (o_ref,), _ = ride.split(refs)
        i = pl.program_id(0)
        ride.start(refs, i == 0)
        o_ref[...] = _attn_block(q_ref[...], kvp_ref[...], kvc_ref[...], bias_ref[...], sk_ref[...], qg_ref[...], kg_ref[...], i == 0).astype(o_ref.dtype)
        ride.finish(refs, i == nb - 1)

    res = pl.pallas_call(
        body, name=name, grid=(nb,), in_specs=_attn_specs(lambda i: i) + ride.in_specs,
        out_specs=[pl.BlockSpec((ATTN_BLOCK, qd), lambda i: (i, 0))] + ride.out_specs,
        out_shape=[jax.ShapeDtypeStruct((t, qd), BF16)] + ride.out_shape, scratch_shapes=ride.scratch,
        compiler_params=pltpu.CompilerParams(dimension_semantics=("arbitrary",)),
    )(pqkv, pqkv, pqkv, bias, sinks, qg, kg, *ride.args)
    return res[0], res[1:]


def _attn_bwd(pqkv, do, bias, sinks, qg, kg, *, name, rider=None):
    t = pqkv.shape[0]
    nb = t // ATTN_BLOCK
    qd, kvw = ATTN_HEADS * ATTN_HEAD_DIM, 2 * ATTN_KV_HEADS * ATTN_HEAD_DIM
    rev = lambda i: nb - 1 - i
    ride = _Ride(rider, 8, 5, 1)

    def body(*refs):
        (q_ref, kvp_ref, kvc_ref, bias_ref, sk_ref, qg_ref, kg_ref, do_ref), (dqkv_ref, dbias_ref, dsk_ref, dqg_ref, dkg_ref), (carry_ref,) = ride.split(refs)
        i = pl.program_id(0)
        ride.start(refs, i == 0)
        first = rev(i) == 0
        f = functools.partial(_attn_block, first=first)
        _, vjp = jax.vjp(f, q_ref[...], kvp_ref[...], kvc_ref[...], bias_ref[...], sk_ref[...], qg_ref[...], kg_ref[...])
        dq, dkvp, dkvc, dbias, dsk, dqg, dkg = vjp(do_ref[...].astype(F32))

        @pl.when(i == 0)
        def _():
            carry_ref[...] = jnp.zeros_like(carry_ref)
            dbias_ref[...] = jnp.zeros_like(dbias_ref)
            dsk_ref[...] = jnp.zeros_like(dsk_ref)
            dqg_ref[...] = jnp.zeros_like(dqg_ref)
            dkg_ref[...] = jnp.zeros_like(dkg_ref)
        dqkv_ref[:, 0:qd] = dq.astype(dqkv_ref.dtype)
        dqkv_ref[:, qd:] = (dkvc + carry_ref[...]).astype(dqkv_ref.dtype)
        carry_ref[...] = dkvp
        dbias_ref[...] += dbias
        dsk_ref[...] += dsk
        dqg_ref[...] += dqg
        dkg_ref[...] += dkg
        ride.finish(refs, i == nb - 1)

    full = lambda shp: pl.BlockSpec(shp, lambda i: (0,) * len(shp))
    bshape = (ATTN_KV_HEADS, ATTN_REP * ATTN_BLOCK, 2 * ATTN_BLOCK)
    res = pl.pallas_call(
        body, name=name, grid=(nb,), in_specs=_attn_specs(rev) + [pl.BlockSpec((ATTN_BLOCK, qd), lambda i: (rev(i), 0))] + ride.in_specs,
        out_specs=[pl.BlockSpec((ATTN_BLOCK, qd + kvw), lambda i: (rev(i), 0)), full(bshape), full((1, ATTN_HEADS)),
                   full((1, ATTN_HEAD_DIM)), full((1, ATTN_HEAD_DIM))] + ride.out_specs,
        out_shape=[jax.ShapeDtypeStruct((t, qd + kvw), BF16), jax.ShapeDtypeStruct(bshape, F32), jax.ShapeDtypeStruct((1, ATTN_HEADS), F32),
                   jax.ShapeDtypeStruct((1, ATTN_HEAD_DIM), F32), jax.ShapeDtypeStruct((1, ATTN_HEAD_DIM), F32)] + ride.out_shape,
        scratch_shapes=[pltpu.VMEM((ATTN_BLOCK, kvw), F32)] + ride.scratch,
        compiler_params=pltpu.CompilerParams(dimension_semantics=("arbitrary",)),
    )(pqkv, pqkv, pqkv, bias, sinks, qg, kg, do, *ride.args)
    return res[:5], res[5:]


def _dev_index(dev):
    return 4 * dev[0] + 2 * dev[1] + dev[2]


def _gather_ops(x_refs, o_refs, send_sems, recv_sems, local_sems, axes):
    n = len(x_refs)
    x, y, c = lax.axis_index("x"), lax.axis_index("y"), lax.axis_index("c")
    me, sibling = (x, y, c), (x, y, 1 - c)
    chips = [(1 - x, y), (x, 1 - y), (1 - x, 1 - y)]

    def slot(i, dev):
        idx = _dev_index(dev)
        return o_refs[i].at[idx] if axes[i] == 0 else o_refs[i].at[:, idx]

    def copy(i, k, block, to, src=None):
        return pltpu.make_async_remote_copy(
            src_ref=slot(i, block) if src is None else src, dst_ref=slot(i, block),
            send_sem=send_sems.at[i, k], recv_sem=recv_sems.at[i, k], device_id=to, device_id_type=MESH)

    mine = [pltpu.make_async_copy(x_refs[i], slot(i, me), local_sems.at[i]) for i in range(n)]
    first = []
    for i in range(n):
        first.append(copy(i, 0, me, sibling, src=x_refs[i]))
        first += [copy(i, 1 + j, me, (*chip, c), src=x_refs[i]) for j, chip in enumerate(chips)]

    def start():
        for cp in mine + first:
            cp.start()

    def finish():
        passed = []
        for j, chip in enumerate(chips):
            for i in range(n):
                copy(i, 1 + j, (*chip, c), me).wait_recv()
                cp = copy(i, 4 + j, (*chip, c), sibling)
                cp.start()
                passed.append(cp)
        for i in range(n):
            copy(i, 0, sibling, me).wait_recv()
        for j, chip in enumerate(chips):
            for i in range(n):
                copy(i, 4 + j, (*chip, 1 - c), me).wait_recv()
        for cp in first + passed:
            cp.wait_send()
        for cp in mine:
            cp.wait()

    return start, finish


def _exchange_ops(g_refs, o_refs, send_sems, recv_sems, local_sems):
    n = len(g_refs)
    x, y, c = lax.axis_index("x"), lax.axis_index("y"), lax.axis_index("c")
    me = _dev_index((x, y, c))
    peers = [(x ^ ((k >> 2) & 1), y ^ ((k >> 1) & 1), c ^ (k & 1)) for k in range(1, N_DEV)]

    def copy(i, k):
        peer = peers[k]
        return pltpu.make_async_remote_copy(
            src_ref=g_refs[i].at[_dev_index(peer)], dst_ref=o_refs[i].at[me],
            send_sem=send_sems.at[i, k], recv_sem=recv_sems.at[i, k], device_id=peer, device_id_type=MESH)

    def arrival(i, k):
        peer = peers[k]
        return pltpu.make_async_remote_copy(
            src_ref=g_refs[i].at[me], dst_ref=o_refs[i].at[_dev_index(peer)],
            send_sem=send_sems.at[i, k], recv_sem=recv_sems.at[i, k], device_id=peer, device_id_type=MESH)

    mine = [pltpu.make_async_copy(g_refs[i].at[me], o_refs[i].at[me], local_sems.at[i]) for i in range(n)]
    sends = [copy(i, k) for i in range(n) for k in range(N_DEV - 1)]

    def start():
        for cp in mine + sends:
            cp.start()

    def finish():
        for i in range(n):
            for k in range(N_DEV - 1):
                arrival(i, k).wait_recv()
        for cp in sends:
            cp.wait_send()
        for cp in mine:
            cp.wait()

    return start, finish


class _Ride:
    def __init__(self, rider, n_in, n_out, n_scr):
        self.rider, self.n_in, self.n_out, self.n_scr = rider, n_in, n_out, n_scr
        self.args = [] if rider is None else list(rider[1])
        n = self.n = len(self.args)
        hbm = pl.BlockSpec(memory_space=pltpu.HBM)
        self.in_specs, self.out_specs = [hbm] * n, [hbm] * n
        if rider is None:
            self.out_shape, self.scratch = [], []
            return
        if rider[0] == 'gather':
            self.out_shape = [jax.ShapeDtypeStruct(a.shape[:ax] + (N_DEV,) + a.shape[ax:], a.dtype) for a, ax in zip(self.args, rider[2])]
        else:
            self.out_shape = [jax.ShapeDtypeStruct(a.shape, a.dtype) for a in self.args]
        self.scratch = [pltpu.SemaphoreType.DMA((n, 7)), pltpu.SemaphoreType.DMA((n, 7)), pltpu.SemaphoreType.DMA((n,))]

    def split(self, refs):
        a = self.n_in
        c = a + self.n + self.n_out
        e = c + self.n
        return refs[:a], refs[a + self.n:c], refs[e:e + self.n_scr]

    def _ops(self, refs):
        a = self.n_in
        c = a + self.n + self.n_out
        e = c + self.n + self.n_scr
        x_refs, o_refs, sems = refs[a:a + self.n], refs[c:c + self.n], refs[e:e + 3]
        if self.rider[0] == 'gather':
            return _gather_ops(x_refs, o_refs, *sems, self.rider[2])
        return _exchange_ops(x_refs, o_refs, *sems)

    def start(self, refs, cond):
        if self.rider is not None:
            pl.when(cond)(lambda: self._ops(refs)[0]())

    def finish(self, refs, cond):
        if self.rider is not None:
            pl.when(cond)(lambda: self._ops(refs)[1]())


def _comm_call(rider, *, name):
    ride = _Ride(rider, 0, 0, 0)

    def body(*refs):
        start, finish = ride._ops(refs)
        start()
        finish()

    return pl.pallas_call(body, name=name, in_specs=ride.in_specs, out_specs=ride.out_specs, out_shape=ride.out_shape,
                          scratch_shapes=ride.scratch)(*ride.args)


def _all_gather(xs, axes, *, name):
    return _comm_call(('gather', xs, axes), name=name)


def _exchange(gs, *, name):
    return _comm_call(('exchange', gs), name=name)


def _pack(arrays):
    parts = []
    for a in arrays:
        flat = a.reshape(-1)
        pad = (-flat.shape[0]) % LANES
        if pad:
            flat = jnp.concatenate([flat, jnp.zeros((pad,), flat.dtype)])
        parts.append(flat.reshape(-1, LANES))
    return jnp.concatenate(parts, axis=0)


def _unpack(buf, shapes):
    out, row = [], 0
    for shp in shapes:
        size = int(np.prod(shp))
        rows = -(-size // LANES)
        out.append(buf[row:row + rows].reshape(-1)[:size].reshape(shp))
        row += rows
    return out


def _row_tile(rows, width, n_bufs):
    padded = -(-width // LANES) * LANES
    cap = max(16, (12 << 20) // (padded * 4 * n_bufs))
    if rows <= cap:
        return rows
    tr = (cap // 16) * 16
    while tr > 16 and rows % tr:
        tr -= 16
    return tr if rows % tr == 0 else rows


def _step(p, m, v, x, mem, loss_target):
    t = x.shape[1]
    h0 = x.reshape(t, D_MODEL)
    mem2 = mem.reshape(MEM_LEN, D_MODEL)
    tgt = loss_target.reshape(t, D_MODEL)
    tr = 256 if t % 256 == 0 else t
    my = _dev_index((lax.axis_index("x"), lax.axis_index("y"), lax.axis_index("c")))

    small_sh_shapes = [p[n].shape for n in SMALL_SHARDED]
    gathered_small = _all_gather([_pack([p[n] for n in SMALL_SHARDED])], [0], name="gather_small")[0]
    full = {}
    for n, a in zip(SMALL_SHARDED, zip(*[_unpack(gathered_small[d], small_sh_shapes) for d in range(N_DEV)])):
        full[n] = jnp.concatenate(a, axis=-1)
    big_names = ROW_SHARDED + COL_SHARDED
    seg_order = ['a', 'ag', 'z', 'x', 'dt', 'qkv', 'g']
    wg = {n: [] for n in big_names}
    w_seg = {k: [] for k in seg_order}
    with_ssd = ['w_in']
    with_swa = [n for n in big_names if n != 'w_in']

    def shards(layer, names):
        return [p[n][layer].astype(BF16) for n in names]

    def use_weights(got):
        for n in ROW_SHARDED:
            a = got[n]
            wg[n].append(a.reshape(a.shape[0] * a.shape[1], a.shape[2]))
        for n in ('w_xkv', 'w_mlp_up'):
            wg[n].append(got[n])
        w_in_full = jnp.transpose(got['w_in'], (1, 0, 2)).reshape(D_MODEL, IN_COLS)
        segs = {'a': w_in_full[:, 0:D_MODEL], 'ag': w_in_full[:, D_MODEL:OFF_Z], 'z': w_in_full[:, OFF_Z:OFF_XBC], 'x': w_in_full[:, OFF_XBC:OFF_DT],
                'dt': jnp.pad(w_in_full[:, OFF_DT:OFF_Q], ((0, 0), (0, LANES - SSD_HEADS))),
                'qkv': w_in_full[:, OFF_Q:OFF_GATE], 'g': w_in_full[:, OFF_GATE:IN_COLS]}
        for k in seg_order:
            w_seg[k].append(segs[k])

    use_weights(dict(zip(big_names, _all_gather(shards(0, big_names), [0] * len(big_names), name="gather_weights"))))

    def vec(name, layer, width=None):
        a = p[name][layer].reshape(1, -1)
        if width is not None and a.shape[1] < width:
            a = jnp.pad(a, ((0, 0), (0, width - a.shape[1])))
        return a

    buckets = jnp.asarray(_rel_buckets().reshape(1, -1))
    bias = _band_bias(jnp.transpose(p['rel_table']), buckets).reshape(ATTN_KV_HEADS, ATTN_REP * ATTN_BLOCK, 2 * ATTN_BLOCK)

    saved = []
    h = h0
    for l in range(DEPTH):
        s = {'h0': h}
        u = _rowwise(lambda a, g: (_f_rms(a, g), ()), [(h, 0, D_MODEL)], [vec('norm_mix', l)], [(D_MODEL, BF16)], tr=tr, name="rms_mix")[0]
        s['u'] = u
        pr = {k: _mm(u, (w_seg[k], 'plain', l), mode='nn', name="proj_" + k) for k in seg_order}
        s['pr'] = pr
        dw_w, dw_b = full['conv_dw_w'][l], vec('conv_dw_b', l)
        ca = _conv_fwd([(pr['a'], 0), (pr['ag'], 0)], dw_w, dw_b, kk=CONV_KERNEL, pre_glu=True, post_silu=False, name="conv31")
        s['ca'] = ca
        ya_in = _rowwise(lambda a, g, b: (_f_lnsilu(a, g, b), ()), [(ca, 0, D_MODEL)], [vec('conv_ln_g', l), vec('conv_ln_b', l)],
                         [(D_MODEL, BF16)], tr=tr, name="ln_silu")[0]
        s['ya_in'] = ya_in
        y_a = _mm(ya_in, (wg['w_conv_out'], 'plain', l), mode='nn', name="conv_out")
        xbc = _conv_fwd([(pr['x'], 0)], full['ssd_conv_w'][l], vec('ssd_conv_b', l), kk=SSD_CONV, pre_glu=False, post_silu=True, name="conv4")
        s['xbc'] = xbc
        ssd_vecs = [vec('ssd_dt_bias', l, LANES), vec('ssd_A_log', l, LANES), vec('ssd_D', l, LANES)]
        nxt = {}
        more = l + 1 < DEPTH
        (y_ssd, states), got = _ssd_fwd(xbc, pr['dt'], *ssd_vecs, name="ssd",
                                        rider=('gather', shards(l + 1, with_ssd), [0] * len(with_ssd)) if more else None)
        nxt.update(zip(with_ssd, got))
        s['y_ssd'], s['states'] = y_ssd, states
        yb_in = _rowwise(lambda a, z, g: (_f_ssdgate(a, z, g), ()), [(y_ssd, 0, SSD_INNER), (pr['z'], 0, SSD_INNER)], [vec('ssd_norm_g', l)],
                         [(SSD_INNER, BF16)], tr=tr, name="ssd_gate")[0]
        s['yb_in'] = yb_in
        y_b = _mm(yb_in, (wg['w_ssd_out'], 'plain', l), mode='nn', name="ssd_out")
        att, got = _attn_fwd(pr['qkv'], bias, vec('attn_sinks', l), vec('attn_q_norm', l), vec('attn_k_norm', l), name="swa",
                             rider=('gather', shards(l + 1, with_swa), [0] * len(with_swa)) if more else None)
        nxt.update(zip(with_swa, got))
        if more:
            use_weights(nxt)
        s['att'] = att
        y_c = _mm(att, (wg['w_attn_out'], 'plain', l), mode='nn', name="attn_out")
        s['y_a'], s['y_b'], s['y_c'] = y_a, y_b, y_c
        merged = _rowwise(lambda pg, a, b, c, gb: (_f_merge(pg, a, b, c, gb), ()),
                          [(pr['g'], 0, 3 * D_MODEL), (y_a, 0, D_MODEL), (y_b, 0, D_MODEL), (y_c, 0, D_MODEL)], [full['gate_bias'][l]],
                          [(D_MODEL, BF16)], tr=tr, name="merge")[0]
        s['merged'] = merged
        h = _mm(merged, (wg['w_mix_out'], 'plain', l), mode='nn', add=h, name="mix_out")
        s['h1'] = h
        un = _rowwise(lambda a, g: (_f_rms(a, g), ()), [(h, 0, D_MODEL)], [vec('norm_xattn', l)], [(D_MODEL, BF16)], tr=tr, name="rms_xattn")[0]
        memn = _rowwise(lambda a, g: (_f_rms(a, g), ()), [(mem2, 0, D_MODEL)], [vec('norm_mem', l)], [(D_MODEL, BF16)], tr=MEM_LEN, name="rms_mem")[0]
        s['un'], s['memn'] = un, memn
        xq = _mm(un, (wg['w_xq'], 'plain', l), mode='nn', name="xq")
        kv = _mm(memn, (wg['w_xkv'], 'col', l), mode='nn', name="xkv", tn=256)
        s['xq'], s['kv'] = xq, kv
        xo = _rowwise(lambda q, kvv, qg, kg: (_f_xattn(q, kvv, qg, kg), ()), [(xq, 0, D_MODEL)], [kv, vec('xattn_q_norm', l), vec('xattn_k_norm', l)],
                      [(D_MODEL, BF16)], tr=tr, name="xattn")[0]
        s['xo'] = xo
        h = _mm(xo, (wg['w_xo'], 'plain', l), mode='nn', add=h, name="xattn_out")
        s['h2'] = h
        um = _rowwise(lambda a, g: (_f_rms(a, g), ()), [(h, 0, D_MODEL)], [vec('norm_mlp', l)], [(D_MODEL, BF16)], tr=tr, name="rms_mlp")[0]
        s['um'] = um
        up, act = _mm(um, (wg['w_mlp_up'], 'col', l), mode='nn', name="mlp_up", epi=(lambda acc: (acc,) + _f_relu2(acc), [], [F32, BF16]))
        s['up'], s['act'] = up, act
        h = _mm(act, (wg['w_mlp_down'], 'plain', l), mode='nn', add=h, name="mlp_down")
        saved.append(s)

    dh, dh_b, loss_part = _rowwise(_f_loss, [(h, 0, D_MODEL), (tgt, 0, D_MODEL)], [], [(D_MODEL, F32), (D_MODEL, BF16)], [(1, LANES)], tr=tr, name="loss")

    sg = {n: [None] * DEPTH for n in SMALL if n != 'rel_table'}
    dbias_layers = [None] * DEPTH
    recv = {}
    with_d_swa = ['w_mlp_down', 'w_mlp_up', 'w_xo', 'w_xq', 'w_xkv', 'w_mix_out', 'w_attn_out']
    left_over = []

    def by_device(g_):
        return g_.reshape(N_DEV, g_.shape[0] // N_DEV, g_.shape[1])

    for l in reversed(range(DEPTH)):
        s = saved[l]
        bg = {}
        dup = _mm(dh_b, (wg['w_mlp_down'], 'plain', l), mode='nt', name="d_act",
                  epi=(lambda acc, up_: _vjp_rows(_f_relu2, 1, 1)(up_, acc)[0], [s['up']], [BF16]))[0]
        bg['w_mlp_down'] = _mm(s['act'], dh_b, mode='tn', out_dtype=BF16, name="dw_mlp_down")
        bg['w_mlp_up'] = _mm(s['um'], dup, mode='tn', out_dtype=BF16, out_col=True, name="dw_mlp_up")
        dum = _mm(dup, (wg['w_mlp_up'], 'col', l), mode='nt', name="d_um", tk=512)
        dh, dh_b, dg = _rms_bwd_call(s['h2'], vec('norm_mlp', l), dum, dh, tr, "d_rms_mlp")
        sg['norm_mlp'][l] = dg
        dxo = _mm(dh_b, (wg['w_xo'], 'plain', l), mode='nt', out_dtype=BF16, name="d_xo")
        bg['w_xo'] = _mm(s['xo'], dh_b, mode='tn', out_dtype=BF16, name="dw_xo")
        qg, kg = vec('xattn_q_norm', l), vec('xattn_k_norm', l)
        dxq, dkv, dqg, dkg = _xattn_bwd_call(s['xq'], s['kv'], qg, kg, dxo, tr)
        sg['xattn_q_norm'][l], sg['xattn_k_norm'][l] = dqg, dkg
        bg['w_xq'] = _mm(s['un'], dxq, mode='tn', out_dtype=BF16, name="dw_xq")
        dun = _mm(dxq, (wg['w_xq'], 'plain', l), mode='nt', name="d_un")
        dh, dh_b, dg = _rms_bwd_call(s['h1'], vec('norm_xattn', l), dun, dh, tr, "d_rms_xattn")
        sg['norm_xattn'][l] = dg
        bg['w_xkv'] = _mm(s['memn'], dkv, mode='tn', out_dtype=BF16, out_col=True, name="dw_xkv", tn=256)
        dmemn = _mm(dkv, (wg['w_xkv'], 'col', l), mode='nt', name="d_memn", tk=256)
        _, _, dg = _rms_bwd_call(mem2, vec('norm_mem', l), dmemn, jnp.zeros_like(mem2), MEM_LEN, "d_rms_mem")
        sg['norm_mem'][l] = dg
        dmerged = _mm(dh_b, (wg['w_mix_out'], 'plain', l), mode='nt', out_dtype=BF16, name="d_merged")
        bg['w_mix_out'] = _mm(s['merged'], dh_b, mode='tn', out_dtype=BF16, name="dw_mix_out")
        pr = s['pr']
        gb = full['gate_bias'][l]
        dpg, dya, dyb, dyc, dgb = _merge_bwd_call(pr['g'], s['y_a'], s['y_b'], s['y_c'], gb, dmerged, tr)
        sg['gate_bias'][l] = dgb
        dseg = {'g': dpg}
        datt = _mm(dyc, (wg['w_attn_out'], 'plain', l), mode='nt', out_dtype=BF16, name="d_att")
        bg['w_attn_out'] = _mm(s['att'], dyc, mode='tn', out_dtype=BF16, name="dw_attn_out")
        early = [bg[n] if n in COL_SHARDED else by_device(bg[n]) for n in with_d_swa]
        (dqkv, dbias_l, dsk, dqn, dkn), got = _attn_bwd(pr['qkv'], datt, bias, vec('attn_sinks', l), vec('attn_q_norm', l), vec('attn_k_norm', l),
                                                     name="d_swa", rider=('exchange', early))
        recv.update({(n, l): r for n, r in zip(with_d_swa, got)})
        dseg['qkv'] = dqkv
        dbias_layers[l] = dbias_l
        sg['attn_sinks'][l], sg['attn_q_norm'][l], sg['attn_k_norm'][l] = dsk, dqn, dkn
        dyb_in = _mm(dyb, (wg['w_ssd_out'], 'plain', l), mode='nt', out_dtype=BF16, name="d_yb_in")
        bg['w_ssd_out'] = _mm(s['yb_in'], dyb, mode='tn', out_dtype=BF16, name="dw_ssd_out")
        ng = vec('ssd_norm_g', l)
        dy_ssd, dz, dng = _ssdgate_bwd_call(s['y_ssd'], pr['z'], ng, dyb_in, tr)
        sg['ssd_norm_g'][l] = dng
        dseg['z'] = dz
        ssd_vecs = [vec('ssd_dt_bias', l, LANES), vec('ssd_A_log', l, LANES), vec('ssd_D', l, LANES)]
        late = [('w_ssd_out', l, by_device(bg['w_ssd_out']))] + left_over
        (dxbc_act, ddt, dtb, dal, ddsk), got = _ssd_bwd(s['xbc'], pr['dt'], s['states'], dy_ssd, *ssd_vecs, name="d_ssd",
                                                        rider=('exchange', [a for _, _, a in late]))
        recv.update({(n, ll): r for (n, ll, _), r in zip(late, got)})
        dseg['dt'] = ddt
        sg['ssd_dt_bias'][l], sg['ssd_A_log'][l], sg['ssd_D'][l] = dtb[:, :SSD_HEADS], dal[:, :SSD_HEADS], ddsk[:, :SSD_HEADS]
        dxbc, dcw, dcb = _conv_bwd([(pr['x'], 0)], full['ssd_conv_w'][l], vec('ssd_conv_b', l), [(dxbc_act, 0, SSD_XBC // LANES)],
                                   kk=SSD_CONV, pre_glu=False, post_silu=True, name="d_conv4", dx_dtype=BF16)
        dseg['x'] = dxbc
        sg['ssd_conv_w'][l], sg['ssd_conv_b'][l] = dcw, dcb
        dya_in = _mm(dya, (wg['w_conv_out'], 'plain', l), mode='nt', out_dtype=BF16, name="d_ya_in")
        bg['w_conv_out'] = _mm(s['ya_in'], dya, mode='tn', out_dtype=BF16, name="dw_conv_out")
        lg, lb = vec('conv_ln_g', l), vec('conv_ln_b', l)
        dca, dlg, dlb = _lnsilu_bwd_call(s['ca'], lg, lb, dya_in, tr)
        sg['conv_ln_g'][l], sg['conv_ln_b'][l] = dlg, dlb
        dseg['a'], dseg['ag'], dww, dwb = _conv_bwd([(pr['a'], 0), (pr['ag'], 0)], full['conv_dw_w'][l], vec('conv_dw_b', l),
                                                    [(dca, 0, D_MODEL // LANES)], kk=CONV_KERNEL, pre_glu=True, post_silu=False, name="d_conv31", dx_dtype=BF16)
        sg['conv_dw_w'][l], sg['conv_dw_b'][l] = dww, dwb
        du = None
        dw_parts = []
        for k in seg_order:
            du = _mm(dseg[k], (w_seg[k], 'plain', l), mode='nt', add=du, name="d_u_" + k)
            dw_k = _mm(s['u'], dseg[k], mode='tn', out_dtype=BF16, name="dw_in_" + k)
            dw_parts.append(dw_k[:, :SSD_HEADS] if k == 'dt' else dw_k)
        dw_in = jnp.concatenate(dw_parts, axis=-1)
        dw_in = jnp.transpose(dw_in.reshape(D_MODEL, N_DEV, IN_COLS // N_DEV), (1, 0, 2))
        dh, dh_b, dg = _rms_bwd_call(s['h0'], vec('norm_mix', l), du, dh, tr, "d_rms_mix")
        sg['norm_mix'][l] = dg
        left_over = [('w_conv_out', l, by_device(bg['w_conv_out'])), ('w_in', l, dw_in)]

    got = _exchange([a for _, _, a in left_over], name="exchange_grads")
    recv.update({(n, ll): r for (n, ll, _), r in zip(left_over, got)})
    grad_x = dh.reshape(x.shape)
    d_rel = jnp.transpose(_band_bias_bwd(jnp.stack(dbias_layers).reshape(DEPTH, ATTN_HEADS, -1), buckets))

    small_full = {'rel_table': d_rel}
    for n in SMALL:
        if n != 'rel_table':
            small_full[n] = jnp.stack(sg[n]).reshape((DEPTH,) + (full[n].shape[1:] if n in SMALL_SHARDED else p[n].shape[1:]))
    small_shapes = [(1, LANES)] + [small_full[n].shape for n in SMALL]
    packed = _pack([loss_part] + [small_full[n] for n in SMALL])
    slots = _all_gather([packed], [0], name="gather_small_grads")[0]
    rows = packed.shape[0]
    reduced = _rowwise(_f_sum_slots, [(slots, 0, LANES)], [], [(LANES, F32)], tr=_row_tile(rows, LANES, 12), name="sum_small")[0]
    red = _unpack(reduced, small_shapes)
    loss = red[0][0, 0]
    small_grad = {}
    for n, g_ in zip(SMALL, red[1:]):
        if n in SMALL_SHARDED:
            wdt = p[n].shape[-1]
            g_ = lax.dynamic_slice_in_dim(g_, my * wdt, wdt, axis=g_.ndim - 1)
        small_grad[n] = g_
    local_shapes = [p[n].shape for n in SMALL]
    pk = lambda d: _pack([d[n] for n in SMALL])
    pg_, pw_, pm_, pv_ = pk(small_grad), pk(p), pk(m), pk(v)
    srows = pg_.shape[0]
    sd, sm, sv = _rowwise(_f_adam, [(pg_, 0, LANES), (pw_, 0, LANES), (pm_, 0, LANES), (pv_, 0, LANES)], [],
                          [(LANES, F32)] * 3, tr=_row_tile(srows, LANES, 16), name="adam_small")
    out_delta = dict(zip(SMALL, _unpack(sd, local_shapes)))
    out_m = dict(zip(SMALL, _unpack(sm, local_shapes)))
    out_v = dict(zip(SMALL, _unpack(sv, local_shapes)))
    out_grad = dict(small_grad)

    per_layer = {n: [] for n in big_names}
    for l in range(DEPTH):
        for n in big_names:
            r = recv[(n, l)]
            rws, wdt = r.shape[1], r.shape[2]
            tr_w = _row_tile(rws, wdt, 24)
            outs = _rowwise(_f_adam_slots, [(r, 0, wdt), (p[n], 0, wdt, l), (m[n], 0, wdt, l), (v[n], 0, wdt, l)], [],
                            [(wdt, F32)] * 4, tr=tr_w, name="adam_" + n)
            per_layer[n].append(outs)
    for n in big_names:
        for k, dst in enumerate((out_grad, out_delta, out_m, out_v)):
            dst[n] = jnp.stack([per_layer[n][l][k] for l in range(DEPTH)])
    return (loss, grad_x, *[out_grad[n] for n in WEIGHTS], *[out_delta[n] for n in WEIGHTS],
            *[out_m[n] for n in WEIGHTS], *[out_v[n] for n in WEIGHTS])


def _rms_bwd_call(h, g, du, dres, tr, name):
    return _rowwise(lambda a, d, r, gg: _f_rms_bwd(a, gg, d, r), [(h, 0, D_MODEL), (du, 0, D_MODEL), (dres, 0, D_MODEL)], [g],
                    [(D_MODEL, F32), (D_MODEL, BF16)], [g.shape], tr=tr, name=name)


def _xattn_bwd_call(xq, kv, qg, kg, dxo, tr):
    def f(q, d, kvv, qgv, kgv):
        return _vjp_rows(lambda a, b, c, e: _f_xattn(a, b, c, e), 4, 1)(q, kvv, qgv, kgv, d)
    return _rowwise(f, [(xq, 0, D_MODEL), (dxo, 0, D_MODEL)], [kv, qg, kg], [(D_MODEL, BF16)], [kv.shape, qg.shape, kg.shape], tr=tr, name="d_xattn")


def _merge_bwd_call(pg, ya, yb, yc, gb, dmerged, tr):
    def f(a, b, c, e, d, gbv):
        return _vjp_rows(_f_merge, 5, 4)(a, b, c, e, gbv, d)
    return _rowwise(f, [(pg, 0, 3 * D_MODEL), (ya, 0, D_MODEL), (yb, 0, D_MODEL), (yc, 0, D_MODEL), (dmerged, 0, D_MODEL)], [gb],
                    [(3 * D_MODEL, BF16)] + [(D_MODEL, BF16)] * 3, [gb.shape], tr=tr, name="d_merge")


def _ssdgate_bwd_call(y, z, ng, dy, tr):
    def f(a, b, d, g):
        return _vjp_rows(_f_ssdgate, 3, 2)(a, b, g, d)
    return _rowwise(f, [(y, 0, SSD_INNER), (z, 0, SSD_INNER), (dy, 0, SSD_INNER)], [ng], [(SSD_INNER, F32), (SSD_INNER, BF16)], [ng.shape], tr=tr, name="d_ssd_gate")


def _lnsilu_bwd_call(ca, lg, lb, dy, tr):
    def f(a, d, g, b):
        return _vjp_rows(_f_lnsilu, 3, 1)(a, g, b, d)
    return _rowwise(f, [(ca, 0, D_MODEL), (dy, 0, D_MODEL)], [lg, lb], [(D_MODEL, F32)], [lg.shape, lb.shape], tr=tr, name="d_ln_silu")


def kernel(x, mem, rel_table, norm_mix, w_in, gate_bias, conv_dw_w, conv_dw_b, conv_ln_g, conv_ln_b, w_conv_out, ssd_conv_w, ssd_conv_b, ssd_dt_bias, ssd_A_log, ssd_D, ssd_norm_g, w_ssd_out, attn_q_norm, attn_k_norm, attn_sinks, w_attn_out, w_mix_out, norm_xattn, norm_mem, w_xq, w_xkv, xattn_q_norm, xattn_k_norm, w_xo, norm_mlp, w_mlp_up, w_mlp_down, loss_target, m_rel_table, m_norm_mix, m_w_in, m_gate_bias, m_conv_dw_w, m_conv_dw_b, m_conv_ln_g, m_conv_ln_b, m_w_conv_out, m_ssd_conv_w, m_ssd_conv_b, m_ssd_dt_bias, m_ssd_A_log, m_ssd_D, m_ssd_norm_g, m_w_ssd_out, m_attn_q_norm, m_attn_k_norm, m_attn_sinks, m_w_attn_out, m_w_mix_out, m_norm_xattn, m_norm_mem, m_w_xq, m_w_xkv, m_xattn_q_norm, m_xattn_k_norm, m_w_xo, m_norm_mlp, m_w_mlp_up, m_w_mlp_down, v_rel_table, v_norm_mix, v_w_in, v_gate_bias, v_conv_dw_w, v_conv_dw_b, v_conv_ln_g, v_conv_ln_b, v_w_conv_out, v_ssd_conv_w, v_ssd_conv_b, v_ssd_dt_bias, v_ssd_A_log, v_ssd_D, v_ssd_norm_g, v_w_ssd_out, v_attn_q_norm, v_attn_k_norm, v_attn_sinks, v_w_attn_out, v_w_mix_out, v_norm_xattn, v_norm_mem, v_w_xq, v_w_xkv, v_xattn_q_norm, v_xattn_k_norm, v_w_xo, v_norm_mlp, v_w_mlp_up, v_w_mlp_down):
    args = locals()
    p = {n: args[n] for n in WEIGHTS}
    m = {n: args["m_" + n] for n in WEIGHTS}
    v = {n: args["v_" + n] for n in WEIGHTS}
    return _step(p, m, v, x, mem, loss_target)
```

```python
import functools
import math

import numpy as np
import jax
import jax.numpy as jnp
from jax import lax
from jax.experimental import pallas as pl
from jax.experimental.pallas import tpu as pltpu

F32 = jnp.float32
BF16 = jnp.bfloat16
HI = lax.Precision.HIGHEST
MESH = pl.DeviceIdType.MESH

N_DEV = 8
D_MODEL = 1024
DEPTH = 4
MEM_LEN = 256
EPS = 1e-6
NEG_INF = -1e30
CONV_KERNEL = 31
SSD_INNER = 2048
SSD_HEAD_DIM = 64
SSD_HEADS = 32
SSD_GROUPS = 4
SSD_STATE = 128
SSD_CONV = 4
SSD_CHUNK = 128
SSD_XBC = SSD_INNER + 2 * SSD_GROUPS * SSD_STATE
HEADS_PER_GROUP = SSD_HEADS // SSD_GROUPS
ATTN_HEADS = 16
ATTN_KV_HEADS = 4
ATTN_HEAD_DIM = 64
ATTN_BLOCK = 128
ATTN_REP = ATTN_HEADS // ATTN_KV_HEADS
REL_BUCKETS = 32
REL_MAX_DIST = 128
XATTN_HEADS = 4
XATTN_HEAD_DIM = 256
MLP_HIDDEN = 4096
OFF_Z = 2048
OFF_XBC = 4096
OFF_DT = 7168
OFF_Q = 7200
OFF_GATE = 8736
IN_COLS = 11808
LANES = 128
CONV_PAD = 32
MM_VMEM_BUDGET = 20 << 20

ADAM_LR, ADAM_B1, ADAM_B2, ADAM_EPS, ADAM_WD, ADAM_STEP = 0.001, 0.9, 0.999, 1e-08, 0.01, 10

WEIGHTS = ['rel_table', 'norm_mix', 'w_in', 'gate_bias', 'conv_dw_w', 'conv_dw_b', 'conv_ln_g', 'conv_ln_b', 'w_conv_out',
           'ssd_conv_w', 'ssd_conv_b', 'ssd_dt_bias', 'ssd_A_log', 'ssd_D', 'ssd_norm_g', 'w_ssd_out', 'attn_q_norm',
           'attn_k_norm', 'attn_sinks', 'w_attn_out', 'w_mix_out', 'norm_xattn', 'norm_mem', 'w_xq', 'w_xkv', 'xattn_q_norm',
           'xattn_k_norm', 'w_xo', 'norm_mlp', 'w_mlp_up', 'w_mlp_down']
ROW_SHARDED = ['w_conv_out', 'w_ssd_out', 'w_attn_out', 'w_mix_out', 'w_xq', 'w_xo', 'w_mlp_down']
COL_SHARDED = ['w_in', 'w_xkv', 'w_mlp_up']
BIG = ROW_SHARDED + COL_SHARDED
SMALL_SHARDED = ['gate_bias', 'conv_dw_w', 'ssd_conv_w']
SMALL = [n for n in WEIGHTS if n not in BIG]


def _bdot(a, b, dims):
    return lax.dot_general(a.astype(BF16), b.astype(BF16), (dims, ((), ())), preferred_element_type=F32)


def _hdot(a, b, dims):
    return lax.dot_general(a, b, (dims, ((), ())), precision=HI, preferred_element_type=F32)


def _dot3(x, s, dims, s_first=False):
    hi = x.astype(jnp.bfloat16)
    rest = x - hi.astype(F32)
    mid = rest.astype(jnp.bfloat16)
    lo = (rest - mid.astype(F32)).astype(jnp.bfloat16)
    dot = lambda piece: lax.dot_general(*((s, piece) if s_first else (piece, s)), (dims, ((), ())), preferred_element_type=F32)
    return dot(hi) + dot(mid) + dot(lo)


@jax.custom_vjp
def _select_dot(s, x):
    return _dot3(x, s, ((1,), (0,)), s_first=True)


def _select_dot_fwd(s, x):
    return _select_dot(s, x), s


def _select_dot_bwd(s, g):
    return jnp.zeros_like(s), _dot3(g, s, ((0,), (0,)), s_first=True)


_select_dot.defvjp(_select_dot_fwd, _select_dot_bwd)


@jax.custom_vjp
def _spread_dot(x, s):
    return _dot3(x, s, ((1,), (0,)))


def _spread_dot_fwd(x, s):
    return _spread_dot(x, s), s


def _spread_dot_bwd(s, g):
    return _dot3(g, s, ((1,), (1,))), jnp.zeros_like(s)


_spread_dot.defvjp(_spread_dot_fwd, _spread_dot_bwd)


def _one_hot_groups(rows, cols, group, transpose=False):
    r = lax.broadcasted_iota(jnp.int32, (rows, cols), 0)
    c = lax.broadcasted_iota(jnp.int32, (rows, cols), 1)
    shift = int(math.log2(group))
    hit = (lax.shift_right_logical(r, shift) == c) if transpose else (r == lax.shift_right_logical(c, shift))
    return hit.astype(jnp.bfloat16)


def _heads_rms(x, g):
    w = x.shape[-1]
    nh = w // ATTN_HEAD_DIM
    ss = _spread_dot(x * x, _one_hot_groups(w, nh, ATTN_HEAD_DIM, transpose=True))
    scale = lax.rsqrt(ss * (1.0 / ATTN_HEAD_DIM) + EPS)
    y = x * _spread_dot(scale, _one_hot_groups(nh, w, ATTN_HEAD_DIM)) * jnp.concatenate([g] * nh, axis=-1)
    return [y[:, h * ATTN_HEAD_DIM:(h + 1) * ATTN_HEAD_DIM] for h in range(nh)]


def _pick(dim, pref):
    if dim <= pref:
        return dim
    t = (pref // LANES) * LANES
    while dim % t:
        t -= LANES
    return t


def _logical(op):
    arr, kind, _ = op
    r, c = arr.shape[-2:]
    return (r, c * N_DEV) if kind == 'col' else (r, c)


def _opspec(op, br, bc, rc):
    arr, kind, layer = op
    lead = () if layer is None else (layer,)
    none = (None,) * len(lead)
    if kind == 'plain':
        return pl.BlockSpec(none + (br, bc), lambda i, j, k: lead + rc(i, j, k))
    per = arr.shape[-1] // bc

    def imap(i, j, k):
        r, c = rc(i, j, k)
        if per == 1:
            return lead + (c, r, 0)
        return lead + (lax.div(c, per), r, lax.rem(c, per))
    return pl.BlockSpec(none + (None, br, bc), imap)


def _mm(a, b, *, mode, name, add=None, out_dtype=F32, out_col=False, epi=None, tm=2048, tn=512, tk=1024):
    def operand(op):
        op = op if isinstance(op, tuple) else (op, 'plain', None)
        return (op[0][op[2]], op[1], None) if isinstance(op[0], list) else op

    a, b = operand(a), operand(b)
    ar, ac = _logical(a)
    br_, bc_ = _logical(b)
    if mode == 'nn':
        m, kd, n = ar, ac, bc_
        assert br_ == kd
    elif mode == 'nt':
        m, kd, n = ar, ac, br_
        assert bc_ == kd
    else:
        m, kd, n = ac, ar, bc_
        assert br_ == kd

    def lim(op, is_col_dim):
        return op[0].shape[-1] if (op[1] == 'col' and is_col_dim) else 1 << 30

    tm = _pick(m, min(tm, lim(a, mode == 'tn')))
    tn = _pick(n, min(tn, lim(b, mode != 'nt'), (n // N_DEV) if out_col else 1 << 30))
    tk = _pick(kd, min(tk, lim(a, mode != 'tn'), lim(b, mode == 'nt')))
    nk = kd // tk

    def vmem_bytes(tm_):
        out_bytes = sum(jnp.dtype(dt).itemsize for dt in (epi[2] if epi is not None else [out_dtype]))
        extra_bytes = sum(e.dtype.itemsize for e in (epi[1] if epi is not None else [])) + (add.dtype.itemsize if add is not None else 0)
        blocks = tm_ * tk * a[0].dtype.itemsize + tk * tn * b[0].dtype.itemsize + tm_ * tn * (out_bytes + extra_bytes)
        return 2 * blocks + tm_ * tn * 4 * (2 if nk > 1 else 1)

    while vmem_bytes(tm) > MM_VMEM_BUDGET and tm % 256 == 0 and tm > 256:
        tm //= 2
    if mode == 'nn':
        a_spec = _opspec(a, tm, tk, lambda i, j, k: (i, k))
        b_spec = _opspec(b, tk, tn, lambda i, j, k: (k, j))
        dims = ((1,), (0,))
    elif mode == 'nt':
        a_spec = _opspec(a, tm, tk, lambda i, j, k: (i, k))
        b_spec = _opspec(b, tn, tk, lambda i, j, k: (j, k))
        dims = ((1,), (1,))
    else:
        a_spec = _opspec(a, tk, tm, lambda i, j, k: (k, i))
        b_spec = _opspec(b, tk, tn, lambda i, j, k: (k, j))
        dims = ((0,), (0,))
    if out_col:
        out_shape = jax.ShapeDtypeStruct((N_DEV, m, n // N_DEV), out_dtype)
        out_spec = _opspec((out_shape, 'col', None), tm, tn, lambda i, j, k: (i, j))
    else:
        out_shape = jax.ShapeDtypeStruct((m, n), out_dtype)
        out_spec = pl.BlockSpec((tm, tn), lambda i, j, k: (i, j))
    has_add = add is not None
    epi_fn, epi_extra, epi_dtypes = epi if epi is not None else (None, [], [out_dtype])
    n_in = 2 + has_add + len(epi_extra)
    n_out = len(epi_dtypes)

    def body(*refs):
        a_ref, b_ref = refs[0], refs[1]
        add_ref = refs[2] if has_add else None
        extra_refs = refs[2 + has_add:n_in]
        o_refs = refs[n_in:n_in + n_out]

        def emit(acc):
            outs = (acc,) if epi_fn is None else epi_fn(acc, *[r[...] for r in extra_refs])
            for o_ref, o in zip(o_refs, outs):
                o_ref[...] = o.astype(o_ref.dtype)

        part = _bdot(a_ref[...], b_ref[...], dims)
        if nk == 1:
            emit(part + add_ref[...].astype(F32) if has_add else part)
            return
        acc_ref = refs[n_in + n_out]
        k = pl.program_id(2)

        @pl.when(k == 0)
        def _():
            acc_ref[...] = part + add_ref[...].astype(F32) if has_add else part

        @pl.when(k > 0)
        def _():
            acc_ref[...] += part

        @pl.when(k == nk - 1)
        def _():
            emit(acc_ref[...])

    block = pl.BlockSpec((tm, tn), lambda i, j, k: (i, j))
    in_specs = [a_spec, b_spec] + [block] * (has_add + len(epi_extra))
    args = [a[0], b[0]] + ([add] if has_add else []) + list(epi_extra)
    if epi is not None:
        assert not out_col
        out_spec = [block] * n_out
        out_shape = [jax.ShapeDtypeStruct((m, n), dt) for dt in epi_dtypes]
    return pl.pallas_call(
        body, name=name, grid=(m // tm, n // tn, nk), in_specs=in_specs, out_specs=out_spec, out_shape=out_shape,
        scratch_shapes=[pltpu.VMEM((tm, tn), F32)] if nk > 1 else [],
        compiler_params=pltpu.CompilerParams(dimension_semantics=("parallel", "parallel", "arbitrary")),
    )(*args)


def _rowwise(f, rows, consts, row_outs, acc_outs=(), *, tr, name):
    nr, nc, nro = len(rows), len(consts), len(row_outs)
    first = rows[0][0]
    t = first.shape[-2]
    assert t % tr == 0
    in_specs = []
    for spec in rows:
        arr, cb, w = spec[:3]
        lead = spec[3] if len(spec) > 3 else None
        if arr.ndim == 2:
            in_specs.append(pl.BlockSpec((tr, w), functools.partial(lambda i, cb: (i, cb), cb=cb)))
        elif lead is not None:
            in_specs.append(pl.BlockSpec((None, tr, w), functools.partial(lambda i, cb, lead: (lead, i, cb), cb=cb, lead=lead)))
        else:
            in_specs.append(pl.BlockSpec((arr.shape[0], tr, w), functools.partial(lambda i, cb: (0, i, cb), cb=cb)))
    for cst in consts:
        in_specs.append(pl.BlockSpec(cst.shape, functools.partial(lambda i, nd: (0,) * nd, nd=cst.ndim)))
    out_specs = [pl.BlockSpec((tr, w), lambda i: (i, 0)) for w, _ in row_outs]
    out_shape = [jax.ShapeDtypeStruct((t, w), dt) for w, dt in row_outs]
    for shp in acc_outs:
        out_specs.append(pl.BlockSpec(shp, functools.partial(lambda i, nd: (0,) * nd, nd=len(shp))))
        out_shape.append(jax.ShapeDtypeStruct(shp, F32))

    def body(*refs):
        ins = [r[...] for r in refs[:nr + nc]]
        ro = refs[nr + nc:nr + nc + nro]
        ao = refs[nr + nc + nro:]
        outs, accs = f(*ins)
        for o_ref, o in zip(ro, outs):
            o_ref[...] = o.astype(o_ref.dtype)
        if ao:
            i = pl.program_id(0)

            @pl.when(i == 0)
            def _():
                for a_ref, acc in zip(ao, accs):
                    a_ref[...] = acc

            @pl.when(i > 0)
            def _():
                for a_ref, acc in zip(ao, accs):
                    a_ref[...] += acc

    res = pl.pallas_call(
        body, name=name, grid=(t // tr,), in_specs=in_specs, out_specs=out_specs, out_shape=out_shape,
        compiler_params=pltpu.CompilerParams(dimension_semantics=("arbitrary",)),
    )(*[s[0] for s in rows], *consts)
    return res


def _vjp_rows(f, n_prim, n_rows_grad):
    def g(*args):
        prim, cots = args[:n_prim], args[n_prim:]
        outs, vjp = jax.vjp(f, *prim)
        grads = vjp(tuple(c.astype(o.dtype) for c, o in zip(cots, outs)))
        return tuple(grads[:n_rows_grad]), tuple(grads[n_rows_grad:])
    return g


def _rms(x, g):
    return x * lax.rsqrt(jnp.mean(x * x, axis=-1, keepdims=True) + EPS) * g


def _f_rms(h, g):
    return (_rms(h, g),)


def _f_rms_bwd(h, g, du, dres):
    _, vjp = jax.vjp(_f_rms, h, g)
    dh, dg = vjp((du.astype(F32),))
    return (dh + dres, dh + dres), (dg,)


def _f_lnsilu(x, g, b):
    mu = jnp.mean(x, axis=-1, keepdims=True)
    xc = x - mu
    y = xc * lax.rsqrt(jnp.mean(xc * xc, axis=-1, keepdims=True) + EPS) * g + b
    return (jax.nn.silu(y),)


def _f_ssdgate(y, z, g):
    y = y * jax.nn.silu(z)
    w = SSD_INNER // SSD_GROUPS
    return (jnp.concatenate([_rms(y[:, i * w:(i + 1) * w], g[:, i * w:(i + 1) * w]) for i in range(SSD_GROUPS)], axis=-1),)


def _f_merge(pg, ya, yb, yc, gb):
    out = 0.0
    for i, yi in enumerate((ya, yb, yc)):
        out = out + jax.nn.sigmoid(pg[:, i * D_MODEL:(i + 1) * D_MODEL] + gb[i:i + 1, :]) * yi
    return (out,)


def _f_relu2(a):
    return (jnp.square(jnp.maximum(a, 0.0)),)


def _f_xattn(q, kv, qg, kg):
    outs = []
    for h in range(XATTN_HEADS):
        sl = slice(h * XATTN_HEAD_DIM, (h + 1) * XATTN_HEAD_DIM)
        qh = _rms(q[:, sl], qg)
        kh = _rms(kv[:, sl], kg)
        vh = kv[:, D_MODEL + h * XATTN_HEAD_DIM:D_MODEL + (h + 1) * XATTN_HEAD_DIM]
        s = _bdot(qh, kh, ((1,), (1,))) * (XATTN_HEAD_DIM ** -0.5)
        p = jnp.exp(s - jnp.max(s, axis=-1, keepdims=True))
        p = p * (1.0 / jnp.sum(p, axis=-1, keepdims=True))
        outs.append(_bdot(p, vh, ((1,), (0,))))
    return (jnp.concatenate(outs, axis=-1),)


def _f_loss(y, tgt):
    err = y - tgt
    per_row = jnp.sum(err * err, axis=-1, keepdims=True) * (0.5 / D_MODEL)
    loss = jnp.sum(per_row, axis=0, keepdims=True)
    dy = err * (1.0 / D_MODEL)
    return (dy, dy), (jnp.broadcast_to(loss, (1, LANES)),)


def _adam_core(w, g, m, v):
    m = ADAM_B1 * m + (1.0 - ADAM_B1) * g
    v = ADAM_B2 * v + (1.0 - ADAM_B2) * jnp.square(g)
    m_hat = m / (1.0 - ADAM_B1 ** ADAM_STEP)
    v_hat = v / (1.0 - ADAM_B2 ** ADAM_STEP)
    delta = -ADAM_LR * (m_hat / (jnp.sqrt(v_hat) + ADAM_EPS) + ADAM_WD * w)
    return delta, m, v


def _sum_slots(g8):
    g = g8[0].astype(F32)
    for s in range(1, N_DEV):
        g = g + g8[s].astype(F32)
    return g


def _f_adam_slots(g8, w, m, v):
    g = _sum_slots(g8)
    return (g,) + _adam_core(w, g, m, v), ()


def _f_sum_slots(g8):
    return (_sum_slots(g8),), ()


def _f_adam(g, w, m, v):
    return _adam_core(w, g, m, v), ()


def _conv_chunk(t):
    return 256 if t % 256 == 0 else t


def _conv_fwd(srcs, w, b, *, kk, pre_glu, post_silu, name):
    t = srcs[0][0].shape[0]
    c = w.shape[-1]
    tt = _conv_chunk(t)
    ns = len(srcs)

    def body(*refs):
        w_ref, b_ref, o_ref, pad_ref = refs[ns:]
        if pre_glu:
            xin = refs[0][...] * jax.nn.sigmoid(refs[1][...])
        else:
            xin = refs[0][...]
        pad_ref[0:CONV_PAD, :] = jnp.zeros((CONV_PAD, LANES), F32)
        pad_ref[CONV_PAD:, :] = xin

        def chunk(i, carry):
            base = pl.multiple_of(i * tt, tt)
            acc = jnp.broadcast_to(b_ref[...], (tt, LANES))
            for j in range(kk):
                acc = acc + pad_ref[pl.ds(base + CONV_PAD - (kk - 1) + j, tt), :] * w_ref[j:j + 1, :]
            o_ref[pl.ds(base, tt), :] = jax.nn.silu(acc) if post_silu else acc
            return carry
        lax.fori_loop(0, t // tt, chunk, 0)

    in_specs = [pl.BlockSpec((t, LANES), functools.partial(lambda i, off: (0, off + i), off=off)) for _, off in srcs]
    in_specs += [pl.BlockSpec((kk, LANES), lambda i: (0, i)), pl.BlockSpec((1, LANES), lambda i: (0, i))]
    return pl.pallas_call(
        body, name=name, grid=(c // LANES,), in_specs=in_specs, out_specs=pl.BlockSpec((t, LANES), lambda i: (0, i)),
        out_shape=jax.ShapeDtypeStruct((t, c), F32), scratch_shapes=[pltpu.VMEM((t + CONV_PAD, LANES), F32)],
        compiler_params=pltpu.CompilerParams(dimension_semantics=("parallel",)),
    )(*[s[0] for s in srcs], w, b)


def _conv_bwd(srcs, w, b, dys, *, kk, pre_glu, post_silu, name, dx_dtype):
    t = srcs[0][0].shape[0]
    c = w.shape[-1]
    tt = _conv_chunk(t)
    ns, nd = len(srcs), len(dys)

    def body(*refs):
        src_refs = refs[:ns]
        dy_refs = refs[ns:ns + nd]
        w_ref, b_ref = refs[ns + nd:ns + nd + 2]
        outs = refs[ns + nd + 2:]
        dx_refs, dw_ref, db_ref = outs[:ns], outs[ns], outs[ns + 1]
        pad_ref, dpad_ref = outs[ns + 2:]
        cb = pl.program_id(0)
        if pre_glu:
            a_in = src_refs[0][...]
            sg = jax.nn.sigmoid(src_refs[1][...])
            xin = a_in * sg
        else:
            xin = src_refs[0][...]
        pad_ref[0:CONV_PAD, :] = jnp.zeros((CONV_PAD, LANES), F32)
        pad_ref[CONV_PAD:, :] = xin
        dpad_ref[t:, :] = jnp.zeros((CONV_PAD, LANES), F32)
        dw_ref[...] = jnp.zeros_like(dw_ref)
        db_ref[...] = jnp.zeros_like(db_ref)

        def load_dy(base):
            dy = dy_refs[0][pl.ds(base, tt), :].astype(F32)
            for (_, first, _n), r in zip(dys[1:], dy_refs[1:]):
                dy = jnp.where(cb >= first, r[pl.ds(base, tt), :].astype(F32), dy)
            return dy

        def chunk1(i, carry):
            base = pl.multiple_of(i * tt, tt)
            dy = load_dy(base)
            if post_silu:
                acc = jnp.broadcast_to(b_ref[...], (tt, LANES))
                for j in range(kk):
                    acc = acc + pad_ref[pl.ds(base + CONV_PAD - (kk - 1) + j, tt), :] * w_ref[j:j + 1, :]
                s = jax.nn.sigmoid(acc)
                dy = dy * (s * (1.0 + acc * (1.0 - s)))
            dpad_ref[pl.ds(base, tt), :] = dy
            db_ref[...] += jnp.sum(dy, axis=0, keepdims=True)
            for j in range(kk):
                dw_ref[j:j + 1, :] += jnp.sum(dy * pad_ref[pl.ds(base + CONV_PAD - (kk - 1) + j, tt), :], axis=0, keepdims=True)
            return carry
        lax.fori_loop(0, t // tt, chunk1, 0)

        def chunk2(i, carry):
            base = pl.multiple_of(i * tt, tt)
            acc = jnp.zeros((tt, LANES), F32)
            for j in range(kk):
                acc = acc + dpad_ref[pl.ds(base + (kk - 1) - j, tt), :] * w_ref[j:j + 1, :]
            if pre_glu:
                a_c = src_refs[0][pl.ds(base, tt), :]
                s_c = jax.nn.sigmoid(src_refs[1][pl.ds(base, tt), :])
                dx_refs[0][pl.ds(base, tt), :] = (acc * s_c).astype(dx_dtype)
                dx_refs[1][pl.ds(base, tt), :] = (acc * a_c * s_c * (1.0 - s_c)).astype(dx_dtype)
            else:
                dx_refs[0][pl.ds(base, tt), :] = acc.astype(dx_dtype)
            return carry
        lax.fori_loop(0, t // tt, chunk2, 0)

    in_specs = [pl.BlockSpec((t, LANES), functools.partial(lambda i, off: (0, off + i), off=off)) for _, off in srcs]
    for _, first, n in dys:
        in_specs.append(pl.BlockSpec((t, LANES), functools.partial(lambda i, first, n: (0, jnp.clip(i - first, 0, n - 1)), first=first, n=n)))
    in_specs += [pl.BlockSpec((kk, LANES), lambda i: (0, i)), pl.BlockSpec((1, LANES), lambda i: (0, i))]
    out_specs = [pl.BlockSpec((t, LANES), lambda i: (0, i)) for _ in srcs]
    out_specs += [pl.BlockSpec((kk, LANES), lambda i: (0, i)), pl.BlockSpec((1, LANES), lambda i: (0, i))]
    out_shape = [jax.ShapeDtypeStruct((t, c), dx_dtype) for _ in srcs]
    out_shape += [jax.ShapeDtypeStruct((kk, c), F32), jax.ShapeDtypeStruct((1, c), F32)]
    return pl.pallas_call(
        body, name=name, grid=(c // LANES,), in_specs=in_specs, out_specs=out_specs, out_shape=out_shape,
        scratch_shapes=[pltpu.VMEM((t + CONV_PAD, LANES), F32), pltpu.VMEM((t + CONV_PAD, LANES), F32)],
        compiler_params=pltpu.CompilerParams(dimension_semantics=("parallel",)),
    )(*[s[0] for s in srcs], *[d[0] for d in dys], w, b)


def _ssd_chunk(xbc, dtfull, s_in, dt_bias, a_log, dskip):
    q = xbc.shape[0]
    gw = SSD_INNER // SSD_GROUPS
    x = xbc[:, :SSD_INNER]
    dt_all = jax.nn.softplus(dtfull + dt_bias)
    da_all = dt_all * (-jnp.exp(a_log))
    row = lax.broadcasted_iota(jnp.int32, (q, q), 0)
    col = lax.broadcasted_iota(jnp.int32, (q, q), 1)
    causal = row >= col
    cs = _select_dot(causal.astype(jnp.bfloat16), da_all)
    cs_last = cs[q - 1:q, :]
    per_head = jnp.concatenate([dt_all, jnp.exp(cs), jnp.exp(cs_last - cs), jnp.broadcast_to(dskip, (8, LANES))], axis=0)
    wide = _spread_dot(per_head, _one_hot_groups(LANES, SSD_INNER, SSD_HEAD_DIM))
    xdt = x * wide[0:q]
    from_start, to_end, d_wide = wide[q:2 * q], wide[2 * q:3 * q], wide[3 * q:3 * q + 1]
    xdt_end = xdt * to_end
    chunk_decay = jnp.exp(cs_last)
    decay_rows = jnp.concatenate([jnp.broadcast_to(chunk_decay[:, h:h + 1], (SSD_HEAD_DIM, 1)) for h in range(SSD_HEADS)], axis=0)
    cs_t = jnp.transpose(cs)
    y_off, new_states, y_diag = [], [], []
    for g in range(SSD_GROUPS):
        bm = xbc[:, SSD_INNER + g * SSD_STATE:SSD_INNER + (g + 1) * SSD_STATE]
        cm = xbc[:, SSD_INNER + (SSD_GROUPS + g) * SSD_STATE:SSD_INNER + (SSD_GROUPS + g + 1) * SSD_STATE]
        cb = _bdot(cm, bm, ((1,), (1,)))
        y_off.append(_bdot(cm, s_in[g * gw:(g + 1) * gw, :], ((1,), (1,))))
        new_states.append(_bdot(xdt_end[:, g * gw:(g + 1) * gw], bm, ((0,), (0,))))
        for j in range(HEADS_PER_GROUP):
            h = g * HEADS_PER_GROUP + j
            diff = cs[:, h:h + 1] - cs_t[h:h + 1, :]
            decay = jnp.where(causal, jnp.exp(jnp.where(causal, diff, 0.0)), 0.0)
            y_diag.append(_bdot(cb * decay, xdt[:, h * SSD_HEAD_DIM:(h + 1) * SSD_HEAD_DIM], ((1,), (0,))))
    s_out = s_in * decay_rows + jnp.concatenate(new_states, axis=0)
    y = jnp.concatenate(y_diag, axis=-1) + jnp.concatenate(y_off, axis=-1) * from_start + x * d_wide
    return y, s_out


def _ssd_fwd(xbc, pdt, dt_bias, a_log, dskip, *, name, rider=None):
    t = xbc.shape[0]
    nc = t // SSD_CHUNK
    ride = _Ride(rider, 5, 2, 1)

    def body(*refs):
        (x_ref, dt_ref, tb_ref, al_ref, d_ref), (y_ref, s_ref), (state_ref,) = ride.split(refs)
        c = pl.program_id(0)
        ride.start(refs, c == 0)

        @pl.when(c == 0)
        def _():
            state_ref[...] = jnp.zeros_like(state_ref)
        s_in = state_ref[...]
        s_ref[...] = s_in
        y, s_out = _ssd_chunk(x_ref[...], dt_ref[...], s_in, tb_ref[...], al_ref[...], d_ref[...])
        y_ref[...] = y
        state_ref[...] = s_out
        ride.finish(refs, c == nc - 1)

    vec = pl.BlockSpec((1, LANES), lambda c: (0, 0))
    res = pl.pallas_call(
        body, name=name, grid=(nc,),
        in_specs=[pl.BlockSpec((SSD_CHUNK, SSD_XBC), lambda c: (c, 0)), pl.BlockSpec((SSD_CHUNK, LANES), lambda c: (c, 0)), vec, vec, vec] + ride.in_specs,
        out_specs=[pl.BlockSpec((SSD_CHUNK, SSD_INNER), lambda c: (c, 0)), pl.BlockSpec((None, SSD_INNER, SSD_STATE), lambda c: (c, 0, 0))] + ride.out_specs,
        out_shape=[jax.ShapeDtypeStruct((t, SSD_INNER), F32), jax.ShapeDtypeStruct((nc, SSD_INNER, SSD_STATE), F32)] + ride.out_shape,
        scratch_shapes=[pltpu.VMEM((SSD_INNER, SSD_STATE), F32)] + ride.scratch,
        compiler_params=pltpu.CompilerParams(dimension_semantics=("arbitrary",)),
    )(xbc, pdt, dt_bias, a_log, dskip, *ride.args)
    return res[:2], res[2:]


def _ssd_bwd(xbc, pdt, states, dy, dt_bias, a_log, dskip, *, name, rider=None):
    t = xbc.shape[0]
    nc = t // SSD_CHUNK
    rev = lambda c: nc - 1 - c
    ride = _Ride(rider, 7, 5, 1)

    def body(*refs):
        (x_ref, dt_ref, s_ref, dy_ref, tb_ref, al_ref, d_ref), (dx_ref, ddt_ref, dtb_ref, dal_ref, dd_ref), (dstate_ref,) = ride.split(refs)
        c = pl.program_id(0)
        ride.start(refs, c == 0)

        @pl.when(c == 0)
        def _():
            dstate_ref[...] = jnp.zeros_like(dstate_ref)
            dtb_ref[...] = jnp.zeros_like(dtb_ref)
            dal_ref[...] = jnp.zeros_like(dal_ref)
            dd_ref[...] = jnp.zeros_like(dd_ref)
        _, vjp = jax.vjp(_ssd_chunk, x_ref[...], dt_ref[...], s_ref[...], tb_ref[...], al_ref[...], d_ref[...])
        dx, ddt, ds_in, dtb, dal, dd = vjp((dy_ref[...].astype(F32), dstate_ref[...]))
        dx_ref[...] = dx
        ddt_ref[...] = ddt
        dstate_ref[...] = ds_in
        dtb_ref[...] += dtb
        dal_ref[...] += dal
        dd_ref[...] += dd
        ride.finish(refs, c == nc - 1)

    vec = pl.BlockSpec((1, LANES), lambda c: (0, 0))
    in_specs = [pl.BlockSpec((SSD_CHUNK, SSD_XBC), lambda c: (rev(c), 0)), pl.BlockSpec((SSD_CHUNK, LANES), lambda c: (rev(c), 0)),
                pl.BlockSpec((None, SSD_INNER, SSD_STATE), lambda c: (rev(c), 0, 0)), pl.BlockSpec((SSD_CHUNK, SSD_INNER), lambda c: (rev(c), 0)),
                vec, vec, vec]
    out_specs = [pl.BlockSpec((SSD_CHUNK, SSD_XBC), lambda c: (rev(c), 0)), pl.BlockSpec((SSD_CHUNK, LANES), lambda c: (rev(c), 0)), vec, vec, vec]
    out_shape = [jax.ShapeDtypeStruct((t, SSD_XBC), F32), jax.ShapeDtypeStruct((t, LANES), F32)] + [jax.ShapeDtypeStruct((1, LANES), F32)] * 3
    res = pl.pallas_call(
        body, name=name, grid=(nc,), in_specs=in_specs + ride.in_specs, out_specs=out_specs + ride.out_specs,
        out_shape=out_shape + ride.out_shape, scratch_shapes=[pltpu.VMEM((SSD_INNER, SSD_STATE), F32)] + ride.scratch,
        compiler_params=pltpu.CompilerParams(dimension_semantics=("arbitrary",)),
    )(xbc, pdt, states, dy, dt_bias, a_log, dskip, *ride.args)
    return res[:5], res[5:]


def _rel_buckets():
    qi = np.arange(ATTN_BLOCK)[:, None] + ATTN_BLOCK
    kj = np.arange(2 * ATTN_BLOCK)[None, :]
    dist = qi - kj
    max_exact = REL_BUCKETS // 2
    d = np.maximum(dist, 1).astype(np.float32)
    large = max_exact + (np.log(d / np.float32(max_exact)) / np.float32(math.log(REL_MAX_DIST / max_exact))
                         * np.float32(REL_BUCKETS - max_exact)).astype(np.int32)
    large = np.minimum(large, REL_BUCKETS - 1)
    return np.where(dist < max_exact, np.maximum(dist, 0), large).astype(np.int32)


def _onehot_buckets(bucket_ref):
    n = bucket_ref.shape[-1]
    return (lax.broadcasted_iota(jnp.int32, (REL_BUCKETS, n), 0) == bucket_ref[...]).astype(F32)


def _band_bias(rel_table_t, buckets):
    n = buckets.shape[-1]

    def body(rt_ref, bk_ref, o_ref):
        o_ref[...] = _hdot(rt_ref[...], _onehot_buckets(bk_ref), ((1,), (0,)))
    return pl.pallas_call(body, name="band_bias", out_shape=jax.ShapeDtypeStruct((ATTN_HEADS, n), F32))(rel_table_t, buckets)


def _band_bias_bwd(dbias, buckets):
    def body(db_ref, bk_ref, o_ref):
        d = db_ref[0]
        for layer in range(1, db_ref.shape[0]):
            d = d + db_ref[layer]
        o_ref[...] = _hdot(d, _onehot_buckets(bk_ref), ((1,), (1,)))
    return pl.pallas_call(body, name="band_bias_bwd", out_shape=jax.ShapeDtypeStruct((ATTN_HEADS, REL_BUCKETS), F32))(dbias, buckets)


def _heads_rms_lanes(x, g):
    return [_rms(x[:, h * ATTN_HEAD_DIM:(h + 1) * ATTN_HEAD_DIM], g) for h in range(x.shape[-1] // ATTN_HEAD_DIM)]


def _attn_block(q, kvp, kvc, bias, sinks, qg, kg, first, norm=_heads_rms):
    qn = q.shape[0]
    rows = ATTN_REP * qn
    ri = lax.broadcasted_iota(jnp.int32, (rows, 2 * qn), 0) & (qn - 1)
    cj = lax.broadcasted_iota(jnp.int32, (rows, 2 * qn), 1)
    jj = cj & (qn - 1)
    no_prev = jnp.where(first, qn, 0)
    mask = ((cj < qn) & (jj > ri + no_prev)) | ((cj >= qn) & (jj <= ri))
    kvd = ATTN_KV_HEADS * ATTN_HEAD_DIM
    q_normed = norm(q, qg)
    kn_prev, kn_cur = norm(kvp[:, :kvd], kg), norm(kvc[:, :kvd], kg)
    outs = []
    for g in range(ATTN_KV_HEADS):
        vsl = slice(kvd + g * ATTN_HEAD_DIM, kvd + (g + 1) * ATTN_HEAD_DIM)
        qs = jnp.concatenate(q_normed[g * ATTN_REP:(g + 1) * ATTN_REP], axis=0)
        kb = jnp.concatenate([kn_prev[g], kn_cur[g]], axis=0)
        vb = jnp.concatenate([kvp[:, vsl], kvc[:, vsl]], axis=0)
        logits = _bdot(qs, kb, ((1,), (1,))) * (ATTN_HEAD_DIM ** -0.5) + bias[g]
        logits = jnp.where(mask, logits, NEG_INF)
        sink = jnp.concatenate([jnp.broadcast_to(sinks[:, g * ATTN_REP + j:g * ATTN_REP + j + 1], (qn, 1)) for j in range(ATTN_REP)], axis=0)
        m = jnp.maximum(jnp.max(logits, axis=-1, keepdims=True), sink)
        pexp = jnp.exp(logits - m)
        probs = pexp * (1.0 / (jnp.sum(pexp, axis=-1, keepdims=True) + jnp.exp(sink - m)))
        o = _bdot(probs, vb, ((1,), (0,)))
        outs += [o[j * qn:(j + 1) * qn, :] for j in range(ATTN_REP)]
    return jnp.concatenate(outs, axis=-1)


def _attn_specs(nmap):
    qd, kvw = ATTN_HEADS * ATTN_HEAD_DIM, 2 * ATTN_KV_HEADS * ATTN_HEAD_DIM
    full = lambda shp: pl.BlockSpec(shp, lambda i: (0,) * len(shp))
    return [
        pl.BlockSpec((ATTN_BLOCK, qd), lambda i: (nmap(i), 0)),
        pl.BlockSpec((ATTN_BLOCK, kvw), lambda i: (jnp.maximum(nmap(i) - 1, 0), qd // kvw)),
        pl.BlockSpec((ATTN_BLOCK, kvw), lambda i: (nmap(i), qd // kvw)),
        full((ATTN_KV_HEADS, ATTN_REP * ATTN_BLOCK, 2 * ATTN_BLOCK)), full((1, ATTN_HEADS)), full((1, ATTN_HEAD_DIM)), full((1, ATTN_HEAD_DIM)),
    ]


def _attn_fwd(pqkv, bias, sinks, qg, kg, *, name, rider=None):
    t = pqkv.shape[0]
    nb = t // ATTN_BLOCK
    qd = ATTN_HEADS * ATTN_HEAD_DIM
    ride = _Ride(rider, 7, 1, 0)

    def body(*refs):
        (q_ref, kvp_ref, kvc_ref, bias_ref, sk_ref, qg_ref, kg_ref), (o_ref,), _ = ride.split(refs)
        i = pl.program_id(0)
        ride.start(refs, i == 0)
        o_ref[...] = _attn_block(q_ref[...], kvp_ref[...], kvc_ref[...], bias_ref[...], sk_ref[...], qg_ref[...], kg_ref[...], i == 0,
                                 norm=_heads_rms_lanes).astype(o_ref.dtype)
        ride.finish(refs, i == nb - 1)

    res = pl.pallas_call(
        body, name=name, grid=(nb,), in_specs=_attn_specs(lambda i: i) + ride.in_specs,
        out_specs=[pl.BlockSpec((ATTN_BLOCK, qd), lambda i: (i, 0))] + ride.out_specs,
        out_shape=[jax.ShapeDtypeStruct((t, qd), BF16)] + ride.out_shape, scratch_shapes=ride.scratch,
        compiler_params=pltpu.CompilerParams(dimension_semantics=("arbitrary",)),
    )(pqkv, pqkv, pqkv, bias, sinks, qg, kg, *ride.args)
    return res[0], res[1:]


def _attn_bwd(pqkv, do, bias, sinks, qg, kg, *, name, rider=None):
    t = pqkv.shape[0]
    nb = t // ATTN_BLOCK
    qd, kvw = ATTN_HEADS * ATTN_HEAD_DIM, 2 * ATTN_KV_HEADS * ATTN_HEAD_DIM
    rev = lambda i: nb - 1 - i
    ride = _Ride(rider, 8, 5, 1)

    def body(*refs):
        (q_ref, kvp_ref, kvc_ref, bias_ref, sk_ref, qg_ref, kg_ref, do_ref), (dqkv_ref, dbias_ref, dsk_ref, dqg_ref, dkg_ref), (carry_ref,) = ride.split(refs)
        i = pl.program_id(0)
        ride.start(refs, i == 0)
        first = rev(i) == 0
        f = functools.partial(_attn_block, first=first)
        _, vjp = jax.vjp(f, q_ref[...], kvp_ref[...], kvc_ref[...], bias_ref[...], sk_ref[...], qg_ref[...], kg_ref[...])
        dq, dkvp, dkvc, dbias, dsk, dqg, dkg = vjp(do_ref[...].astype(F32))

        @pl.when(i == 0)
        def _():
            carry_ref[...] = jnp.zeros_like(carry_ref)
            dbias_ref[...] = jnp.zeros_like(dbias_ref)
            dsk_ref[...] = jnp.zeros_like(dsk_ref)
            dqg_ref[...] = jnp.zeros_like(dqg_ref)
            dkg_ref[...] = jnp.zeros_like(dkg_ref)
        dqkv_ref[:, 0:qd] = dq.astype(dqkv_ref.dtype)
        dqkv_ref[:, qd:] = (dkvc + carry_ref[...]).astype(dqkv_ref.dtype)
        carry_ref[...] = dkvp
        dbias_ref[...] += dbias
        dsk_ref[...] += dsk
        dqg_ref[...] += dqg
        dkg_ref[...] += dkg
        ride.finish(refs, i == nb - 1)

    full = lambda shp: pl.BlockSpec(shp, lambda i: (0,) * len(shp))
    bshape = (ATTN_KV_HEADS, ATTN_REP * ATTN_BLOCK, 2 * ATTN_BLOCK)
    res = pl.pallas_call(
        body, name=name, grid=(nb,), in_specs=_attn_specs(rev) + [pl.BlockSpec((ATTN_BLOCK, qd), lambda i: (rev(i), 0))] + ride.in_specs,
        out_specs=[pl.BlockSpec((ATTN_BLOCK, qd + kvw), lambda i: (rev(i), 0)), full(bshape), full((1, ATTN_HEADS)),
                   full((1, ATTN_HEAD_DIM)), full((1, ATTN_HEAD_DIM))] + ride.out_specs,
        out_shape=[jax.ShapeDtypeStruct((t, qd + kvw), BF16), jax.ShapeDtypeStruct(bshape, F32), jax.ShapeDtypeStruct((1, ATTN_HEADS), F32),
                   jax.ShapeDtypeStruct((1, ATTN_HEAD_DIM), F32), jax.ShapeDtypeStruct((1, ATTN_HEAD_DIM), F32)] + ride.out_shape,
        scratch_shapes=[pltpu.VMEM((ATTN_BLOCK, kvw), F32)] + ride.scratch,
        compiler_params=pltpu.CompilerParams(dimension_semantics=("arbitrary",)),
    )(pqkv, pqkv, pqkv, bias, sinks, qg, kg, do, *ride.args)
    return res[:5], res[5:]


def _dev_index(dev):
    return 4 * dev[0] + 2 * dev[1] + dev[2]


def _gather_ops(x_refs, o_refs, send_sems, recv_sems, local_sems, axes):
    n = len(x_refs)
    x, y, c = lax.axis_index("x"), lax.axis_index("y"), lax.axis_index("c")
    me, sibling = (x, y, c), (x, y, 1 - c)
    chips = [(1 - x, y), (x, 1 - y), (1 - x, 1 - y)]

    def slot(i, dev):
        idx = _dev_index(dev)
        return o_refs[i].at[idx] if axes[i] == 0 else o_refs[i].at[:, idx]

    def copy(i, k, block, to, src=None):
        return pltpu.make_async_remote_copy(
            src_ref=slot(i, block) if src is None else src, dst_ref=slot(i, block),
            send_sem=send_sems.at[i, k], recv_sem=recv_sems.at[i, k], device_id=to, device_id_type=MESH)

    mine = [pltpu.make_async_copy(x_refs[i], slot(i, me), local_sems.at[i]) for i in range(n)]
    first = []
    for i in range(n):
        first.append(copy(i, 0, me, sibling, src=x_refs[i]))
        first += [copy(i, 1 + j, me, (*chip, c), src=x_refs[i]) for j, chip in enumerate(chips)]

    def start():
        for cp in mine + first:
            cp.start()

    def finish():
        passed = []
        for j, chip in enumerate(chips):
            for i in range(n):
                copy(i, 1 + j, (*chip, c), me).wait_recv()
                cp = copy(i, 4 + j, (*chip, c), sibling)
                cp.start()
                passed.append(cp)
        for i in range(n):
            copy(i, 0, sibling, me).wait_recv()
        for j, chip in enumerate(chips):
            for i in range(n):
                copy(i, 4 + j, (*chip, 1 - c), me).wait_recv()
        for cp in first + passed:
            cp.wait_send()
        for cp in mine:
            cp.wait()

    return start, finish


def _exchange_ops(g_refs, o_refs, send_sems, recv_sems, local_sems):
    n = len(g_refs)
    x, y, c = lax.axis_index("x"), lax.axis_index("y"), lax.axis_index("c")
    me = _dev_index((x, y, c))
    peers = [(x ^ ((k >> 2) & 1), y ^ ((k >> 1) & 1), c ^ (k & 1)) for k in range(1, N_DEV)]

    def copy(i, k):
        peer = peers[k]
        return pltpu.make_async_remote_copy(
            src_ref=g_refs[i].at[_dev_index(peer)], dst_ref=o_refs[i].at[me],
            send_sem=send_sems.at[i, k], recv_sem=recv_sems.at[i, k], device_id=peer, device_id_type=MESH)

    def arrival(i, k):
        peer = peers[k]
        return pltpu.make_async_remote_copy(
            src_ref=g_refs[i].at[me], dst_ref=o_refs[i].at[_dev_index(peer)],
            send_sem=send_sems.at[i, k], recv_sem=recv_sems.at[i, k], device_id=peer, device_id_type=MESH)

    mine = [pltpu.make_async_copy(g_refs[i].at[me], o_refs[i].at[me], local_sems.at[i]) for i in range(n)]
    sends = [copy(i, k) for i in range(n) for k in range(N_DEV - 1)]

    def start():
        for cp in mine + sends:
            cp.start()

    def finish():
        for i in range(n):
            for k in range(N_DEV - 1):
                arrival(i, k).wait_recv()
        for cp in sends:
            cp.wait_send()
        for cp in mine:
            cp.wait()

    return start, finish


class _Ride:
    def __init__(self, rider, n_in, n_out, n_scr):
        self.rider, self.n_in, self.n_out, self.n_scr = rider, n_in, n_out, n_scr
        self.args = [] if rider is None else list(rider[1])
        n = self.n = len(self.args)
        hbm = pl.BlockSpec(memory_space=pltpu.HBM)
        self.in_specs, self.out_specs = [hbm] * n, [hbm] * n
        if rider is None:
            self.out_shape, self.scratch = [], []
            return
        if rider[0] == 'gather':
            self.out_shape = [jax.ShapeDtypeStruct(a.shape[:ax] + (N_DEV,) + a.shape[ax:], a.dtype) for a, ax in zip(self.args, rider[2])]
        else:
            self.out_shape = [jax.ShapeDtypeStruct(a.shape, a.dtype) for a in self.args]
        self.scratch = [pltpu.SemaphoreType.DMA((n, 7)), pltpu.SemaphoreType.DMA((n, 7)), pltpu.SemaphoreType.DMA((n,))]

    def split(self, refs):
        a = self.n_in
        c = a + self.n + self.n_out
        e = c + self.n
        return refs[:a], refs[a + self.n:c], refs[e:e + self.n_scr]

    def _ops(self, refs):
        a = self.n_in
        c = a + self.n + self.n_out
        e = c + self.n + self.n_scr
        x_refs, o_refs, sems = refs[a:a + self.n], refs[c:c + self.n], refs[e:e + 3]
        if self.rider[0] == 'gather':
            return _gather_ops(x_refs, o_refs, *sems, self.rider[2])
        return _exchange_ops(x_refs, o_refs, *sems)

    def start(self, refs, cond):
        if self.rider is not None:
            pl.when(cond)(lambda: self._ops(refs)[0]())

    def finish(self, refs, cond):
        if self.rider is not None:
            pl.when(cond)(lambda: self._ops(refs)[1]())


def _comm_call(rider, *, name):
    ride = _Ride(rider, 0, 0, 0)

    def body(*refs):
        start, finish = ride._ops(refs)
        start()
        finish()

    return pl.pallas_call(body, name=name, in_specs=ride.in_specs, out_specs=ride.out_specs, out_shape=ride.out_shape,
                          scratch_shapes=ride.scratch)(*ride.args)


def _all_gather(xs, axes, *, name):
    return _comm_call(('gather', xs, axes), name=name)


def _exchange(gs, *, name):
    return _comm_call(('exchange', gs), name=name)


def _pack(arrays):
    parts = []
    for a in arrays:
        flat = a.reshape(-1)
        pad = (-flat.shape[0]) % LANES
        if pad:
            flat = jnp.concatenate([flat, jnp.zeros((pad,), flat.dtype)])
        parts.append(flat.reshape(-1, LANES))
    return jnp.concatenate(parts, axis=0)


def _unpack(buf, shapes):
    out, row = [], 0
    for shp in shapes:
        size = int(np.prod(shp))
        rows = -(-size // LANES)
        out.append(buf[row:row + rows].reshape(-1)[:size].reshape(shp))
        row += rows
    return out


def _row_tile(rows, width, n_bufs):
    padded = -(-width // LANES) * LANES
    cap = max(16, (12 << 20) // (padded * 4 * n_bufs))
    if rows <= cap:
        return rows
    tr = (cap // 16) * 16
    while tr > 16 and rows % tr:
        tr -= 16
    return tr if rows % tr == 0 else rows


def _step(p, m, v, x, mem, loss_target):
    t = x.shape[1]
    h0 = x.reshape(t, D_MODEL)
    mem2 = mem.reshape(MEM_LEN, D_MODEL)
    tgt = loss_target.reshape(t, D_MODEL)
    tr = 256 if t % 256 == 0 else t
    my = _dev_index((lax.axis_index("x"), lax.axis_index("y"), lax.axis_index("c")))

    small_sh_shapes = [p[n].shape for n in SMALL_SHARDED]
    gathered_small = _all_gather([_pack([p[n] for n in SMALL_SHARDED])], [0], name="gather_small")[0]
    full = {}
    for n, a in zip(SMALL_SHARDED, zip(*[_unpack(gathered_small[d], small_sh_shapes) for d in range(N_DEV)])):
        full[n] = jnp.concatenate(a, axis=-1)
    big_names = ROW_SHARDED + COL_SHARDED
    seg_order = ['a', 'ag', 'z', 'x', 'dt', 'qkv', 'g']
    wg = {n: [] for n in big_names}
    w_seg = {k: [] for k in seg_order}
    others = [n for n in big_names if n != 'w_in']

    def shards(layer, names):
        return [p[n][layer].astype(BF16) for n in names]

    def use_weights(layer_names, got):
        for n, a in zip([n for _, names in layer_names for n in names], got):
            if n in ROW_SHARDED:
                wg[n].append(a.reshape(a.shape[0] * a.shape[1], a.shape[2]))
            elif n != 'w_in':
                wg[n].append(a)
            else:
                w_in_full = jnp.transpose(a, (1, 0, 2)).reshape(D_MODEL, IN_COLS)
                segs = {'a': w_in_full[:, 0:D_MODEL], 'ag': w_in_full[:, D_MODEL:OFF_Z], 'z': w_in_full[:, OFF_Z:OFF_XBC],
                        'x': w_in_full[:, OFF_XBC:OFF_DT], 'dt': jnp.pad(w_in_full[:, OFF_DT:OFF_Q], ((0, 0), (0, LANES - SSD_HEADS))),
                        'qkv': w_in_full[:, OFF_Q:OFF_GATE], 'g': w_in_full[:, OFF_GATE:IN_COLS]}
                for k in seg_order:
                    w_seg[k].append(segs[k])

    def gather_rider(layer_names):
        xs = [a for layer, names in layer_names for a in shards(layer, names)]
        return ('gather', xs, [0] * len(xs)) if xs else None

    def riding(layer):
        if layer == 0:
            return [(0, others)], [(1, ['w_in']), (1, others)]
        if layer + 1 < DEPTH:
            return [(layer + 1, ['w_in'])], [(layer + 1, others)]
        return [], []

    use_weights([(0, ['w_in'])], _all_gather(shards(0, ['w_in']), [0], name="gather_weights"))

    def vec(name, layer, width=None):
        a = p[name][layer].reshape(1, -1)
        if width is not None and a.shape[1] < width:
            a = jnp.pad(a, ((0, 0), (0, width - a.shape[1])))
        return a

    buckets = jnp.asarray(_rel_buckets().reshape(1, -1))
    bias = _band_bias(jnp.transpose(p['rel_table']), buckets).reshape(ATTN_KV_HEADS, ATTN_REP * ATTN_BLOCK, 2 * ATTN_BLOCK)

    saved = []
    h = h0
    for l in range(DEPTH):
        s = {'h0': h}
        u = _rowwise(lambda a, g: (_f_rms(a, g), ()), [(h, 0, D_MODEL)], [vec('norm_mix', l)], [(D_MODEL, BF16)], tr=tr, name="rms_mix")[0]
        s['u'] = u
        pr = {k: _mm(u, (w_seg[k], 'plain', l), mode='nn', name="proj_" + k) for k in seg_order}
        s['pr'] = pr
        dw_w, dw_b = full['conv_dw_w'][l], vec('conv_dw_b', l)
        ca = _conv_fwd([(pr['a'], 0), (pr['ag'], 0)], dw_w, dw_b, kk=CONV_KERNEL, pre_glu=True, post_silu=False, name="conv31")
        s['ca'] = ca
        ya_in = _rowwise(lambda a, g, b: (_f_lnsilu(a, g, b), ()), [(ca, 0, D_MODEL)], [vec('conv_ln_g', l), vec('conv_ln_b', l)],
                         [(D_MODEL, BF16)], tr=tr, name="ln_silu")[0]
        s['ya_in'] = ya_in
        xbc = _conv_fwd([(pr['x'], 0)], full['ssd_conv_w'][l], vec('ssd_conv_b', l), kk=SSD_CONV, pre_glu=False, post_silu=True, name="conv4")
        s['xbc'] = xbc
        ssd_vecs = [vec('ssd_dt_bias', l, LANES), vec('ssd_A_log', l, LANES), vec('ssd_D', l, LANES)]
        with_ssd, with_swa = riding(l)
        (y_ssd, states), got = _ssd_fwd(xbc, pr['dt'], *ssd_vecs, name="ssd", rider=gather_rider(with_ssd))
        use_weights(with_ssd, got)
        s['y_ssd'], s['states'] = y_ssd, states
        yb_in = _rowwise(lambda a, z, g: (_f_ssdgate(a, z, g), ()), [(y_ssd, 0, SSD_INNER), (pr['z'], 0, SSD_INNER)], [vec('ssd_norm_g', l)],
                         [(SSD_INNER, BF16)], tr=tr, name="ssd_gate")[0]
        s['yb_in'] = yb_in
        y_b = _mm(yb_in, (wg['w_ssd_out'], 'plain', l), mode='nn', name="ssd_out")
        att, got = _attn_fwd(pr['qkv'], bias, vec('attn_sinks', l), vec('attn_q_norm', l), vec('attn_k_norm', l), name="swa",
                             rider=gather_rider(with_swa))
        use_weights(with_swa, got)
        s['att'] = att
        y_c = _mm(att, (wg['w_attn_out'], 'plain', l), mode='nn', name="attn_out")
        y_a = _mm(ya_in, (wg['w_conv_out'], 'plain', l), mode='nn', name="conv_out")
        s['y_a'], s['y_b'], s['y_c'] = y_a, y_b, y_c
        merged = _rowwise(lambda pg, a, b, c, gb: (_f_merge(pg, a, b, c, gb), ()),
                          [(pr['g'], 0, 3 * D_MODEL), (y_a, 0, D_MODEL), (y_b, 0, D_MODEL), (y_c, 0, D_MODEL)], [full['gate_bias'][l]],
                          [(D_MODEL, BF16)], tr=tr, name="merge")[0]
        s['merged'] = merged
        h = _mm(merged, (wg['w_mix_out'], 'plain', l), mode='nn', add=h, name="mix_out")
        s['h1'] = h
        un = _rowwise(lambda a, g: (_f_rms(a, g), ()), [(h, 0, D_MODEL)], [vec('norm_xattn', l)], [(D_MODEL, BF16)], tr=tr, name="rms_xattn")[0]
        memn = _rowwise(lambda a, g: (_f_rms(a, g), ()), [(mem2, 0, D_MODEL)], [vec('norm_mem', l)], [(D_MODEL, BF16)], tr=MEM_LEN, name="rms_mem")[0]
        s['un'], s['memn'] = un, memn
        xq = _mm(un, (wg['w_xq'], 'plain', l), mode='nn', name="xq")
        kv = _mm(memn, (wg['w_xkv'], 'col', l), mode='nn', name="xkv", tn=256)
        s['xq'], s['kv'] = xq, kv
        xo = _rowwise(lambda q, kvv, qg, kg: (_f_xattn(q, kvv, qg, kg), ()), [(xq, 0, D_MODEL)], [kv, vec('xattn_q_norm', l), vec('xattn_k_norm', l)],
                      [(D_MODEL, BF16)], tr=tr, name="xattn")[0]
        s['xo'] = xo
        h = _mm(xo, (wg['w_xo'], 'plain', l), mode='nn', add=h, name="xattn_out")
        s['h2'] = h
        um = _rowwise(lambda a, g: (_f_rms(a, g), ()), [(h, 0, D_MODEL)], [vec('norm_mlp', l)], [(D_MODEL, BF16)], tr=tr, name="rms_mlp")[0]
        s['um'] = um
        up, act = _mm(um, (wg['w_mlp_up'], 'col', l), mode='nn', name="mlp_up", epi=(lambda acc: (acc,) + _f_relu2(acc), [], [F32, BF16]))
        s['up'], s['act'] = up, act
        h = _mm(act, (wg['w_mlp_down'], 'plain', l), mode='nn', add=h, name="mlp_down")
        saved.append(s)

    dh, dh_b, loss_part = _rowwise(_f_loss, [(h, 0, D_MODEL), (tgt, 0, D_MODEL)], [], [(D_MODEL, F32), (D_MODEL, BF16)], [(1, LANES)], tr=tr, name="loss")

    sg = {n: [None] * DEPTH for n in SMALL if n != 'rel_table'}
    dbias_layers = [None] * DEPTH
    recv = {}
    with_d_swa = ['w_mlp_down', 'w_mlp_up', 'w_xo', 'w_xq', 'w_xkv', 'w_mix_out', 'w_attn_out']
    left_over = []

    def by_device(g_):
        return g_.reshape(N_DEV, g_.shape[0] // N_DEV, g_.shape[1])

    for l in reversed(range(DEPTH)):
        s = saved[l]
        bg = {}
        dup = _mm(dh_b, (wg['w_mlp_down'], 'plain', l), mode='nt', name="d_act",
                  epi=(lambda acc, up_: _vjp_rows(_f_relu2, 1, 1)(up_, acc)[0], [s['up']], [BF16]))[0]
        bg['w_mlp_down'] = _mm(s['act'], dh_b, mode='tn', out_dtype=BF16, name="dw_mlp_down")
        bg['w_mlp_up'] = _mm(s['um'], dup, mode='tn', out_dtype=BF16, out_col=True, name="dw_mlp_up")
        dum = _mm(dup, (wg['w_mlp_up'], 'col', l), mode='nt', name="d_um", tk=512)
        dh, dh_b, dg = _rms_bwd_call(s['h2'], vec('norm_mlp', l), dum, dh, tr, "d_rms_mlp")
        sg['norm_mlp'][l] = dg
        dxo = _mm(dh_b, (wg['w_xo'], 'plain', l), mode='nt', out_dtype=BF16, name="d_xo")
        bg['w_xo'] = _mm(s['xo'], dh_b, mode='tn', out_dtype=BF16, name="dw_xo")
        qg, kg = vec('xattn_q_norm', l), vec('xattn_k_norm', l)
        dxq, dkv, dqg, dkg = _xattn_bwd_call(s['xq'], s['kv'], qg, kg, dxo, tr)
        sg['xattn_q_norm'][l], sg['xattn_k_norm'][l] = dqg, dkg
        bg['w_xq'] = _mm(s['un'], dxq, mode='tn', out_dtype=BF16, name="dw_xq")
        dun = _mm(dxq, (wg['w_xq'], 'plain', l), mode='nt', name="d_un")
        dh, dh_b, dg = _rms_bwd_call(s['h1'], vec('norm_xattn', l), dun, dh, tr, "d_rms_xattn")
        sg['norm_xattn'][l] = dg
        bg['w_xkv'] = _mm(s['memn'], dkv, mode='tn', out_dtype=BF16, out_col=True, name="dw_xkv", tn=256)
        dmemn = _mm(dkv, (wg['w_xkv'], 'col', l), mode='nt', name="d_memn", tk=256)
        _, _, dg = _rms_bwd_call(mem2, vec('norm_mem', l), dmemn, jnp.zeros_like(mem2), MEM_LEN, "d_rms_mem")
        sg['norm_mem'][l] = dg
        dmerged = _mm(dh_b, (wg['w_mix_out'], 'plain', l), mode='nt', out_dtype=BF16, name="d_merged")
        bg['w_mix_out'] = _mm(s['merged'], dh_b, mode='tn', out_dtype=BF16, name="dw_mix_out")
        pr = s['pr']
        gb = full['gate_bias'][l]
        dpg, dya, dyb, dyc, dgb = _merge_bwd_call(pr['g'], s['y_a'], s['y_b'], s['y_c'], gb, dmerged, tr)
        sg['gate_bias'][l] = dgb
        dseg = {'g': dpg}
        datt = _mm(dyc, (wg['w_attn_out'], 'plain', l), mode='nt', out_dtype=BF16, name="d_att")
        bg['w_attn_out'] = _mm(s['att'], dyc, mode='tn', out_dtype=BF16, name="dw_attn_out")
        early = [bg[n] if n in COL_SHARDED else by_device(bg[n]) for n in with_d_swa]
        (dqkv, dbias_l, dsk, dqn, dkn), got = _attn_bwd(pr['qkv'], datt, bias, vec('attn_sinks', l), vec('attn_q_norm', l), vec('attn_k_norm', l),
                                                     name="d_swa", rider=('exchange', early))
        recv.update({(n, l): r for n, r in zip(with_d_swa, got)})
        dseg['qkv'] = dqkv
        dbias_layers[l] = dbias_l
        sg['attn_sinks'][l], sg['attn_q_norm'][l], sg['attn_k_norm'][l] = dsk, dqn, dkn
        dyb_in = _mm(dyb, (wg['w_ssd_out'], 'plain', l), mode='nt', out_dtype=BF16, name="d_yb_in")
        bg['w_ssd_out'] = _mm(s['yb_in'], dyb, mode='tn', out_dtype=BF16, name="dw_ssd_out")
        ng = vec('ssd_norm_g', l)
        dy_ssd, dz, dng = _ssdgate_bwd_call(s['y_ssd'], pr['z'], ng, dyb_in, tr)
        sg['ssd_norm_g'][l] = dng
        dseg['z'] = dz
        ssd_vecs = [vec('ssd_dt_bias', l, LANES), vec('ssd_A_log', l, LANES), vec('ssd_D', l, LANES)]
        dya_in = _mm(dya, (wg['w_conv_out'], 'plain', l), mode='nt', out_dtype=BF16, name="d_ya_in")
        bg['w_conv_out'] = _mm(s['ya_in'], dya, mode='tn', out_dtype=BF16, name="dw_conv_out")
        late = [('w_ssd_out', l, by_device(bg['w_ssd_out'])), ('w_conv_out', l, by_device(bg['w_conv_out']))] + left_over
        (dxbc_act, ddt, dtb, dal, ddsk), got = _ssd_bwd(s['xbc'], pr['dt'], s['states'], dy_ssd, *ssd_vecs, name="d_ssd",
                                                        rider=('exchange', [a for _, _, a in late]))
        recv.update({(n, ll): r for (n, ll, _), r in zip(late, got)})
        dseg['dt'] = ddt
        sg['ssd_dt_bias'][l], sg['ssd_A_log'][l], sg['ssd_D'][l] = dtb[:, :SSD_HEADS], dal[:, :SSD_HEADS], ddsk[:, :SSD_HEADS]
        dxbc, dcw, dcb = _conv_bwd([(pr['x'], 0)], full['ssd_conv_w'][l], vec('ssd_conv_b', l), [(dxbc_act, 0, SSD_XBC // LANES)],
                                   kk=SSD_CONV, pre_glu=False, post_silu=True, name="d_conv4", dx_dtype=BF16)
        dseg['x'] = dxbc
        sg['ssd_conv_w'][l], sg['ssd_conv_b'][l] = dcw, dcb
        lg, lb = vec('conv_ln_g', l), vec('conv_ln_b', l)
        dca, dlg, dlb = _lnsilu_bwd_call(s['ca'], lg, lb, dya_in, tr)
        sg['conv_ln_g'][l], sg['conv_ln_b'][l] = dlg, dlb
        dseg['a'], dseg['ag'], dww, dwb = _conv_bwd([(pr['a'], 0), (pr['ag'], 0)], full['conv_dw_w'][l], vec('conv_dw_b', l),
                                                    [(dca, 0, D_MODEL // LANES)], kk=CONV_KERNEL, pre_glu=True, post_silu=False, name="d_conv31", dx_dtype=BF16)
        sg['conv_dw_w'][l], sg['conv_dw_b'][l] = dww, dwb
        du = None
        dw_parts = []
        for k in seg_order:
            du = _mm(dseg[k], (w_seg[k], 'plain', l), mode='nt', add=du, name="d_u_" + k)
            dw_k = _mm(s['u'], dseg[k], mode='tn', out_dtype=BF16, name="dw_in_" + k)
            dw_parts.append(dw_k[:, :SSD_HEADS] if k == 'dt' else dw_k)
        dw_in = jnp.concatenate(dw_parts, axis=-1)
        dw_in = jnp.transpose(dw_in.reshape(D_MODEL, N_DEV, IN_COLS // N_DEV), (1, 0, 2))
        dh, dh_b, dg = _rms_bwd_call(s['h0'], vec('norm_mix', l), du, dh, tr, "d_rms_mix")
        sg['norm_mix'][l] = dg
        left_over = [('w_in', l, dw_in)]

    got = _exchange([a for _, _, a in left_over], name="exchange_grads")
    recv.update({(n, ll): r for (n, ll, _), r in zip(left_over, got)})
    grad_x = dh.reshape(x.shape)
    d_rel = jnp.transpose(_band_bias_bwd(jnp.stack(dbias_layers).reshape(DEPTH, ATTN_HEADS, -1), buckets))

    small_full = {'rel_table': d_rel}
    for n in SMALL:
        if n != 'rel_table':
            small_full[n] = jnp.stack(sg[n]).reshape((DEPTH,) + (full[n].shape[1:] if n in SMALL_SHARDED else p[n].shape[1:]))
    small_shapes = [(1, LANES)] + [small_full[n].shape for n in SMALL]
    packed = _pack([loss_part] + [small_full[n] for n in SMALL])
    slots = _all_gather([packed], [0], name="gather_small_grads")[0]
    rows = packed.shape[0]
    reduced = _rowwise(_f_sum_slots, [(slots, 0, LANES)], [], [(LANES, F32)], tr=_row_tile(rows, LANES, 12), name="sum_small")[0]
    red = _unpack(reduced, small_shapes)
    loss = red[0][0, 0]
    small_grad = {}
    for n, g_ in zip(SMALL, red[1:]):
        if n in SMALL_SHARDED:
            wdt = p[n].shape[-1]
            g_ = lax.dynamic_slice_in_dim(g_, my * wdt, wdt, axis=g_.ndim - 1)
        small_grad[n] = g_
    local_shapes = [p[n].shape for n in SMALL]
    pk = lambda d: _pack([d[n] for n in SMALL])
    pg_, pw_, pm_, pv_ = pk(small_grad), pk(p), pk(m), pk(v)
    srows = pg_.shape[0]
    sd, sm, sv = _rowwise(_f_adam, [(pg_, 0, LANES), (pw_, 0, LANES), (pm_, 0, LANES), (pv_, 0, LANES)], [],
                          [(LANES, F32)] * 3, tr=_row_tile(srows, LANES, 16), name="adam_small")
    out_delta = dict(zip(SMALL, _unpack(sd, local_shapes)))
    out_m = dict(zip(SMALL, _unpack(sm, local_shapes)))
    out_v = dict(zip(SMALL, _unpack(sv, local_shapes)))
    out_grad = dict(small_grad)

    per_layer = {n: [] for n in big_names}
    for l in range(DEPTH):
        for n in big_names:
            r = recv[(n, l)]
            rws, wdt = r.shape[1], r.shape[2]
            tr_w = _row_tile(rws, wdt, 24)
            outs = _rowwise(_f_adam_slots, [(r, 0, wdt), (p[n], 0, wdt, l), (m[n], 0, wdt, l), (v[n], 0, wdt, l)], [],
                            [(wdt, F32)] * 4, tr=tr_w, name="adam_" + n)
            per_layer[n].append(outs)
    for n in big_names:
        for k, dst in enumerate((out_grad, out_delta, out_m, out_v)):
            dst[n] = jnp.stack([per_layer[n][l][k] for l in range(DEPTH)])
    return (loss, grad_x, *[out_grad[n] for n in WEIGHTS], *[out_delta[n] for n in WEIGHTS],
            *[out_m[n] for n in WEIGHTS], *[out_v[n] for n in WEIGHTS])


def _rms_bwd_call(h, g, du, dres, tr, name):
    return _rowwise(lambda a, d, r, gg: _f_rms_bwd(a, gg, d, r), [(h, 0, D_MODEL), (du, 0, D_MODEL), (dres, 0, D_MODEL)], [g],
                    [(D_MODEL, F32), (D_MODEL, BF16)], [g.shape], tr=tr, name=name)


def _xattn_bwd_call(xq, kv, qg, kg, dxo, tr):
    def f(q, d, kvv, qgv, kgv):
        return _vjp_rows(lambda a, b, c, e: _f_xattn(a, b, c, e), 4, 1)(q, kvv, qgv, kgv, d)
    return _rowwise(f, [(xq, 0, D_MODEL), (dxo, 0, D_MODEL)], [kv, qg, kg], [(D_MODEL, BF16)], [kv.shape, qg.shape, kg.shape], tr=tr, name="d_xattn")


def _merge_bwd_call(pg, ya, yb, yc, gb, dmerged, tr):
    def f(a, b, c, e, d, gbv):
        return _vjp_rows(_f_merge, 5, 4)(a, b, c, e, gbv, d)
    return _rowwise(f, [(pg, 0, 3 * D_MODEL), (ya, 0, D_MODEL), (yb, 0, D_MODEL), (yc, 0, D_MODEL), (dmerged, 0, D_MODEL)], [gb],
                    [(3 * D_MODEL, BF16)] + [(D_MODEL, BF16)] * 3, [gb.shape], tr=tr, name="d_merge")


def _ssdgate_bwd_call(y, z, ng, dy, tr):
    def f(a, b, d, g):
        return _vjp_rows(_f_ssdgate, 3, 2)(a, b, g, d)
    return _rowwise(f, [(y, 0, SSD_INNER), (z, 0, SSD_INNER), (dy, 0, SSD_INNER)], [ng], [(SSD_INNER, F32), (SSD_INNER, BF16)], [ng.shape], tr=tr, name="d_ssd_gate")


def _lnsilu_bwd_call(ca, lg, lb, dy, tr):
    def f(a, d, g, b):
        return _vjp_rows(_f_lnsilu, 3, 1)(a, g, b, d)
    return _rowwise(f, [(ca, 0, D_MODEL), (dy, 0, D_MODEL)], [lg, lb], [(D_MODEL, F32)], [lg.shape, lb.shape], tr=tr, name="d_ln_silu")


def kernel(x, mem, rel_table, norm_mix, w_in, gate_bias, conv_dw_w, conv_dw_b, conv_ln_g, conv_ln_b, w_conv_out, ssd_conv_w, ssd_conv_b, ssd_dt_bias, ssd_A_log, ssd_D, ssd_norm_g, w_ssd_out, attn_q_norm, attn_k_norm, attn_sinks, w_attn_out, w_mix_out, norm_xattn, norm_mem, w_xq, w_xkv, xattn_q_norm, xattn_k_norm, w_xo, norm_mlp, w_mlp_up, w_mlp_down, loss_target, m_rel_table, m_norm_mix, m_w_in, m_gate_bias, m_conv_dw_w, m_conv_dw_b, m_conv_ln_g, m_conv_ln_b, m_w_conv_out, m_ssd_conv_w, m_ssd_conv_b, m_ssd_dt_bias, m_ssd_A_log, m_ssd_D, m_ssd_norm_g, m_w_ssd_out, m_attn_q_norm, m_attn_k_norm, m_attn_sinks, m_w_attn_out, m_w_mix_out, m_norm_xattn, m_norm_mem, m_w_xq, m_w_xkv, m_xattn_q_norm, m_xattn_k_norm, m_w_xo, m_norm_mlp, m_w_mlp_up, m_w_mlp_down, v_rel_table, v_norm_mix, v_w_in, v_gate_bias, v_conv_dw_w, v_conv_dw_b, v_conv_ln_g, v_conv_ln_b, v_w_conv_out, v_ssd_conv_w, v_ssd_conv_b, v_ssd_dt_bias, v_ssd_A_log, v_ssd_D, v_ssd_norm_g, v_w_ssd_out, v_attn_q_norm, v_attn_k_norm, v_attn_sinks, v_w_attn_out, v_w_mix_out, v_norm_xattn, v_norm_mem, v_w_xq, v_w_xkv, v_xattn_q_norm, v_xattn_k_norm, v_w_xo, v_norm_mlp, v_w_mlp_up, v_w_mlp_down):
    args = locals()
    p = {n: args[n] for n in WEIGHTS}
    m = {n: args["m_" + n] for n in WEIGHTS}
    v = {n: args["v_" + n] for n in WEIGHTS}
    return _step(p, m, v, x, mem, loss_target)
```

```python
import functools
import math

import numpy as np
import jax
import jax.numpy as jnp
from jax import lax
from jax.experimental import pallas as pl
from jax.experimental.pallas import tpu as pltpu

F32 = jnp.float32
BF16 = jnp.bfloat16
HI = lax.Precision.HIGHEST
MESH = pl.DeviceIdType.MESH

N_DEV = 8
D_MODEL = 1024
DEPTH = 4
MEM_LEN = 256
EPS = 1e-6
NEG_INF = -1e30
CONV_KERNEL = 31
SSD_INNER = 2048
SSD_HEAD_DIM = 64
SSD_HEADS = 32
SSD_GROUPS = 4
SSD_STATE = 128
SSD_CONV = 4
SSD_CHUNK = 128
SSD_XBC = SSD_INNER + 2 * SSD_GROUPS * SSD_STATE
HEADS_PER_GROUP = SSD_HEADS // SSD_GROUPS
ATTN_HEADS = 16
ATTN_KV_HEADS = 4
ATTN_HEAD_DIM = 64
ATTN_BLOCK = 128
ATTN_REP = ATTN_HEADS // ATTN_KV_HEADS
REL_BUCKETS = 32
REL_MAX_DIST = 128
XATTN_HEADS = 4
XATTN_HEAD_DIM = 256
MLP_HIDDEN = 4096
OFF_Z = 2048
OFF_XBC = 4096
OFF_DT = 7168
OFF_Q = 7200
OFF_GATE = 8736
IN_COLS = 11808
LANES = 128
CONV_PAD = 32
MM_VMEM_BUDGET = 20 << 20

ADAM_LR, ADAM_B1, ADAM_B2, ADAM_EPS, ADAM_WD, ADAM_STEP = 0.001, 0.9, 0.999, 1e-08, 0.01, 10

WEIGHTS = ['rel_table', 'norm_mix', 'w_in', 'gate_bias', 'conv_dw_w', 'conv_dw_b', 'conv_ln_g', 'conv_ln_b', 'w_conv_out',
           'ssd_conv_w', 'ssd_conv_b', 'ssd_dt_bias', 'ssd_A_log', 'ssd_D', 'ssd_norm_g', 'w_ssd_out', 'attn_q_norm',
           'attn_k_norm', 'attn_sinks', 'w_attn_out', 'w_mix_out', 'norm_xattn', 'norm_mem', 'w_xq', 'w_xkv', 'xattn_q_norm',
           'xattn_k_norm', 'w_xo', 'norm_mlp', 'w_mlp_up', 'w_mlp_down']
ROW_SHARDED = ['w_conv_out', 'w_ssd_out', 'w_attn_out', 'w_mix_out', 'w_xq', 'w_xo', 'w_mlp_down']
COL_SHARDED = ['w_in', 'w_xkv', 'w_mlp_up']
BIG = ROW_SHARDED + COL_SHARDED
SMALL_SHARDED = ['gate_bias', 'conv_dw_w', 'ssd_conv_w']
SMALL = [n for n in WEIGHTS if n not in BIG]


def _bdot(a, b, dims):
    return lax.dot_general(a.astype(BF16), b.astype(BF16), (dims, ((), ())), preferred_element_type=F32)


def _hdot(a, b, dims):
    return lax.dot_general(a, b, (dims, ((), ())), precision=HI, preferred_element_type=F32)


def _dot3(x, s, dims, s_first=False):
    hi = x.astype(jnp.bfloat16)
    rest = x - hi.astype(F32)
    mid = rest.astype(jnp.bfloat16)
    lo = (rest - mid.astype(F32)).astype(jnp.bfloat16)
    dot = lambda piece: lax.dot_general(*((s, piece) if s_first else (piece, s)), (dims, ((), ())), preferred_element_type=F32)
    return dot(hi) + dot(mid) + dot(lo)


@jax.custom_vjp
def _select_dot(s, x):
    return _dot3(x, s, ((1,), (0,)), s_first=True)


def _select_dot_fwd(s, x):
    return _select_dot(s, x), s


def _select_dot_bwd(s, g):
    return jnp.zeros_like(s), _dot3(g, s, ((0,), (0,)), s_first=True)


_select_dot.defvjp(_select_dot_fwd, _select_dot_bwd)


@jax.custom_vjp
def _spread_dot(x, s):
    return _dot3(x, s, ((1,), (0,)))


def _spread_dot_fwd(x, s):
    return _spread_dot(x, s), s


def _spread_dot_bwd(s, g):
    return _dot3(g, s, ((1,), (1,))), jnp.zeros_like(s)


_spread_dot.defvjp(_spread_dot_fwd, _spread_dot_bwd)


def _one_hot_groups(rows, cols, group, transpose=False):
    r = lax.broadcasted_iota(jnp.int32, (rows, cols), 0)
    c = lax.broadcasted_iota(jnp.int32, (rows, cols), 1)
    shift = int(math.log2(group))
    hit = (lax.shift_right_logical(r, shift) == c) if transpose else (r == lax.shift_right_logical(c, shift))
    return hit.astype(jnp.bfloat16)


def _heads_rms(x, g):
    w = x.shape[-1]
    nh = w // ATTN_HEAD_DIM
    ss = _spread_dot(x * x, _one_hot_groups(w, nh, ATTN_HEAD_DIM, transpose=True))
    scale = lax.rsqrt(ss * (1.0 / ATTN_HEAD_DIM) + EPS)
    y = x * _spread_dot(scale, _one_hot_groups(nh, w, ATTN_HEAD_DIM)) * jnp.concatenate([g] * nh, axis=-1)
    return [y[:, h * ATTN_HEAD_DIM:(h + 1) * ATTN_HEAD_DIM] for h in range(nh)]


def _pick(dim, pref):
    if dim <= pref:
        return dim
    t = (pref // LANES) * LANES
    while dim % t:
        t -= LANES
    return t


def _logical(op):
    arr, kind, _ = op
    r, c = arr.shape[-2:]
    return (r, c * N_DEV) if kind == 'col' else (r, c)


def _opspec(op, br, bc, rc):
    arr, kind, layer = op
    lead = () if layer is None else (layer,)
    none = (None,) * len(lead)
    if kind == 'plain':
        return pl.BlockSpec(none + (br, bc), lambda i, j, k: lead + rc(i, j, k))
    per = arr.shape[-1] // bc

    def imap(i, j, k):
        r, c = rc(i, j, k)
        if per == 1:
            return lead + (c, r, 0)
        return lead + (lax.div(c, per), r, lax.rem(c, per))
    return pl.BlockSpec(none + (None, br, bc), imap)


def _mm(a, b, *, mode, name, add=None, out_dtype=F32, out_col=False, epi=None, tm=2048, tn=512, tk=1024):
    def operand(op):
        op = op if isinstance(op, tuple) else (op, 'plain', None)
        return (op[0][op[2]], op[1], None) if isinstance(op[0], list) else op

    a, b = operand(a), operand(b)
    ar, ac = _logical(a)
    br_, bc_ = _logical(b)
    if mode == 'nn':
        m, kd, n = ar, ac, bc_
        assert br_ == kd
    elif mode == 'nt':
        m, kd, n = ar, ac, br_
        assert bc_ == kd
    else:
        m, kd, n = ac, ar, bc_
        assert br_ == kd

    def lim(op, is_col_dim):
        return op[0].shape[-1] if (op[1] == 'col' and is_col_dim) else 1 << 30

    tm = _pick(m, min(tm, lim(a, mode == 'tn')))
    tn = _pick(n, min(tn, lim(b, mode != 'nt'), (n // N_DEV) if out_col else 1 << 30))
    tk = _pick(kd, min(tk, lim(a, mode != 'tn'), lim(b, mode == 'nt')))
    nk = kd // tk

    def vmem_bytes(tm_):
        out_bytes = sum(jnp.dtype(dt).itemsize for dt in (epi[2] if epi is not None else [out_dtype]))
        extra_bytes = sum(e.dtype.itemsize for e in (epi[1] if epi is not None else [])) + (add.dtype.itemsize if add is not None else 0)
        blocks = tm_ * tk * a[0].dtype.itemsize + tk * tn * b[0].dtype.itemsize + tm_ * tn * (out_bytes + extra_bytes)
        return 2 * blocks + tm_ * tn * 4 * (2 if nk > 1 else 1)

    while vmem_bytes(tm) > MM_VMEM_BUDGET and tm % 256 == 0 and tm > 256:
        tm //= 2
    if mode == 'nn':
        a_spec = _opspec(a, tm, tk, lambda i, j, k: (i, k))
        b_spec = _opspec(b, tk, tn, lambda i, j, k: (k, j))
        dims = ((1,), (0,))
    elif mode == 'nt':
        a_spec = _opspec(a, tm, tk, lambda i, j, k: (i, k))
        b_spec = _opspec(b, tn, tk, lambda i, j, k: (j, k))
        dims = ((1,), (1,))
    else:
        a_spec = _opspec(a, tk, tm, lambda i, j, k: (k, i))
        b_spec = _opspec(b, tk, tn, lambda i, j, k: (k, j))
        dims = ((0,), (0,))
    if out_col:
        out_shape = jax.ShapeDtypeStruct((N_DEV, m, n // N_DEV), out_dtype)
        out_spec = _opspec((out_shape, 'col', None), tm, tn, lambda i, j, k: (i, j))
    else:
        out_shape = jax.ShapeDtypeStruct((m, n), out_dtype)
        out_spec = pl.BlockSpec((tm, tn), lambda i, j, k: (i, j))
    has_add = add is not None
    epi_fn, epi_extra, epi_dtypes = epi if epi is not None else (None, [], [out_dtype])
    n_in = 2 + has_add + len(epi_extra)
    n_out = len(epi_dtypes)

    def body(*refs):
        a_ref, b_ref = refs[0], refs[1]
        add_ref = refs[2] if has_add else None
        extra_refs = refs[2 + has_add:n_in]
        o_refs = refs[n_in:n_in + n_out]

        def emit(acc):
            outs = (acc,) if epi_fn is None else epi_fn(acc, *[r[...] for r in extra_refs])
            for o_ref, o in zip(o_refs, outs):
                o_ref[...] = o.astype(o_ref.dtype)

        part = _bdot(a_ref[...], b_ref[...], dims)
        if nk == 1:
            emit(part + add_ref[...].astype(F32) if has_add else part)
            return
        acc_ref = refs[n_in + n_out]
        k = pl.program_id(2)

        @pl.when(k == 0)
        def _():
            acc_ref[...] = part + add_ref[...].astype(F32) if has_add else part

        @pl.when(k > 0)
        def _():
            acc_ref[...] += part

        @pl.when(k == nk - 1)
        def _():
            emit(acc_ref[...])

    block = pl.BlockSpec((tm, tn), lambda i, j, k: (i, j))
    in_specs = [a_spec, b_spec] + [block] * (has_add + len(epi_extra))
    args = [a[0], b[0]] + ([add] if has_add else []) + list(epi_extra)
    if epi is not None:
        assert not out_col
        out_spec = [block] * n_out
        out_shape = [jax.ShapeDtypeStruct((m, n), dt) for dt in epi_dtypes]
    return pl.pallas_call(
        body, name=name, grid=(m // tm, n // tn, nk), in_specs=in_specs, out_specs=out_spec, out_shape=out_shape,
        scratch_shapes=[pltpu.VMEM((tm, tn), F32)] if nk > 1 else [],
        compiler_params=pltpu.CompilerParams(dimension_semantics=("parallel", "parallel", "arbitrary")),
    )(*args)


def _rowwise(f, rows, consts, row_outs, acc_outs=(), *, tr, name):
    nr, nc, nro = len(rows), len(consts), len(row_outs)
    first = rows[0][0]
    t = first.shape[-2]
    assert t % tr == 0
    in_specs = []
    for spec in rows:
        arr, cb, w = spec[:3]
        lead = spec[3] if len(spec) > 3 else None
        if arr.ndim == 2:
            in_specs.append(pl.BlockSpec((tr, w), functools.partial(lambda i, cb: (i, cb), cb=cb)))
        elif lead is not None:
            in_specs.append(pl.BlockSpec((None, tr, w), functools.partial(lambda i, cb, lead: (lead, i, cb), cb=cb, lead=lead)))
        else:
            in_specs.append(pl.BlockSpec((arr.shape[0], tr, w), functools.partial(lambda i, cb: (0, i, cb), cb=cb)))
    for cst in consts:
        in_specs.append(pl.BlockSpec(cst.shape, functools.partial(lambda i, nd: (0,) * nd, nd=cst.ndim)))
    out_specs = [pl.BlockSpec((tr, w), lambda i: (i, 0)) for w, _ in row_outs]
    out_shape = [jax.ShapeDtypeStruct((t, w), dt) for w, dt in row_outs]
    for shp in acc_outs:
        out_specs.append(pl.BlockSpec(shp, functools.partial(lambda i, nd: (0,) * nd, nd=len(shp))))
        out_shape.append(jax.ShapeDtypeStruct(shp, F32))

    def body(*refs):
        ins = [r[...] for r in refs[:nr + nc]]
        ro = refs[nr + nc:nr + nc + nro]
        ao = refs[nr + nc + nro:]
        outs, accs = f(*ins)
        for o_ref, o in zip(ro, outs):
            o_ref[...] = o.astype(o_ref.dtype)
        if ao:
            i = pl.program_id(0)

            @pl.when(i == 0)
            def _():
                for a_ref, acc in zip(ao, accs):
                    a_ref[...] = acc

            @pl.when(i > 0)
            def _():
                for a_ref, acc in zip(ao, accs):
                    a_ref[...] += acc

    res = pl.pallas_call(
        body, name=name, grid=(t // tr,), in_specs=in_specs, out_specs=out_specs, out_shape=out_shape,
        compiler_params=pltpu.CompilerParams(dimension_semantics=("arbitrary",)),
    )(*[s[0] for s in rows], *consts)
    return res


def _vjp_rows(f, n_prim, n_rows_grad):
    def g(*args):
        prim, cots = args[:n_prim], args[n_prim:]
        outs, vjp = jax.vjp(f, *prim)
        grads = vjp(tuple(c.astype(o.dtype) for c, o in zip(cots, outs)))
        return tuple(grads[:n_rows_grad]), tuple(grads[n_rows_grad:])
    return g


def _rms(x, g):
    return x * lax.rsqrt(jnp.mean(x * x, axis=-1, keepdims=True) + EPS) * g


def _f_rms(h, g):
    return (_rms(h, g),)


def _f_rms_bwd(h, g, du, dres):
    _, vjp = jax.vjp(_f_rms, h, g)
    dh, dg = vjp((du.astype(F32),))
    return (dh + dres, dh + dres), (dg,)


def _f_lnsilu(x, g, b):
    mu = jnp.mean(x, axis=-1, keepdims=True)
    xc = x - mu
    y = xc * lax.rsqrt(jnp.mean(xc * xc, axis=-1, keepdims=True) + EPS) * g + b
    return (jax.nn.silu(y),)


def _f_ssdgate(y, z, g):
    y = y * jax.nn.silu(z)
    w = SSD_INNER // SSD_GROUPS
    return (jnp.concatenate([_rms(y[:, i * w:(i + 1) * w], g[:, i * w:(i + 1) * w]) for i in range(SSD_GROUPS)], axis=-1),)


def _f_merge(pg, ya, yb, yc, gb):
    out = 0.0
    for i, yi in enumerate((ya, yb, yc)):
        out = out + jax.nn.sigmoid(pg[:, i * D_MODEL:(i + 1) * D_MODEL] + gb[i:i + 1, :]) * yi
    return (out,)


def _f_relu2(a):
    return (jnp.square(jnp.maximum(a, 0.0)),)


def _f_xattn(q, kv, qg, kg):
    outs = []
    for h in range(XATTN_HEADS):
        sl = slice(h * XATTN_HEAD_DIM, (h + 1) * XATTN_HEAD_DIM)
        qh = _rms(q[:, sl], qg)
        kh = _rms(kv[:, sl], kg)
        vh = kv[:, D_MODEL + h * XATTN_HEAD_DIM:D_MODEL + (h + 1) * XATTN_HEAD_DIM]
        s = _bdot(qh, kh, ((1,), (1,))) * (XATTN_HEAD_DIM ** -0.5)
        p = jnp.exp(s - jnp.max(s, axis=-1, keepdims=True))
        p = p * (1.0 / jnp.sum(p, axis=-1, keepdims=True))
        outs.append(_bdot(p, vh, ((1,), (0,))))
    return (jnp.concatenate(outs, axis=-1),)


def _f_loss(y, tgt):
    err = y - tgt
    per_row = jnp.sum(err * err, axis=-1, keepdims=True) * (0.5 / D_MODEL)
    loss = jnp.sum(per_row, axis=0, keepdims=True)
    dy = err * (1.0 / D_MODEL)
    return (dy, dy), (jnp.broadcast_to(loss, (1, LANES)),)


def _adam_core(w, g, m, v):
    m = ADAM_B1 * m + (1.0 - ADAM_B1) * g
    v = ADAM_B2 * v + (1.0 - ADAM_B2) * jnp.square(g)
    m_hat = m / (1.0 - ADAM_B1 ** ADAM_STEP)
    v_hat = v / (1.0 - ADAM_B2 ** ADAM_STEP)
    delta = -ADAM_LR * (m_hat / (jnp.sqrt(v_hat) + ADAM_EPS) + ADAM_WD * w)
    return delta, m, v


def _sum_slots(g8):
    g = g8[0].astype(F32)
    for s in range(1, N_DEV):
        g = g + g8[s].astype(F32)
    return g


def _f_adam_slots(g8, w, m, v):
    g = _sum_slots(g8)
    return (g,) + _adam_core(w, g, m, v), ()


def _f_sum_slots(g8):
    return (_sum_slots(g8),), ()


def _f_adam(g, w, m, v):
    return _adam_core(w, g, m, v), ()


def _conv_chunk(t):
    return 256 if t % 256 == 0 else t


def _conv_fwd(srcs, w, b, *, kk, pre_glu, post_silu, name):
    t = srcs[0][0].shape[0]
    c = w.shape[-1]
    tt = _conv_chunk(t)
    ns = len(srcs)

    def body(*refs):
        w_ref, b_ref, o_ref, pad_ref = refs[ns:]
        if pre_glu:
            xin = refs[0][...] * jax.nn.sigmoid(refs[1][...])
        else:
            xin = refs[0][...]
        pad_ref[0:CONV_PAD, :] = jnp.zeros((CONV_PAD, LANES), F32)
        pad_ref[CONV_PAD:, :] = xin

        def chunk(i, carry):
            base = pl.multiple_of(i * tt, tt)
            acc = jnp.broadcast_to(b_ref[...], (tt, LANES))
            for j in range(kk):
                acc = acc + pad_ref[pl.ds(base + CONV_PAD - (kk - 1) + j, tt), :] * w_ref[j:j + 1, :]
            o_ref[pl.ds(base, tt), :] = jax.nn.silu(acc) if post_silu else acc
            return carry
        lax.fori_loop(0, t // tt, chunk, 0)

    in_specs = [pl.BlockSpec((t, LANES), functools.partial(lambda i, off: (0, off + i), off=off)) for _, off in srcs]
    in_specs += [pl.BlockSpec((kk, LANES), lambda i: (0, i)), pl.BlockSpec((1, LANES), lambda i: (0, i))]
    return pl.pallas_call(
        body, name=name, grid=(c // LANES,), in_specs=in_specs, out_specs=pl.BlockSpec((t, LANES), lambda i: (0, i)),
        out_shape=jax.ShapeDtypeStruct((t, c), F32), scratch_shapes=[pltpu.VMEM((t + CONV_PAD, LANES), F32)],
        compiler_params=pltpu.CompilerParams(dimension_semantics=("parallel",)),
    )(*[s[0] for s in srcs], w, b)


def _conv_bwd(srcs, w, b, dys, *, kk, pre_glu, post_silu, name, dx_dtype):
    t = srcs[0][0].shape[0]
    c = w.shape[-1]
    tt = _conv_chunk(t)
    ns, nd = len(srcs), len(dys)

    def body(*refs):
        src_refs = refs[:ns]
        dy_refs = refs[ns:ns + nd]
        w_ref, b_ref = refs[ns + nd:ns + nd + 2]
        outs = refs[ns + nd + 2:]
        dx_refs, dw_ref, db_ref = outs[:ns], outs[ns], outs[ns + 1]
        pad_ref, dpad_ref = outs[ns + 2:]
        cb = pl.program_id(0)
        if pre_glu:
            a_in = src_refs[0][...]
            sg = jax.nn.sigmoid(src_refs[1][...])
            xin = a_in * sg
        else:
            xin = src_refs[0][...]
        pad_ref[0:CONV_PAD, :] = jnp.zeros((CONV_PAD, LANES), F32)
        pad_ref[CONV_PAD:, :] = xin
        dpad_ref[t:, :] = jnp.zeros((CONV_PAD, LANES), F32)
        dw_ref[...] = jnp.zeros_like(dw_ref)
        db_ref[...] = jnp.zeros_like(db_ref)

        def load_dy(base):
            dy = dy_refs[0][pl.ds(base, tt), :].astype(F32)
            for (_, first, _n), r in zip(dys[1:], dy_refs[1:]):
                dy = jnp.where(cb >= first, r[pl.ds(base, tt), :].astype(F32), dy)
            return dy

        def chunk1(i, carry):
            base = pl.multiple_of(i * tt, tt)
            dy = load_dy(base)
            if post_silu:
                acc = jnp.broadcast_to(b_ref[...], (tt, LANES))
                for j in range(kk):
                    acc = acc + pad_ref[pl.ds(base + CONV_PAD - (kk - 1) + j, tt), :] * w_ref[j:j + 1, :]
                s = jax.nn.sigmoid(acc)
                dy = dy * (s * (1.0 + acc * (1.0 - s)))
            dpad_ref[pl.ds(base, tt), :] = dy
            db_ref[...] += jnp.sum(dy, axis=0, keepdims=True)
            for j in range(kk):
                dw_ref[j:j + 1, :] += jnp.sum(dy * pad_ref[pl.ds(base + CONV_PAD - (kk - 1) + j, tt), :], axis=0, keepdims=True)
            return carry
        lax.fori_loop(0, t // tt, chunk1, 0)

        def chunk2(i, carry):
            base = pl.multiple_of(i * tt, tt)
            acc = jnp.zeros((tt, LANES), F32)
            for j in range(kk):
                acc = acc + dpad_ref[pl.ds(base + (kk - 1) - j, tt), :] * w_ref[j:j + 1, :]
            if pre_glu:
                a_c = src_refs[0][pl.ds(base, tt), :]
                s_c = jax.nn.sigmoid(src_refs[1][pl.ds(base, tt), :])
                dx_refs[0][pl.ds(base, tt), :] = (acc * s_c).astype(dx_dtype)
                dx_refs[1][pl.ds(base, tt), :] = (acc * a_c * s_c * (1.0 - s_c)).astype(dx_dtype)
            else:
                dx_refs[0][pl.ds(base, tt), :] = acc.astype(dx_dtype)
            return carry
        lax.fori_loop(0, t // tt, chunk2, 0)

    in_specs = [pl.BlockSpec((t, LANES), functools.partial(lambda i, off: (0, off + i), off=off)) for _, off in srcs]
    for _, first, n in dys:
        in_specs.append(pl.BlockSpec((t, LANES), functools.partial(lambda i, first, n: (0, jnp.clip(i - first, 0, n - 1)), first=first, n=n)))
    in_specs += [pl.BlockSpec((kk, LANES), lambda i: (0, i)), pl.BlockSpec((1, LANES), lambda i: (0, i))]
    out_specs = [pl.BlockSpec((t, LANES), lambda i: (0, i)) for _ in srcs]
    out_specs += [pl.BlockSpec((kk, LANES), lambda i: (0, i)), pl.BlockSpec((1, LANES), lambda i: (0, i))]
    out_shape = [jax.ShapeDtypeStruct((t, c), dx_dtype) for _ in srcs]
    out_shape += [jax.ShapeDtypeStruct((kk, c), F32), jax.ShapeDtypeStruct((1, c), F32)]
    return pl.pallas_call(
        body, name=name, grid=(c // LANES,), in_specs=in_specs, out_specs=out_specs, out_shape=out_shape,
        scratch_shapes=[pltpu.VMEM((t + CONV_PAD, LANES), F32), pltpu.VMEM((t + CONV_PAD, LANES), F32)],
        compiler_params=pltpu.CompilerParams(dimension_semantics=("parallel",)),
    )(*[s[0] for s in srcs], *[d[0] for d in dys], w, b)


def _ssd_chunk(xbc, dtfull, s_in, dt_bias, a_log, dskip):
    q = xbc.shape[0]
    gw = SSD_INNER // SSD_GROUPS
    x = xbc[:, :SSD_INNER]
    dt_all = jax.nn.softplus(dtfull + dt_bias)
    da_all = dt_all * (-jnp.exp(a_log))
    row = lax.broadcasted_iota(jnp.int32, (q, q), 0)
    col = lax.broadcasted_iota(jnp.int32, (q, q), 1)
    causal = row >= col
    cs = _select_dot(causal.astype(jnp.bfloat16), da_all)
    cs_last = cs[q - 1:q, :]
    per_head = jnp.concatenate([dt_all, jnp.exp(cs), jnp.exp(cs_last - cs), jnp.broadcast_to(dskip, (8, LANES))], axis=0)
    wide = _spread_dot(per_head, _one_hot_groups(LANES, SSD_INNER, SSD_HEAD_DIM))
    xdt = x * wide[0:q]
    from_start, to_end, d_wide = wide[q:2 * q], wide[2 * q:3 * q], wide[3 * q:3 * q + 1]
    xdt_end = xdt * to_end
    chunk_decay = jnp.exp(cs_last)
    decay_rows = jnp.concatenate([jnp.broadcast_to(chunk_decay[:, h:h + 1], (SSD_HEAD_DIM, 1)) for h in range(SSD_HEADS)], axis=0)
    cs_t = jnp.transpose(cs)
    y_off, new_states, y_diag = [], [], []
    for g in range(SSD_GROUPS):
        bm = xbc[:, SSD_INNER + g * SSD_STATE:SSD_INNER + (g + 1) * SSD_STATE]
        cm = xbc[:, SSD_INNER + (SSD_GROUPS + g) * SSD_STATE:SSD_INNER + (SSD_GROUPS + g + 1) * SSD_STATE]
        cb = _bdot(cm, bm, ((1,), (1,)))
        y_off.append(_bdot(cm, s_in[g * gw:(g + 1) * gw, :], ((1,), (1,))))
        new_states.append(_bdot(xdt_end[:, g * gw:(g + 1) * gw], bm, ((0,), (0,))))
        for j in range(HEADS_PER_GROUP):
            h = g * HEADS_PER_GROUP + j
            diff = cs[:, h:h + 1] - cs_t[h:h + 1, :]
            decay = jnp.where(causal, jnp.exp(jnp.where(causal, diff, 0.0)), 0.0)
            y_diag.append(_bdot(cb * decay, xdt[:, h * SSD_HEAD_DIM:(h + 1) * SSD_HEAD_DIM], ((1,), (0,))))
    s_out = s_in * decay_rows + jnp.concatenate(new_states, axis=0)
    y = jnp.concatenate(y_diag, axis=-1) + jnp.concatenate(y_off, axis=-1) * from_start + x * d_wide
    return y, s_out


def _ssd_fwd(xbc, pdt, dt_bias, a_log, dskip, *, name, rider=None):
    t = xbc.shape[0]
    nc = t // SSD_CHUNK
    ride = _Ride(rider, 5, 2, 1)

    def body(*refs):
        (x_ref, dt_ref, tb_ref, al_ref, d_ref), (y_ref, s_ref), (state_ref,) = ride.split(refs)
        c = pl.program_id(0)
        ride.start(refs, c == 0)

        @pl.when(c == 0)
        def _():
            state_ref[...] = jnp.zeros_like(state_ref)
        s_in = state_ref[...]
        s_ref[...] = s_in
        y, s_out = _ssd_chunk(x_ref[...], dt_ref[...], s_in, tb_ref[...], al_ref[...], d_ref[...])
        y_ref[...] = y
        state_ref[...] = s_out
        ride.finish(refs, c == nc - 1)

    vec = pl.BlockSpec((1, LANES), lambda c: (0, 0))
    res = pl.pallas_call(
        body, name=name, grid=(nc,),
        in_specs=[pl.BlockSpec((SSD_CHUNK, SSD_XBC), lambda c: (c, 0)), pl.BlockSpec((SSD_CHUNK, LANES), lambda c: (c, 0)), vec, vec, vec] + ride.in_specs,
        out_specs=[pl.BlockSpec((SSD_CHUNK, SSD_INNER), lambda c: (c, 0)), pl.BlockSpec((None, SSD_INNER, SSD_STATE), lambda c: (c, 0, 0))] + ride.out_specs,
        out_shape=[jax.ShapeDtypeStruct((t, SSD_INNER), F32), jax.ShapeDtypeStruct((nc, SSD_INNER, SSD_STATE), F32)] + ride.out_shape,
        scratch_shapes=[pltpu.VMEM((SSD_INNER, SSD_STATE), F32)] + ride.scratch,
        compiler_params=pltpu.CompilerParams(dimension_semantics=("arbitrary",)),
    )(xbc, pdt, dt_bias, a_log, dskip, *ride.args)
    return res[:2], res[2:]


def _ssd_bwd(xbc, pdt, states, dy, dt_bias, a_log, dskip, *, name, rider=None):
    t = xbc.shape[0]
    nc = t // SSD_CHUNK
    rev = lambda c: nc - 1 - c
    ride = _Ride(rider, 7, 5, 1)

    def body(*refs):
        (x_ref, dt_ref, s_ref, dy_ref, tb_ref, al_ref, d_ref), (dx_ref, ddt_ref, dtb_ref, dal_ref, dd_ref), (dstate_ref,) = ride.split(refs)
        c = pl.program_id(0)
        ride.start(refs, c == 0)

        @pl.when(c == 0)
        def _():
            dstate_ref[...] = jnp.zeros_like(dstate_ref)
            dtb_ref[...] = jnp.zeros_like(dtb_ref)
            dal_ref[...] = jnp.zeros_like(dal_ref)
            dd_ref[...] = jnp.zeros_like(dd_ref)
        _, vjp = jax.vjp(_ssd_chunk, x_ref[...], dt_ref[...], s_ref[...], tb_ref[...], al_ref[...], d_ref[...])
        dx, ddt, ds_in, dtb, dal, dd = vjp((dy_ref[...].astype(F32), dstate_ref[...]))
        dx_ref[...] = dx
        ddt_ref[...] = ddt
        dstate_ref[...] = ds_in
        dtb_ref[...] += dtb
        dal_ref[...] += dal
        dd_ref[...] += dd
        ride.finish(refs, c == nc - 1)

    vec = pl.BlockSpec((1, LANES), lambda c: (0, 0))
    in_specs = [pl.BlockSpec((SSD_CHUNK, SSD_XBC), lambda c: (rev(c), 0)), pl.BlockSpec((SSD_CHUNK, LANES), lambda c: (rev(c), 0)),
                pl.BlockSpec((None, SSD_INNER, SSD_STATE), lambda c: (rev(c), 0, 0)), pl.BlockSpec((SSD_CHUNK, SSD_INNER), lambda c: (rev(c), 0)),
                vec, vec, vec]
    out_specs = [pl.BlockSpec((SSD_CHUNK, SSD_XBC), lambda c: (rev(c), 0)), pl.BlockSpec((SSD_CHUNK, LANES), lambda c: (rev(c), 0)), vec, vec, vec]
    out_shape = [jax.ShapeDtypeStruct((t, SSD_XBC), F32), jax.ShapeDtypeStruct((t, LANES), F32)] + [jax.ShapeDtypeStruct((1, LANES), F32)] * 3
    res = pl.pallas_call(
        body, name=name, grid=(nc,), in_specs=in_specs + ride.in_specs, out_specs=out_specs + ride.out_specs,
        out_shape=out_shape + ride.out_shape, scratch_shapes=[pltpu.VMEM((SSD_INNER, SSD_STATE), F32)] + ride.scratch,
        compiler_params=pltpu.CompilerParams(dimension_semantics=("arbitrary",)),
    )(xbc, pdt, states, dy, dt_bias, a_log, dskip, *ride.args)
    return res[:5], res[5:]


def _rel_buckets():
    qi = np.arange(ATTN_BLOCK)[:, None] + ATTN_BLOCK
    kj = np.arange(2 * ATTN_BLOCK)[None, :]
    dist = qi - kj
    max_exact = REL_BUCKETS // 2
    d = np.maximum(dist, 1).astype(np.float32)
    large = max_exact + (np.log(d / np.float32(max_exact)) / np.float32(math.log(REL_MAX_DIST / max_exact))
                         * np.float32(REL_BUCKETS - max_exact)).astype(np.int32)
    large = np.minimum(large, REL_BUCKETS - 1)
    return np.where(dist < max_exact, np.maximum(dist, 0), large).astype(np.int32)


def _onehot_buckets(bucket_ref):
    n = bucket_ref.shape[-1]
    return (lax.broadcasted_iota(jnp.int32, (REL_BUCKETS, n), 0) == bucket_ref[...]).astype(F32)


def _band_bias(rel_table_t, buckets):
    n = buckets.shape[-1]

    def body(rt_ref, bk_ref, o_ref):
        o_ref[...] = _hdot(rt_ref[...], _onehot_buckets(bk_ref), ((1,), (0,)))
    return pl.pallas_call(body, name="band_bias", out_shape=jax.ShapeDtypeStruct((ATTN_HEADS, n), F32))(rel_table_t, buckets)


def _band_bias_bwd(dbias, buckets):
    def body(db_ref, bk_ref, o_ref):
        d = db_ref[0]
        for layer in range(1, db_ref.shape[0]):
            d = d + db_ref[layer]
        o_ref[...] = _hdot(d, _onehot_buckets(bk_ref), ((1,), (1,)))
    return pl.pallas_call(body, name="band_bias_bwd", out_shape=jax.ShapeDtypeStruct((ATTN_HEADS, REL_BUCKETS), F32))(dbias, buckets)


def _heads_rms_lanes(x, g):
    return [_rms(x[:, h * ATTN_HEAD_DIM:(h + 1) * ATTN_HEAD_DIM], g) for h in range(x.shape[-1] // ATTN_HEAD_DIM)]


def _attn_block(q, kvp, kvc, bias, sinks, qg, kg, first, norm=_heads_rms):
    qn = q.shape[0]
    rows = ATTN_REP * qn
    ri = lax.broadcasted_iota(jnp.int32, (rows, 2 * qn), 0) & (qn - 1)
    cj = lax.broadcasted_iota(jnp.int32, (rows, 2 * qn), 1)
    jj = cj & (qn - 1)
    no_prev = jnp.where(first, qn, 0)
    mask = ((cj < qn) & (jj > ri + no_prev)) | ((cj >= qn) & (jj <= ri))
    kvd = ATTN_KV_HEADS * ATTN_HEAD_DIM
    q_normed = norm(q, qg)
    kn_prev, kn_cur = norm(kvp[:, :kvd], kg), norm(kvc[:, :kvd], kg)
    outs = []
    for g in range(ATTN_KV_HEADS):
        vsl = slice(kvd + g * ATTN_HEAD_DIM, kvd + (g + 1) * ATTN_HEAD_DIM)
        qs = jnp.concatenate(q_normed[g * ATTN_REP:(g + 1) * ATTN_REP], axis=0)
        kb = jnp.concatenate([kn_prev[g], kn_cur[g]], axis=0)
        vb = jnp.concatenate([kvp[:, vsl], kvc[:, vsl]], axis=0)
        logits = _bdot(qs, kb, ((1,), (1,))) * (ATTN_HEAD_DIM ** -0.5) + bias[g]
        logits = jnp.where(mask, logits, NEG_INF)
        sink = jnp.concatenate([jnp.broadcast_to(sinks[:, g * ATTN_REP + j:g * ATTN_REP + j + 1], (qn, 1)) for j in range(ATTN_REP)], axis=0)
        m = jnp.maximum(jnp.max(logits, axis=-1, keepdims=True), sink)
        pexp = jnp.exp(logits - m)
        probs = pexp * (1.0 / (jnp.sum(pexp, axis=-1, keepdims=True) + jnp.exp(sink - m)))
        o = _bdot(probs, vb, ((1,), (0,)))
        outs += [o[j * qn:(j + 1) * qn, :] for j in range(ATTN_REP)]
    return jnp.concatenate(outs, axis=-1)


def _attn_specs(nmap):
    qd, kvw = ATTN_HEADS * ATTN_HEAD_DIM, 2 * ATTN_KV_HEADS * ATTN_HEAD_DIM
    full = lambda shp: pl.BlockSpec(shp, lambda i: (0,) * len(shp))
    return [
        pl.BlockSpec((ATTN_BLOCK, qd), lambda i: (nmap(i), 0)),
        pl.BlockSpec((ATTN_BLOCK, kvw), lambda i: (jnp.maximum(nmap(i) - 1, 0), qd // kvw)),
        pl.BlockSpec((ATTN_BLOCK, kvw), lambda i: (nmap(i), qd // kvw)),
        full((ATTN_KV_HEADS, ATTN_REP * ATTN_BLOCK, 2 * ATTN_BLOCK)), full((1, ATTN_HEADS)), full((1, ATTN_HEAD_DIM)), full((1, ATTN_HEAD_DIM)),
    ]


def _attn_fwd(pqkv, bias, sinks, qg, kg, *, name, rider=None):
    t = pqkv.shape[0]
    nb = t // ATTN_BLOCK
    qd = ATTN_HEADS * ATTN_HEAD_DIM
    ride = _Ride(rider, 7, 1, 0)

    def body(*refs):
        (q_ref, kvp_ref, kvc_ref, bias_ref, sk_ref, qg_ref, kg_ref), (o_ref,), _ = ride.split(refs)
        i = pl.program_id(0)
        ride.start(refs, i == 0)
        o_ref[...] = _attn_block(q_ref[...], kvp_ref[...], kvc_ref[...], bias_ref[...], sk_ref[...], qg_ref[...], kg_ref[...], i == 0,
                                 norm=_heads_rms_lanes).astype(o_ref.dtype)
        ride.finish(refs, i == nb - 1)

    res = pl.pallas_call(
        body, name=name, grid=(nb,), in_specs=_attn_specs(lambda i: i) + ride.in_specs,
        out_specs=[pl.BlockSpec((ATTN_BLOCK, qd), lambda i: (i, 0))] + ride.out_specs,
        out_shape=[jax.ShapeDtypeStruct((t, qd), BF16)] + ride.out_shape, scratch_shapes=ride.scratch,
        compiler_params=pltpu.CompilerParams(dimension_semantics=("arbitrary",)),
    )(pqkv, pqkv, pqkv, bias, sinks, qg, kg, *ride.args)
    return res[0], res[1:]


def _attn_bwd(pqkv, do, bias, sinks, qg, kg, *, name, rider=None):
    t = pqkv.shape[0]
    nb = t // ATTN_BLOCK
    qd, kvw = ATTN_HEADS * ATTN_HEAD_DIM, 2 * ATTN_KV_HEADS * ATTN_HEAD_DIM
    rev = lambda i: nb - 1 - i
    ride = _Ride(rider, 8, 5, 1)

    def body(*refs):
        (q_ref, kvp_ref, kvc_ref, bias_ref, sk_ref, qg_ref, kg_ref, do_ref), (dqkv_ref, dbias_ref, dsk_ref, dqg_ref, dkg_ref), (carry_ref,) = ride.split(refs)
        i = pl.program_id(0)
        ride.start(refs, i == 0)
        first = rev(i) == 0
        f = functools.partial(_attn_block, first=first)
        _, vjp = jax.vjp(f, q_ref[...], kvp_ref[...], kvc_ref[...], bias_ref[...], sk_ref[...], qg_ref[...], kg_ref[...])
        dq, dkvp, dkvc, dbias, dsk, dqg, dkg = vjp(do_ref[...].astype(F32))

        @pl.when(i == 0)
        def _():
            carry_ref[...] = jnp.zeros_like(carry_ref)
            dbias_ref[...] = jnp.zeros_like(dbias_ref)
            dsk_ref[...] = jnp.zeros_like(dsk_ref)
            dqg_ref[...] = jnp.zeros_like(dqg_ref)
            dkg_ref[...] = jnp.zeros_like(dkg_ref)
        dqkv_ref[:, 0:qd] = dq.astype(dqkv_ref.dtype)
        dqkv_ref[:, qd:] = (dkvc + carry_ref[...]).astype(dqkv_ref.dtype)
        carry_ref[...] = dkvp
        dbias_ref[...] += dbias
        dsk_ref[...] += dsk
        dqg_ref[...] += dqg
        dkg_ref[...] += dkg
        ride.finish(refs, i == nb - 1)

    full = lambda shp: pl.BlockSpec(shp, lambda i: (0,) * len(shp))
    bshape = (ATTN_KV_HEADS, ATTN_REP * ATTN_BLOCK, 2 * ATTN_BLOCK)
    res = pl.pallas_call(
        body, name=name, grid=(nb,), in_specs=_attn_specs(rev) + [pl.BlockSpec((ATTN_BLOCK, qd), lambda i: (rev(i), 0))] + ride.in_specs,
        out_specs=[pl.BlockSpec((ATTN_BLOCK, qd + kvw), lambda i: (rev(i), 0)), full(bshape), full((1, ATTN_HEADS)),
                   full((1, ATTN_HEAD_DIM)), full((1, ATTN_HEAD_DIM))] + ride.out_specs,
        out_shape=[jax.ShapeDtypeStruct((t, qd + kvw), BF16), jax.ShapeDtypeStruct(bshape, F32), jax.ShapeDtypeStruct((1, ATTN_HEADS), F32),
                   jax.ShapeDtypeStruct((1, ATTN_HEAD_DIM), F32), jax.ShapeDtypeStruct((1, ATTN_HEAD_DIM), F32)] + ride.out_shape,
        scratch_shapes=[pltpu.VMEM((ATTN_BLOCK, kvw), F32)] + ride.scratch,
        compiler_params=pltpu.CompilerParams(dimension_semantics=("arbitrary",)),
    )(pqkv, pqkv, pqkv, bias, sinks, qg, kg, do, *ride.args)
    return res[:5], res[5:]


def _dev_index(dev):
    return 4 * dev[0] + 2 * dev[1] + dev[2]


def _gather_ops(x_refs, o_refs, send_sems, recv_sems, local_sems, axes):
    n = len(x_refs)
    x, y, c = lax.axis_index("x"), lax.axis_index("y"), lax.axis_index("c")
    me, sibling = (x, y, c), (x, y, 1 - c)
    chips = [(1 - x, y), (x, 1 - y), (1 - x, 1 - y)]

    def slot(i, dev):
        idx = _dev_index(dev)
        return o_refs[i].at[idx] if axes[i] == 0 else o_refs[i].at[:, idx]

    def copy(i, k, block, to, src=None):
        return pltpu.make_async_remote_copy(
            src_ref=slot(i, block) if src is None else src, dst_ref=slot(i, block),
            send_sem=send_sems.at[i, k], recv_sem=recv_sems.at[i, k], device_id=to, device_id_type=MESH)

    mine = [pltpu.make_async_copy(x_refs[i], slot(i, me), local_sems.at[i]) for i in range(n)]
    first = []
    for i in range(n):
        first.append(copy(i, 0, me, sibling, src=x_refs[i]))
        first += [copy(i, 1 + j, me, (*chip, c), src=x_refs[i]) for j, chip in enumerate(chips)]

    def start():
        for cp in mine + first:
            cp.start()

    def finish():
        passed = []
        for j, chip in enumerate(chips):
            for i in range(n):
                copy(i, 1 + j, (*chip, c), me).wait_recv()
                cp = copy(i, 4 + j, (*chip, c), sibling)
                cp.start()
                passed.append(cp)
        for i in range(n):
            copy(i, 0, sibling, me).wait_recv()
        for j, chip in enumerate(chips):
            for i in range(n):
                copy(i, 4 + j, (*chip, 1 - c), me).wait_recv()
        for cp in first + passed:
            cp.wait_send()
        for cp in mine:
            cp.wait()

    return start, finish


def _exchange_ops(g_refs, o_refs, send_sems, recv_sems, local_sems):
    n = len(g_refs)
    x, y, c = lax.axis_index("x"), lax.axis_index("y"), lax.axis_index("c")
    me = _dev_index((x, y, c))
    peers = [(x ^ ((k >> 2) & 1), y ^ ((k >> 1) & 1), c ^ (k & 1)) for k in range(1, N_DEV)]

    def copy(i, k):
        peer = peers[k]
        return pltpu.make_async_remote_copy(
            src_ref=g_refs[i].at[_dev_index(peer)], dst_ref=o_refs[i].at[me],
            send_sem=send_sems.at[i, k], recv_sem=recv_sems.at[i, k], device_id=peer, device_id_type=MESH)

    def arrival(i, k):
        peer = peers[k]
        return pltpu.make_async_remote_copy(
            src_ref=g_refs[i].at[me], dst_ref=o_refs[i].at[_dev_index(peer)],
            send_sem=send_sems.at[i, k], recv_sem=recv_sems.at[i, k], device_id=peer, device_id_type=MESH)

    mine = [pltpu.make_async_copy(g_refs[i].at[me], o_refs[i].at[me], local_sems.at[i]) for i in range(n)]
    sends = [copy(i, k) for i in range(n) for k in range(N_DEV - 1)]

    def start():
        for cp in mine + sends:
            cp.start()

    def finish():
        for i in range(n):
            for k in range(N_DEV - 1):
                arrival(i, k).wait_recv()
        for cp in sends:
            cp.wait_send()
        for cp in mine:
            cp.wait()

    return start, finish


class _Ride:
    def __init__(self, rider, n_in, n_out, n_scr):
        self.rider, self.n_in, self.n_out, self.n_scr = rider, n_in, n_out, n_scr
        self.args = [] if rider is None else list(rider[1])
        n = self.n = len(self.args)
        hbm = pl.BlockSpec(memory_space=pltpu.HBM)
        self.in_specs, self.out_specs = [hbm] * n, [hbm] * n
        if rider is None:
            self.out_shape, self.scratch = [], []
            return
        if rider[0] == 'gather':
            self.out_shape = [jax.ShapeDtypeStruct(a.shape[:ax] + (N_DEV,) + a.shape[ax:], a.dtype) for a, ax in zip(self.args, rider[2])]
        else:
            self.out_shape = [jax.ShapeDtypeStruct(a.shape, a.dtype) for a in self.args]
        self.scratch = [pltpu.SemaphoreType.DMA((n, 7)), pltpu.SemaphoreType.DMA((n, 7)), pltpu.SemaphoreType.DMA((n,))]

    def split(self, refs):
        a = self.n_in
        c = a + self.n + self.n_out
        e = c + self.n
        return refs[:a], refs[a + self.n:c], refs[e:e + self.n_scr]

    def _ops(self, refs):
        a = self.n_in
        c = a + self.n + self.n_out
        e = c + self.n + self.n_scr
        x_refs, o_refs, sems = refs[a:a + self.n], refs[c:c + self.n], refs[e:e + 3]
        if self.rider[0] == 'gather':
            return _gather_ops(x_refs, o_refs, *sems, self.rider[2])
        return _exchange_ops(x_refs, o_refs, *sems)

    def start(self, refs, cond):
        if self.rider is not None:
            pl.when(cond)(lambda: self._ops(refs)[0]())

    def finish(self, refs, cond):
        if self.rider is not None:
            pl.when(cond)(lambda: self._ops(refs)[1]())


def _comm_call(rider, *, name):
    ride = _Ride(rider, 0, 0, 0)

    def body(*refs):
        start, finish = ride._ops(refs)
        start()
        finish()

    return pl.pallas_call(body, name=name, in_specs=ride.in_specs, out_specs=ride.out_specs, out_shape=ride.out_shape,
                          scratch_shapes=ride.scratch)(*ride.args)


def _all_gather(xs, axes, *, name):
    return _comm_call(('gather', xs, axes), name=name)


def _exchange(gs, *, name):
    return _comm_call(('exchange', gs), name=name)


W_IN_SEGMENTS = [('a', 0, D_MODEL), ('ag', D_MODEL, D_MODEL), ('z', OFF_Z, OFF_XBC - OFF_Z), ('x', OFF_XBC, OFF_DT - OFF_XBC),
                 ('dt', OFF_DT, OFF_Q - OFF_DT), ('qkv', OFF_Q, OFF_GATE - OFF_Q), ('g', OFF_GATE, IN_COLS - OFF_GATE)]
W_IN_SHARD = IN_COLS // N_DEV


def _w_in_pieces():
    out = []
    for d in range(N_DEV):
        for k, (_, off, n) in enumerate(W_IN_SEGMENTS):
            lo, hi = max(d * W_IN_SHARD, off), min((d + 1) * W_IN_SHARD, off + n)
            if lo < hi:
                out.append((d, k, lo - d * W_IN_SHARD, lo - off, hi - lo))
    return out


def _seg_width(n):
    return -(-n // LANES) * LANES


def _w_in_segments(gathered, *, tr=256):
    rows = gathered.shape[1]
    widths = [_seg_width(n) for _, _, n in W_IN_SEGMENTS]

    def body(g_ref, *o_refs):
        for k, (_, _, n) in enumerate(W_IN_SEGMENTS):
            if widths[k] != n:
                o_refs[k][...] = jnp.zeros_like(o_refs[k])
        for d, k, at_dev, at_seg, n in _w_in_pieces():
            o_refs[k][:, at_seg:at_seg + n] = g_ref[d, :, at_dev:at_dev + n]

    return pl.pallas_call(
        body, name="w_in_segments", grid=(rows // tr,), in_specs=[pl.BlockSpec((N_DEV, tr, W_IN_SHARD), lambda i: (0, i, 0))],
        out_specs=[pl.BlockSpec((tr, w), lambda i: (i, 0)) for w in widths],
        out_shape=[jax.ShapeDtypeStruct((rows, w), gathered.dtype) for w in widths],
        compiler_params=pltpu.CompilerParams(dimension_semantics=("parallel",)),
    )(gathered)


def _w_in_by_device(segs, *, tr=256):
    rows = segs[0].shape[0]

    def body(*refs):
        o_ref = refs[-1]
        for d, k, at_dev, at_seg, n in _w_in_pieces():
            o_ref[d, :, at_dev:at_dev + n] = refs[k][:, at_seg:at_seg + n]

    return pl.pallas_call(
        body, name="w_in_by_device", grid=(rows // tr,), in_specs=[pl.BlockSpec((tr, s.shape[1]), lambda i: (i, 0)) for s in segs],
        out_specs=pl.BlockSpec((N_DEV, tr, W_IN_SHARD), lambda i: (0, i, 0)),
        out_shape=jax.ShapeDtypeStruct((N_DEV, rows, W_IN_SHARD), segs[0].dtype),
        compiler_params=pltpu.CompilerParams(dimension_semantics=("parallel",)),
    )(*segs)


def _pack(arrays):
    parts = []
    for a in arrays:
        flat = a.reshape(-1)
        pad = (-flat.shape[0]) % LANES
        if pad:
            flat = jnp.concatenate([flat, jnp.zeros((pad,), flat.dtype)])
        parts.append(flat.reshape(-1, LANES))
    return jnp.concatenate(parts, axis=0)


def _unpack(buf, shapes):
    out, row = [], 0
    for shp in shapes:
        size = int(np.prod(shp))
        rows = -(-size // LANES)
        out.append(buf[row:row + rows].reshape(-1)[:size].reshape(shp))
        row += rows
    return out


def _row_tile(rows, width, n_bufs):
    padded = -(-width // LANES) * LANES
    cap = max(16, (12 << 20) // (padded * 4 * n_bufs))
    if rows <= cap:
        return rows
    tr = (cap // 16) * 16
    while tr > 16 and rows % tr:
        tr -= 16
    return tr if rows % tr == 0 else rows


def _step(p, m, v, x, mem, loss_target):
    t = x.shape[1]
    h0 = x.reshape(t, D_MODEL)
    mem2 = mem.reshape(MEM_LEN, D_MODEL)
    tgt = loss_target.reshape(t, D_MODEL)
    tr = 256 if t % 256 == 0 else t
    my = _dev_index((lax.axis_index("x"), lax.axis_index("y"), lax.axis_index("c")))

    small_sh_shapes = [p[n].shape for n in SMALL_SHARDED]
    gathered_small = _all_gather([_pack([p[n] for n in SMALL_SHARDED])], [0], name="gather_small")[0]
    full = {}
    for n, a in zip(SMALL_SHARDED, zip(*[_unpack(gathered_small[d], small_sh_shapes) for d in range(N_DEV)])):
        full[n] = jnp.concatenate(a, axis=-1)
    big_names = ROW_SHARDED + COL_SHARDED
    seg_order = ['a', 'ag', 'z', 'x', 'dt', 'qkv', 'g']
    wg = {n: [] for n in big_names}
    w_seg = {k: [] for k in seg_order}
    others = [n for n in big_names if n != 'w_in']

    def shards(layer, names):
        return [p[n][layer].astype(BF16) for n in names]

    def use_weights(layer_names, got):
        for n, a in zip([n for _, names in layer_names for n in names], got):
            if n in ROW_SHARDED:
                wg[n].append(a.reshape(a.shape[0] * a.shape[1], a.shape[2]))
            elif n != 'w_in':
                wg[n].append(a)
            else:
                for k, seg in zip(seg_order, _w_in_segments(a)):
                    w_seg[k].append(seg)

    def gather_rider(layer_names):
        xs = [a for layer, names in layer_names for a in shards(layer, names)]
        return ('gather', xs, [0] * len(xs)) if xs else None

    def riding(layer):
        if layer == 0:
            return [(0, others)], [(1, ['w_in']), (1, others)]
        if layer + 1 < DEPTH:
            return [(layer + 1, ['w_in'])], [(layer + 1, others)]
        return [], []

    use_weights([(0, ['w_in'])], _all_gather(shards(0, ['w_in']), [0], name="gather_weights"))

    def vec(name, layer, width=None):
        a = p[name][layer].reshape(1, -1)
        if width is not None and a.shape[1] < width:
            a = jnp.pad(a, ((0, 0), (0, width - a.shape[1])))
        return a

    buckets = jnp.asarray(_rel_buckets().reshape(1, -1))
    bias = _band_bias(jnp.transpose(p['rel_table']), buckets).reshape(ATTN_KV_HEADS, ATTN_REP * ATTN_BLOCK, 2 * ATTN_BLOCK)

    saved = []
    h = h0
    for l in range(DEPTH):
        s = {'h0': h}
        u = _rowwise(lambda a, g: (_f_rms(a, g), ()), [(h, 0, D_MODEL)], [vec('norm_mix', l)], [(D_MODEL, BF16)], tr=tr, name="rms_mix")[0]
        s['u'] = u
        pr = {k: _mm(u, (w_seg[k], 'plain', l), mode='nn', name="proj_" + k) for k in seg_order}
        s['pr'] = pr
        dw_w, dw_b = full['conv_dw_w'][l], vec('conv_dw_b', l)
        ca = _conv_fwd([(pr['a'], 0), (pr['ag'], 0)], dw_w, dw_b, kk=CONV_KERNEL, pre_glu=True, post_silu=False, name="conv31")
        s['ca'] = ca
        ya_in = _rowwise(lambda a, g, b: (_f_lnsilu(a, g, b), ()), [(ca, 0, D_MODEL)], [vec('conv_ln_g', l), vec('conv_ln_b', l)],
                         [(D_MODEL, BF16)], tr=tr, name="ln_silu")[0]
        s['ya_in'] = ya_in
        xbc = _conv_fwd([(pr['x'], 0)], full['ssd_conv_w'][l], vec('ssd_conv_b', l), kk=SSD_CONV, pre_glu=False, post_silu=True, name="conv4")
        s['xbc'] = xbc
        ssd_vecs = [vec('ssd_dt_bias', l, LANES), vec('ssd_A_log', l, LANES), vec('ssd_D', l, LANES)]
        with_ssd, with_swa = riding(l)
        (y_ssd, states), got = _ssd_fwd(xbc, pr['dt'], *ssd_vecs, name="ssd", rider=gather_rider(with_ssd))
        use_weights(with_ssd, got)
        s['y_ssd'], s['states'] = y_ssd, states
        yb_in = _rowwise(lambda a, z, g: (_f_ssdgate(a, z, g), ()), [(y_ssd, 0, SSD_INNER), (pr['z'], 0, SSD_INNER)], [vec('ssd_norm_g', l)],
                         [(SSD_INNER, BF16)], tr=tr, name="ssd_gate")[0]
        s['yb_in'] = yb_in
        y_b = _mm(yb_in, (wg['w_ssd_out'], 'plain', l), mode='nn', name="ssd_out")
        att, got = _attn_fwd(pr['qkv'], bias, vec('attn_sinks', l), vec('attn_q_norm', l), vec('attn_k_norm', l), name="swa",
                             rider=gather_rider(with_swa))
        use_weights(with_swa, got)
        s['att'] = att
        y_c = _mm(att, (wg['w_attn_out'], 'plain', l), mode='nn', name="attn_out")
        y_a = _mm(ya_in, (wg['w_conv_out'], 'plain', l), mode='nn', name="conv_out")
        s['y_a'], s['y_b'], s['y_c'] = y_a, y_b, y_c
        merged = _rowwise(lambda pg, a, b, c, gb: (_f_merge(pg, a, b, c, gb), ()),
                          [(pr['g'], 0, 3 * D_MODEL), (y_a, 0, D_MODEL), (y_b, 0, D_MODEL), (y_c, 0, D_MODEL)], [full['gate_bias'][l]],
                          [(D_MODEL, BF16)], tr=tr, name="merge")[0]
        s['merged'] = merged
        h = _mm(merged, (wg['w_mix_out'], 'plain', l), mode='nn', add=h, name="mix_out")
        s['h1'] = h
        un = _rowwise(lambda a, g: (_f_rms(a, g), ()), [(h, 0, D_MODEL)], [vec('norm_xattn', l)], [(D_MODEL, BF16)], tr=tr, name="rms_xattn")[0]
        memn = _rowwise(lambda a, g: (_f_rms(a, g), ()), [(mem2, 0, D_MODEL)], [vec('norm_mem', l)], [(D_MODEL, BF16)], tr=MEM_LEN, name="rms_mem")[0]
        s['un'], s['memn'] = un, memn
        xq = _mm(un, (wg['w_xq'], 'plain', l), mode='nn', name="xq")
        kv = _mm(memn, (wg['w_xkv'], 'col', l), mode='nn', name="xkv", tn=256)
        s['xq'], s['kv'] = xq, kv
        xo = _rowwise(lambda q, kvv, qg, kg: (_f_xattn(q, kvv, qg, kg), ()), [(xq, 0, D_MODEL)], [kv, vec('xattn_q_norm', l), vec('xattn_k_norm', l)],
                      [(D_MODEL, BF16)], tr=tr, name="xattn")[0]
        s['xo'] = xo
        h = _mm(xo, (wg['w_xo'], 'plain', l), mode='nn', add=h, name="xattn_out")
        s['h2'] = h
        um = _rowwise(lambda a, g: (_f_rms(a, g), ()), [(h, 0, D_MODEL)], [vec('norm_mlp', l)], [(D_MODEL, BF16)], tr=tr, name="rms_mlp")[0]
        s['um'] = um
        up, act = _mm(um, (wg['w_mlp_up'], 'col', l), mode='nn', name="mlp_up", epi=(lambda acc: (acc,) + _f_relu2(acc), [], [F32, BF16]))
        s['up'], s['act'] = up, act
        h = _mm(act, (wg['w_mlp_down'], 'plain', l), mode='nn', add=h, name="mlp_down")
        saved.append(s)

    dh, dh_b, loss_part = _rowwise(_f_loss, [(h, 0, D_MODEL), (tgt, 0, D_MODEL)], [], [(D_MODEL, F32), (D_MODEL, BF16)], [(1, LANES)], tr=tr, name="loss")

    sg = {n: [None] * DEPTH for n in SMALL if n != 'rel_table'}
    dbias_layers = [None] * DEPTH
    recv = {}
    with_d_swa = ['w_mlp_down', 'w_mlp_up', 'w_xo', 'w_xq', 'w_xkv', 'w_mix_out', 'w_attn_out']
    left_over = []

    def by_device(g_):
        return g_.reshape(N_DEV, g_.shape[0] // N_DEV, g_.shape[1])

    for l in reversed(range(DEPTH)):
        s = saved[l]
        bg = {}
        dup = _mm(dh_b, (wg['w_mlp_down'], 'plain', l), mode='nt', name="d_act",
                  epi=(lambda acc, up_: _vjp_rows(_f_relu2, 1, 1)(up_, acc)[0], [s['up']], [BF16]))[0]
        bg['w_mlp_down'] = _mm(s['act'], dh_b, mode='tn', out_dtype=BF16, name="dw_mlp_down")
        bg['w_mlp_up'] = _mm(s['um'], dup, mode='tn', out_dtype=BF16, out_col=True, name="dw_mlp_up")
        dum = _mm(dup, (wg['w_mlp_up'], 'col', l), mode='nt', name="d_um", tk=512)
        dh, dh_b, dg = _rms_bwd_call(s['h2'], vec('norm_mlp', l), dum, dh, tr, "d_rms_mlp")
        sg['norm_mlp'][l] = dg
        dxo = _mm(dh_b, (wg['w_xo'], 'plain', l), mode='nt', out_dtype=BF16, name="d_xo")
        bg['w_xo'] = _mm(s['xo'], dh_b, mode='tn', out_dtype=BF16, name="dw_xo")
        qg, kg = vec('xattn_q_norm', l), vec('xattn_k_norm', l)
        dxq, dkv, dqg, dkg = _xattn_bwd_call(s['xq'], s['kv'], qg, kg, dxo, tr)
        sg['xattn_q_norm'][l], sg['xattn_k_norm'][l] = dqg, dkg
        bg['w_xq'] = _mm(s['un'], dxq, mode='tn', out_dtype=BF16, name="dw_xq")
        dun = _mm(dxq, (wg['w_xq'], 'plain', l), mode='nt', name="d_un")
        dh, dh_b, dg = _rms_bwd_call(s['h1'], vec('norm_xattn', l), dun, dh, tr, "d_rms_xattn")
        sg['norm_xattn'][l] = dg
        bg['w_xkv'] = _mm(s['memn'], dkv, mode='tn', out_dtype=BF16, out_col=True, name="dw_xkv", tn=256)
        dmemn = _mm(dkv, (wg['w_xkv'], 'col', l), mode='nt', name="d_memn", tk=256)
        _, _, dg = _rms_bwd_call(mem2, vec('norm_mem', l), dmemn, jnp.zeros_like(mem2), MEM_LEN, "d_rms_mem")
        sg['norm_mem'][l] = dg
        dmerged = _mm(dh_b, (wg['w_mix_out'], 'plain', l), mode='nt', out_dtype=BF16, name="d_merged")
        bg['w_mix_out'] = _mm(s['merged'], dh_b, mode='tn', out_dtype=BF16, name="dw_mix_out")
        pr = s['pr']
        gb = full['gate_bias'][l]
        dpg, dya, dyb, dyc, dgb = _merge_bwd_call(pr['g'], s['y_a'], s['y_b'], s['y_c'], gb, dmerged, tr)
        sg['gate_bias'][l] = dgb
        dseg = {'g': dpg}
        datt = _mm(dyc, (wg['w_attn_out'], 'plain', l), mode='nt', out_dtype=BF16, name="d_att")
        bg['w_attn_out'] = _mm(s['att'], dyc, mode='tn', out_dtype=BF16, name="dw_attn_out")
        early = [bg[n] if n in COL_SHARDED else by_device(bg[n]) for n in with_d_swa]
        (dqkv, dbias_l, dsk, dqn, dkn), got = _attn_bwd(pr['qkv'], datt, bias, vec('attn_sinks', l), vec('attn_q_norm', l), vec('attn_k_norm', l),
                                                     name="d_swa", rider=('exchange', early))
        recv.update({(n, l): r for n, r in zip(with_d_swa, got)})
        dseg['qkv'] = dqkv
        dbias_layers[l] = dbias_l
        sg['attn_sinks'][l], sg['attn_q_norm'][l], sg['attn_k_norm'][l] = dsk, dqn, dkn
        dyb_in = _mm(dyb, (wg['w_ssd_out'], 'plain', l), mode='nt', out_dtype=BF16, name="d_yb_in")
        bg['w_ssd_out'] = _mm(s['yb_in'], dyb, mode='tn', out_dtype=BF16, name="dw_ssd_out")
        ng = vec('ssd_norm_g', l)
        dy_ssd, dz, dng = _ssdgate_bwd_call(s['y_ssd'], pr['z'], ng, dyb_in, tr)
        sg['ssd_norm_g'][l] = dng
        dseg['z'] = dz
        ssd_vecs = [vec('ssd_dt_bias', l, LANES), vec('ssd_A_log', l, LANES), vec('ssd_D', l, LANES)]
        dya_in = _mm(dya, (wg['w_conv_out'], 'plain', l), mode='nt', out_dtype=BF16, name="d_ya_in")
        bg['w_conv_out'] = _mm(s['ya_in'], dya, mode='tn', out_dtype=BF16, name="dw_conv_out")
        late = [('w_ssd_out', l, by_device(bg['w_ssd_out'])), ('w_conv_out', l, by_device(bg['w_conv_out']))] + left_over
        (dxbc_act, ddt, dtb, dal, ddsk), got = _ssd_bwd(s['xbc'], pr['dt'], s['states'], dy_ssd, *ssd_vecs, name="d_ssd",
                                                        rider=('exchange', [a for _, _, a in late]))
        recv.update({(n, ll): r for (n, ll, _), r in zip(late, got)})
        dseg['dt'] = ddt
        sg['ssd_dt_bias'][l], sg['ssd_A_log'][l], sg['ssd_D'][l] = dtb[:, :SSD_HEADS], dal[:, :SSD_HEADS], ddsk[:, :SSD_HEADS]
        dxbc, dcw, dcb = _conv_bwd([(pr['x'], 0)], full['ssd_conv_w'][l], vec('ssd_conv_b', l), [(dxbc_act, 0, SSD_XBC // LANES)],
                                   kk=SSD_CONV, pre_glu=False, post_silu=True, name="d_conv4", dx_dtype=BF16)
        dseg['x'] = dxbc
        sg['ssd_conv_w'][l], sg['ssd_conv_b'][l] = dcw, dcb
        lg, lb = vec('conv_ln_g', l), vec('conv_ln_b', l)
        dca, dlg, dlb = _lnsilu_bwd_call(s['ca'], lg, lb, dya_in, tr)
        sg['conv_ln_g'][l], sg['conv_ln_b'][l] = dlg, dlb
        dseg['a'], dseg['ag'], dww, dwb = _conv_bwd([(pr['a'], 0), (pr['ag'], 0)], full['conv_dw_w'][l], vec('conv_dw_b', l),
                                                    [(dca, 0, D_MODEL // LANES)], kk=CONV_KERNEL, pre_glu=True, post_silu=False, name="d_conv31", dx_dtype=BF16)
        sg['conv_dw_w'][l], sg['conv_dw_b'][l] = dww, dwb
        du = None
        dw_parts = []
        for k in seg_order:
            du = _mm(dseg[k], (w_seg[k], 'plain', l), mode='nt', add=du, name="d_u_" + k)
            dw_parts.append(_mm(s['u'], dseg[k], mode='tn', out_dtype=BF16, name="dw_in_" + k))
        dw_in = _w_in_by_device(dw_parts)
        dh, dh_b, dg = _rms_bwd_call(s['h0'], vec('norm_mix', l), du, dh, tr, "d_rms_mix")
        sg['norm_mix'][l] = dg
        left_over = [('w_in', l, dw_in)]

    got = _exchange([a for _, _, a in left_over], name="exchange_grads")
    recv.update({(n, ll): r for (n, ll, _), r in zip(left_over, got)})
    grad_x = dh.reshape(x.shape)
    d_rel = jnp.transpose(_band_bias_bwd(jnp.stack(dbias_layers).reshape(DEPTH, ATTN_HEADS, -1), buckets))

    small_full = {'rel_table': d_rel}
    for n in SMALL:
        if n != 'rel_table':
            small_full[n] = jnp.stack(sg[n]).reshape((DEPTH,) + (full[n].shape[1:] if n in SMALL_SHARDED else p[n].shape[1:]))
    small_shapes = [(1, LANES)] + [small_full[n].shape for n in SMALL]
    packed = _pack([loss_part] + [small_full[n] for n in SMALL])
    slots = _all_gather([packed], [0], name="gather_small_grads")[0]
    rows = packed.shape[0]
    reduced = _rowwise(_f_sum_slots, [(slots, 0, LANES)], [], [(LANES, F32)], tr=_row_tile(rows, LANES, 12), name="sum_small")[0]
    red = _unpack(reduced, small_shapes)
    loss = red[0][0, 0]
    small_grad = {}
    for n, g_ in zip(SMALL, red[1:]):
        if n in SMALL_SHARDED:
            wdt = p[n].shape[-1]
            g_ = lax.dynamic_slice_in_dim(g_, my * wdt, wdt, axis=g_.ndim - 1)
        small_grad[n] = g_
    local_shapes = [p[n].shape for n in SMALL]
    pk = lambda d: _pack([d[n] for n in SMALL])
    pg_, pw_, pm_, pv_ = pk(small_grad), pk(p), pk(m), pk(v)
    srows = pg_.shape[0]
    sd, sm, sv = _rowwise(_f_adam, [(pg_, 0, LANES), (pw_, 0, LANES), (pm_, 0, LANES), (pv_, 0, LANES)], [],
                          [(LANES, F32)] * 3, tr=_row_tile(srows, LANES, 16), name="adam_small")
    out_delta = dict(zip(SMALL, _unpack(sd, local_shapes)))
    out_m = dict(zip(SMALL, _unpack(sm, local_shapes)))
    out_v = dict(zip(SMALL, _unpack(sv, local_shapes)))
    out_grad = dict(small_grad)

    per_layer = {n: [] for n in big_names}
    for l in range(DEPTH):
        for n in big_names:
            r = recv[(n, l)]
            rws, wdt = r.shape[1], r.shape[2]
            tr_w = _row_tile(rws, wdt, 24)
            outs = _rowwise(_f_adam_slots, [(r, 0, wdt), (p[n], 0, wdt, l), (m[n], 0, wdt, l), (v[n], 0, wdt, l)], [],
                            [(wdt, F32)] * 4, tr=tr_w, name="adam_" + n)
            per_layer[n].append(outs)
    for n in big_names:
        for k, dst in enumerate((out_grad, out_delta, out_m, out_v)):
            dst[n] = jnp.stack([per_layer[n][l][k] for l in range(DEPTH)])
    return (loss, grad_x, *[out_grad[n] for n in WEIGHTS], *[out_delta[n] for n in WEIGHTS],
            *[out_m[n] for n in WEIGHTS], *[out_v[n] for n in WEIGHTS])


def _rms_bwd_call(h, g, du, dres, tr, name):
    return _rowwise(lambda a, d, r, gg: _f_rms_bwd(a, gg, d, r), [(h, 0, D_MODEL), (du, 0, D_MODEL), (dres, 0, D_MODEL)], [g],
                    [(D_MODEL, F32), (D_MODEL, BF16)], [g.shape], tr=tr, name=name)


def _xattn_bwd_call(xq, kv, qg, kg, dxo, tr):
    def f(q, d, kvv, qgv, kgv):
        return _vjp_rows(lambda a, b, c, e: _f_xattn(a, b, c, e), 4, 1)(q, kvv, qgv, kgv, d)
    return _rowwise(f, [(xq, 0, D_MODEL), (dxo, 0, D_MODEL)], [kv, qg, kg], [(D_MODEL, BF16)], [kv.shape, qg.shape, kg.shape], tr=tr, name="d_xattn")


def _merge_bwd_call(pg, ya, yb, yc, gb, dmerged, tr):
    def f(a, b, c, e, d, gbv):
        return _vjp_rows(_f_merge, 5, 4)(a, b, c, e, gbv, d)
    return _rowwise(f, [(pg, 0, 3 * D_MODEL), (ya, 0, D_MODEL), (yb, 0, D_MODEL), (yc, 0, D_MODEL), (dmerged, 0, D_MODEL)], [gb],
                    [(3 * D_MODEL, BF16)] + [(D_MODEL, BF16)] * 3, [gb.shape], tr=tr, name="d_merge")


def _ssdgate_bwd_call(y, z, ng, dy, tr):
    def f(a, b, d, g):
        return _vjp_rows(_f_ssdgate, 3, 2)(a, b, g, d)
    return _rowwise(f, [(y, 0, SSD_INNER), (z, 0, SSD_INNER), (dy, 0, SSD_INNER)], [ng], [(SSD_INNER, F32), (SSD_INNER, BF16)], [ng.shape], tr=tr, name="d_ssd_gate")


def _lnsilu_bwd_call(ca, lg, lb, dy, tr):
    def f(a, d, g, b):
        return _vjp_rows(_f_lnsilu, 3, 1)(a, g, b, d)
    return _rowwise(f, [(ca, 0, D_MODEL), (dy, 0, D_MODEL)], [lg, lb], [(D_MODEL, F32)], [lg.shape, lb.shape], tr=tr, name="d_ln_silu")


def kernel(x, mem, rel_table, norm_mix, w_in, gate_bias, conv_dw_w, conv_dw_b, conv_ln_g, conv_ln_b, w_conv_out, ssd_conv_w, ssd_conv_b, ssd_dt_bias, ssd_A_log, ssd_D, ssd_norm_g, w_ssd_out, attn_q_norm, attn_k_norm, attn_sinks, w_attn_out, w_mix_out, norm_xattn, norm_mem, w_xq, w_xkv, xattn_q_norm, xattn_k_norm, w_xo, norm_mlp, w_mlp_up, w_mlp_down, loss_target, m_rel_table, m_norm_mix, m_w_in, m_gate_bias, m_conv_dw_w, m_conv_dw_b, m_conv_ln_g, m_conv_ln_b, m_w_conv_out, m_ssd_conv_w, m_ssd_conv_b, m_ssd_dt_bias, m_ssd_A_log, m_ssd_D, m_ssd_norm_g, m_w_ssd_out, m_attn_q_norm, m_attn_k_norm, m_attn_sinks, m_w_attn_out, m_w_mix_out, m_norm_xattn, m_norm_mem, m_w_xq, m_w_xkv, m_xattn_q_norm, m_xattn_k_norm, m_w_xo, m_norm_mlp, m_w_mlp_up, m_w_mlp_down, v_rel_table, v_norm_mix, v_w_in, v_gate_bias, v_conv_dw_w, v_conv_dw_b, v_conv_ln_g, v_conv_ln_b, v_w_conv_out, v_ssd_conv_w, v_ssd_conv_b, v_ssd_dt_bias, v_ssd_A_log, v_ssd_D, v_ssd_norm_g, v_w_ssd_out, v_attn_q_norm, v_attn_k_norm, v_attn_sinks, v_w_attn_out, v_w_mix_out, v_norm_xattn, v_norm_mem, v_w_xq, v_w_xkv, v_xattn_q_norm, v_xattn_k_norm, v_w_xo, v_norm_mlp, v_w_mlp_up, v_w_mlp_down):
    args = locals()
    p = {n: args[n] for n in WEIGHTS}
    m = {n: args["m_" + n] for n in WEIGHTS}
    v = {n: args["v_" + n] for n in WEIGHTS}
    return _step(p, m, v, x, mem, loss_target)
```

```python
import functools
import math

import numpy as np
import jax
import jax.numpy as jnp
from jax import lax
from jax.experimental import pallas as pl
from jax.experimental.pallas import tpu as pltpu

F32 = jnp.float32
BF16 = jnp.bfloat16
HI = lax.Precision.HIGHEST
MESH = pl.DeviceIdType.MESH

N_DEV = 8
D_MODEL = 1024
DEPTH = 4
MEM_LEN = 256
EPS = 1e-6
NEG_INF = -1e30
CONV_KERNEL = 31
SSD_INNER = 2048
SSD_HEAD_DIM = 64
SSD_HEADS = 32
SSD_GROUPS = 4
SSD_STATE = 128
SSD_CONV = 4
SSD_CHUNK = 128
SSD_XBC = SSD_INNER + 2 * SSD_GROUPS * SSD_STATE
HEADS_PER_GROUP = SSD_HEADS // SSD_GROUPS
ATTN_HEADS = 16
ATTN_KV_HEADS = 4
ATTN_HEAD_DIM = 64
ATTN_BLOCK = 128
ATTN_REP = ATTN_HEADS // ATTN_KV_HEADS
REL_BUCKETS = 32
REL_MAX_DIST = 128
XATTN_HEADS = 4
XATTN_HEAD_DIM = 256
MLP_HIDDEN = 4096
OFF_Z = 2048
OFF_XBC = 4096
OFF_DT = 7168
OFF_Q = 7200
OFF_GATE = 8736
IN_COLS = 11808
LANES = 128
CONV_PAD = 32
MM_VMEM_BUDGET = 20 << 20

ADAM_LR, ADAM_B1, ADAM_B2, ADAM_EPS, ADAM_WD, ADAM_STEP = 0.001, 0.9, 0.999, 1e-08, 0.01, 10

WEIGHTS = ['rel_table', 'norm_mix', 'w_in', 'gate_bias', 'conv_dw_w', 'conv_dw_b', 'conv_ln_g', 'conv_ln_b', 'w_conv_out',
           'ssd_conv_w', 'ssd_conv_b', 'ssd_dt_bias', 'ssd_A_log', 'ssd_D', 'ssd_norm_g', 'w_ssd_out', 'attn_q_norm',
           'attn_k_norm', 'attn_sinks', 'w_attn_out', 'w_mix_out', 'norm_xattn', 'norm_mem', 'w_xq', 'w_xkv', 'xattn_q_norm',
           'xattn_k_norm', 'w_xo', 'norm_mlp', 'w_mlp_up', 'w_mlp_down']
ROW_SHARDED = ['w_conv_out', 'w_ssd_out', 'w_attn_out', 'w_mix_out', 'w_xq', 'w_xo', 'w_mlp_down']
COL_SHARDED = ['w_in', 'w_xkv', 'w_mlp_up']
BIG = ROW_SHARDED + COL_SHARDED
SMALL_SHARDED = ['gate_bias', 'conv_dw_w', 'ssd_conv_w']
SMALL = [n for n in WEIGHTS if n not in BIG]


def _bdot(a, b, dims):
    return lax.dot_general(a.astype(BF16), b.astype(BF16), (dims, ((), ())), preferred_element_type=F32)


def _hdot(a, b, dims):
    return lax.dot_general(a, b, (dims, ((), ())), precision=HI, preferred_element_type=F32)


def _dot3(x, s, dims, s_first=False):
    hi = x.astype(jnp.bfloat16)
    rest = x - hi.astype(F32)
    mid = rest.astype(jnp.bfloat16)
    lo = (rest - mid.astype(F32)).astype(jnp.bfloat16)
    dot = lambda piece: lax.dot_general(*((s, piece) if s_first else (piece, s)), (dims, ((), ())), preferred_element_type=F32)
    return dot(hi) + dot(mid) + dot(lo)


@jax.custom_vjp
def _select_dot(s, x):
    return _dot3(x, s, ((1,), (0,)), s_first=True)


def _select_dot_fwd(s, x):
    return _select_dot(s, x), s


def _select_dot_bwd(s, g):
    return jnp.zeros_like(s), _dot3(g, s, ((0,), (0,)), s_first=True)


_select_dot.defvjp(_select_dot_fwd, _select_dot_bwd)


@jax.custom_vjp
def _spread_dot(x, s):
    return _dot3(x, s, ((1,), (0,)))


def _spread_dot_fwd(x, s):
    return _spread_dot(x, s), s


def _spread_dot_bwd(s, g):
    return _dot3(g, s, ((1,), (1,))), jnp.zeros_like(s)


_spread_dot.defvjp(_spread_dot_fwd, _spread_dot_bwd)


def _one_hot_groups(rows, cols, group, transpose=False):
    r = lax.broadcasted_iota(jnp.int32, (rows, cols), 0)
    c = lax.broadcasted_iota(jnp.int32, (rows, cols), 1)
    shift = int(math.log2(group))
    hit = (lax.shift_right_logical(r, shift) == c) if transpose else (r == lax.shift_right_logical(c, shift))
    return hit.astype(jnp.bfloat16)


def _heads_rms(x, g):
    w = x.shape[-1]
    nh = w // ATTN_HEAD_DIM
    ss = _spread_dot(x * x, _one_hot_groups(w, nh, ATTN_HEAD_DIM, transpose=True))
    scale = lax.rsqrt(ss * (1.0 / ATTN_HEAD_DIM) + EPS)
    y = x * _spread_dot(scale, _one_hot_groups(nh, w, ATTN_HEAD_DIM)) * jnp.concatenate([g] * nh, axis=-1)
    return [y[:, h * ATTN_HEAD_DIM:(h + 1) * ATTN_HEAD_DIM] for h in range(nh)]


def _pick(dim, pref):
    if dim <= pref:
        return dim
    t = (pref // LANES) * LANES
    while dim % t:
        t -= LANES
    return t


def _logical(op):
    arr, kind, _ = op
    r, c = arr.shape[-2:]
    return (r, c * N_DEV) if kind == 'col' else (r, c)


def _opspec(op, br, bc, rc):
    arr, kind, layer = op
    lead = () if layer is None else (layer,)
    none = (None,) * len(lead)
    if kind == 'plain':
        return pl.BlockSpec(none + (br, bc), lambda i, j, k: lead + rc(i, j, k))
    per = arr.shape[-1] // bc

    def imap(i, j, k):
        r, c = rc(i, j, k)
        if per == 1:
            return lead + (c, r, 0)
        return lead + (lax.div(c, per), r, lax.rem(c, per))
    return pl.BlockSpec(none + (None, br, bc), imap)


def _mm(a, b, *, mode, name, add=None, out_dtype=F32, out_col=False, epi=None, tm=2048, tn=512, tk=1024):
    def operand(op):
        op = op if isinstance(op, tuple) else (op, 'plain', None)
        return (op[0][op[2]], op[1], None) if isinstance(op[0], list) else op

    a, b = operand(a), operand(b)
    ar, ac = _logical(a)
    br_, bc_ = _logical(b)
    if mode == 'nn':
        m, kd, n = ar, ac, bc_
        assert br_ == kd
    elif mode == 'nt':
        m, kd, n = ar, ac, br_
        assert bc_ == kd
    else:
        m, kd, n = ac, ar, bc_
        assert br_ == kd

    def lim(op, is_col_dim):
        return op[0].shape[-1] if (op[1] == 'col' and is_col_dim) else 1 << 30

    tm = _pick(m, min(tm, lim(a, mode == 'tn')))
    tn = _pick(n, min(tn, lim(b, mode != 'nt'), (n // N_DEV) if out_col else 1 << 30))
    tk = _pick(kd, min(tk, lim(a, mode != 'tn'), lim(b, mode == 'nt')))
    nk = kd // tk

    def vmem_bytes(tm_):
        out_bytes = sum(jnp.dtype(dt).itemsize for dt in (epi[2] if epi is not None else [out_dtype]))
        extra_bytes = sum(e.dtype.itemsize for e in (epi[1] if epi is not None else [])) + (add.dtype.itemsize if add is not None else 0)
        blocks = tm_ * tk * a[0].dtype.itemsize + tk * tn * b[0].dtype.itemsize + tm_ * tn * (out_bytes + extra_bytes)
        return 2 * blocks + tm_ * tn * 4 * (2 if nk > 1 else 1)

    while vmem_bytes(tm) > MM_VMEM_BUDGET and tm % 256 == 0 and tm > 256:
        tm //= 2
    if mode == 'nn':
        a_spec = _opspec(a, tm, tk, lambda i, j, k: (i, k))
        b_spec = _opspec(b, tk, tn, lambda i, j, k: (k, j))
        dims = ((1,), (0,))
    elif mode == 'nt':
        a_spec = _opspec(a, tm, tk, lambda i, j, k: (i, k))
        b_spec = _opspec(b, tn, tk, lambda i, j, k: (j, k))
        dims = ((1,), (1,))
    else:
        a_spec = _opspec(a, tk, tm, lambda i, j, k: (k, i))
        b_spec = _opspec(b, tk, tn, lambda i, j, k: (k, j))
        dims = ((0,), (0,))
    if out_col:
        out_shape = jax.ShapeDtypeStruct((N_DEV, m, n // N_DEV), out_dtype)
        out_spec = _opspec((out_shape, 'col', None), tm, tn, lambda i, j, k: (i, j))
    else:
        out_shape = jax.ShapeDtypeStruct((m, n), out_dtype)
        out_spec = pl.BlockSpec((tm, tn), lambda i, j, k: (i, j))
    has_add = add is not None
    epi_fn, epi_extra, epi_dtypes = epi if epi is not None else (None, [], [out_dtype])
    n_in = 2 + has_add + len(epi_extra)
    n_out = len(epi_dtypes)

    def body(*refs):
        a_ref, b_ref = refs[0], refs[1]
        add_ref = refs[2] if has_add else None
        extra_refs = refs[2 + has_add:n_in]
        o_refs = refs[n_in:n_in + n_out]

        def emit(acc):
            outs = (acc,) if epi_fn is None else epi_fn(acc, *[r[...] for r in extra_refs])
            for o_ref, o in zip(o_refs, outs):
                o_ref[...] = o.astype(o_ref.dtype)

        part = _bdot(a_ref[...], b_ref[...], dims)
        if nk == 1:
            emit(part + add_ref[...].astype(F32) if has_add else part)
            return
        acc_ref = refs[n_in + n_out]
        k = pl.program_id(2)

        @pl.when(k == 0)
        def _():
            acc_ref[...] = part + add_ref[...].astype(F32) if has_add else part

        @pl.when(k > 0)
        def _():
            acc_ref[...] += part

        @pl.when(k == nk - 1)
        def _():
            emit(acc_ref[...])

    block = pl.BlockSpec((tm, tn), lambda i, j, k: (i, j))
    in_specs = [a_spec, b_spec] + [block] * (has_add + len(epi_extra))
    args = [a[0], b[0]] + ([add] if has_add else []) + list(epi_extra)
    if epi is not None:
        assert not out_col
        out_spec = [block] * n_out
        out_shape = [jax.ShapeDtypeStruct((m, n), dt) for dt in epi_dtypes]
    return pl.pallas_call(
        body, name=name, grid=(m // tm, n // tn, nk), in_specs=in_specs, out_specs=out_spec, out_shape=out_shape,
        scratch_shapes=[pltpu.VMEM((tm, tn), F32)] if nk > 1 else [],
        compiler_params=pltpu.CompilerParams(dimension_semantics=("parallel", "parallel", "arbitrary")),
    )(*args)


def _rowwise(f, rows, consts, row_outs, acc_outs=(), *, tr, name):
    nr, nc, nro = len(rows), len(consts), len(row_outs)
    first = rows[0][0]
    t = first.shape[-2]
    assert t % tr == 0
    in_specs = []
    for spec in rows:
        arr, cb, w = spec[:3]
        lead = spec[3] if len(spec) > 3 else None
        if arr.ndim == 2:
            in_specs.append(pl.BlockSpec((tr, w), functools.partial(lambda i, cb: (i, cb), cb=cb)))
        elif lead is not None:
            in_specs.append(pl.BlockSpec((None, tr, w), functools.partial(lambda i, cb, lead: (lead, i, cb), cb=cb, lead=lead)))
        else:
            in_specs.append(pl.BlockSpec((arr.shape[0], tr, w), functools.partial(lambda i, cb: (0, i, cb), cb=cb)))
    for cst in consts:
        in_specs.append(pl.BlockSpec(cst.shape, functools.partial(lambda i, nd: (0,) * nd, nd=cst.ndim)))
    out_specs = [pl.BlockSpec((tr, w), lambda i: (i, 0)) for w, _ in row_outs]
    out_shape = [jax.ShapeDtypeStruct((t, w), dt) for w, dt in row_outs]
    for shp in acc_outs:
        out_specs.append(pl.BlockSpec(shp, functools.partial(lambda i, nd: (0,) * nd, nd=len(shp))))
        out_shape.append(jax.ShapeDtypeStruct(shp, F32))

    def body(*refs):
        ins = [r[...] for r in refs[:nr + nc]]
        ro = refs[nr + nc:nr + nc + nro]
        ao = refs[nr + nc + nro:]
        outs, accs = f(*ins)
        for o_ref, o in zip(ro, outs):
            o_ref[...] = o.astype(o_ref.dtype)
        if ao:
            i = pl.program_id(0)

            @pl.when(i == 0)
            def _():
                for a_ref, acc in zip(ao, accs):
                    a_ref[...] = acc

            @pl.when(i > 0)
            def _():
                for a_ref, acc in zip(ao, accs):
                    a_ref[...] += acc

    res = pl.pallas_call(
        body, name=name, grid=(t // tr,), in_specs=in_specs, out_specs=out_specs, out_shape=out_shape,
        compiler_params=pltpu.CompilerParams(dimension_semantics=("arbitrary",)),
    )(*[s[0] for s in rows], *consts)
    return res


def _vjp_rows(f, n_prim, n_rows_grad):
    def g(*args):
        prim, cots = args[:n_prim], args[n_prim:]
        outs, vjp = jax.vjp(f, *prim)
        grads = vjp(tuple(c.astype(o.dtype) for c, o in zip(cots, outs)))
        return tuple(grads[:n_rows_grad]), tuple(grads[n_rows_grad:])
    return g


def _rms(x, g):
    return x * lax.rsqrt(jnp.mean(x * x, axis=-1, keepdims=True) + EPS) * g


def _f_rms(h, g):
    return (_rms(h, g),)


def _f_rms_bwd(h, g, du, dres):
    _, vjp = jax.vjp(_f_rms, h, g)
    dh, dg = vjp((du.astype(F32),))
    return (dh + dres, dh + dres), (dg,)


def _f_lnsilu(x, g, b):
    mu = jnp.mean(x, axis=-1, keepdims=True)
    xc = x - mu
    y = xc * lax.rsqrt(jnp.mean(xc * xc, axis=-1, keepdims=True) + EPS) * g + b
    return (jax.nn.silu(y),)


def _f_ssdgate(y, z, g):
    y = y * jax.nn.silu(z)
    w = SSD_INNER // SSD_GROUPS
    return (jnp.concatenate([_rms(y[:, i * w:(i + 1) * w], g[:, i * w:(i + 1) * w]) for i in range(SSD_GROUPS)], axis=-1),)


def _f_merge(pg, ya, yb, yc, gb):
    out = 0.0
    for i, yi in enumerate((ya, yb, yc)):
        out = out + jax.nn.sigmoid(pg[:, i * D_MODEL:(i + 1) * D_MODEL] + gb[i:i + 1, :]) * yi
    return (out,)


def _f_relu2(a):
    return (jnp.square(jnp.maximum(a, 0.0)),)


def _f_xattn(q, kv, qg, kg):
    outs = []
    for h in range(XATTN_HEADS):
        sl = slice(h * XATTN_HEAD_DIM, (h + 1) * XATTN_HEAD_DIM)
        qh = _rms(q[:, sl], qg)
        kh = _rms(kv[:, sl], kg)
        vh = kv[:, D_MODEL + h * XATTN_HEAD_DIM:D_MODEL + (h + 1) * XATTN_HEAD_DIM]
        s = _bdot(qh, kh, ((1,), (1,))) * (XATTN_HEAD_DIM ** -0.5)
        p = jnp.exp(s - jnp.max(s, axis=-1, keepdims=True))
        p = p * (1.0 / jnp.sum(p, axis=-1, keepdims=True))
        outs.append(_bdot(p, vh, ((1,), (0,))))
    return (jnp.concatenate(outs, axis=-1),)


def _f_loss(y, tgt):
    err = y - tgt
    per_row = jnp.sum(err * err, axis=-1, keepdims=True) * (0.5 / D_MODEL)
    loss = jnp.sum(per_row, axis=0, keepdims=True)
    dy = err * (1.0 / D_MODEL)
    return (dy, dy), (jnp.broadcast_to(loss, (1, LANES)),)


def _adam_core(w, g, m, v):
    m = ADAM_B1 * m + (1.0 - ADAM_B1) * g
    v = ADAM_B2 * v + (1.0 - ADAM_B2) * jnp.square(g)
    m_hat = m / (1.0 - ADAM_B1 ** ADAM_STEP)
    v_hat = v / (1.0 - ADAM_B2 ** ADAM_STEP)
    delta = -ADAM_LR * (m_hat / (jnp.sqrt(v_hat) + ADAM_EPS) + ADAM_WD * w)
    return delta, m, v


def _sum_slots(g8):
    g = g8[0].astype(F32)
    for s in range(1, N_DEV):
        g = g + g8[s].astype(F32)
    return g


def _f_adam_slots(g8, w, m, v):
    g = _sum_slots(g8)
    return (g,) + _adam_core(w, g, m, v), ()


def _f_sum_slots(g8):
    return (_sum_slots(g8),), ()


def _f_adam(g, w, m, v):
    return _adam_core(w, g, m, v), ()


def _conv_chunk(t):
    return 256 if t % 256 == 0 else t


def _conv_fwd(srcs, w, b, *, kk, pre_glu, post_silu, name):
    t = srcs[0][0].shape[0]
    c = w.shape[-1]
    tt = _conv_chunk(t)
    ns = len(srcs)

    def body(*refs):
        w_ref, b_ref, o_ref, pad_ref = refs[ns:]
        if pre_glu:
            xin = refs[0][...] * jax.nn.sigmoid(refs[1][...])
        else:
            xin = refs[0][...]
        pad_ref[0:CONV_PAD, :] = jnp.zeros((CONV_PAD, LANES), F32)
        pad_ref[CONV_PAD:, :] = xin

        def chunk(i, carry):
            base = pl.multiple_of(i * tt, tt)
            acc = jnp.broadcast_to(b_ref[...], (tt, LANES))
            for j in range(kk):
                acc = acc + pad_ref[pl.ds(base + CONV_PAD - (kk - 1) + j, tt), :] * w_ref[j:j + 1, :]
            o_ref[pl.ds(base, tt), :] = jax.nn.silu(acc) if post_silu else acc
            return carry
        lax.fori_loop(0, t // tt, chunk, 0)

    in_specs = [pl.BlockSpec((t, LANES), functools.partial(lambda i, off: (0, off + i), off=off)) for _, off in srcs]
    in_specs += [pl.BlockSpec((kk, LANES), lambda i: (0, i)), pl.BlockSpec((1, LANES), lambda i: (0, i))]
    return pl.pallas_call(
        body, name=name, grid=(c // LANES,), in_specs=in_specs, out_specs=pl.BlockSpec((t, LANES), lambda i: (0, i)),
        out_shape=jax.ShapeDtypeStruct((t, c), F32), scratch_shapes=[pltpu.VMEM((t + CONV_PAD, LANES), F32)],
        compiler_params=pltpu.CompilerParams(dimension_semantics=("parallel",)),
    )(*[s[0] for s in srcs], w, b)


def _conv_bwd(srcs, w, b, dys, *, kk, pre_glu, post_silu, name, dx_dtype):
    t = srcs[0][0].shape[0]
    c = w.shape[-1]
    tt = _conv_chunk(t)
    ns, nd = len(srcs), len(dys)

    def body(*refs):
        src_refs = refs[:ns]
        dy_refs = refs[ns:ns + nd]
        w_ref, b_ref = refs[ns + nd:ns + nd + 2]
        outs = refs[ns + nd + 2:]
        dx_refs, dw_ref, db_ref = outs[:ns], outs[ns], outs[ns + 1]
        pad_ref, dpad_ref = outs[ns + 2:]
        cb = pl.program_id(0)
        if pre_glu:
            a_in = src_refs[0][...]
            sg = jax.nn.sigmoid(src_refs[1][...])
            xin = a_in * sg
        else:
            xin = src_refs[0][...]
        pad_ref[0:CONV_PAD, :] = jnp.zeros((CONV_PAD, LANES), F32)
        pad_ref[CONV_PAD:, :] = xin
        dpad_ref[t:, :] = jnp.zeros((CONV_PAD, LANES), F32)
        dw_ref[...] = jnp.zeros_like(dw_ref)
        db_ref[...] = jnp.zeros_like(db_ref)

        def load_dy(base):
            dy = dy_refs[0][pl.ds(base, tt), :].astype(F32)
            for (_, first, _n), r in zip(dys[1:], dy_refs[1:]):
                dy = jnp.where(cb >= first, r[pl.ds(base, tt), :].astype(F32), dy)
            return dy

        def chunk1(i, carry):
            base = pl.multiple_of(i * tt, tt)
            dy = load_dy(base)
            if post_silu:
                acc = jnp.broadcast_to(b_ref[...], (tt, LANES))
                for j in range(kk):
                    acc = acc + pad_ref[pl.ds(base + CONV_PAD - (kk - 1) + j, tt), :] * w_ref[j:j + 1, :]
                s = jax.nn.sigmoid(acc)
                dy = dy * (s * (1.0 + acc * (1.0 - s)))
            dpad_ref[pl.ds(base, tt), :] = dy
            db_ref[...] += jnp.sum(dy, axis=0, keepdims=True)
            for j in range(kk):
                dw_ref[j:j + 1, :] += jnp.sum(dy * pad_ref[pl.ds(base + CONV_PAD - (kk - 1) + j, tt), :], axis=0, keepdims=True)
            return carry
        lax.fori_loop(0, t // tt, chunk1, 0)

        def chunk2(i, carry):
            base = pl.multiple_of(i * tt, tt)
            acc = jnp.zeros((tt, LANES), F32)
            for j in range(kk):
                acc = acc + dpad_ref[pl.ds(base + (kk - 1) - j, tt), :] * w_ref[j:j + 1, :]
            if pre_glu:
                a_c = src_refs[0][pl.ds(base, tt), :]
                s_c = jax.nn.sigmoid(src_refs[1][pl.ds(base, tt), :])
                dx_refs[0][pl.ds(base, tt), :] = (acc * s_c).astype(dx_dtype)
                dx_refs[1][pl.ds(base, tt), :] = (acc * a_c * s_c * (1.0 - s_c)).astype(dx_dtype)
            else:
                dx_refs[0][pl.ds(base, tt), :] = acc.astype(dx_dtype)
            return carry
        lax.fori_loop(0, t // tt, chunk2, 0)

    in_specs = [pl.BlockSpec((t, LANES), functools.partial(lambda i, off: (0, off + i), off=off)) for _, off in srcs]
    for _, first, n in dys:
        in_specs.append(pl.BlockSpec((t, LANES), functools.partial(lambda i, first, n: (0, jnp.clip(i - first, 0, n - 1)), first=first, n=n)))
    in_specs += [pl.BlockSpec((kk, LANES), lambda i: (0, i)), pl.BlockSpec((1, LANES), lambda i: (0, i))]
    out_specs = [pl.BlockSpec((t, LANES), lambda i: (0, i)) for _ in srcs]
    out_specs += [pl.BlockSpec((kk, LANES), lambda i: (0, i)), pl.BlockSpec((1, LANES), lambda i: (0, i))]
    out_shape = [jax.ShapeDtypeStruct((t, c), dx_dtype) for _ in srcs]
    out_shape += [jax.ShapeDtypeStruct((kk, c), F32), jax.ShapeDtypeStruct((1, c), F32)]
    return pl.pallas_call(
        body, name=name, grid=(c // LANES,), in_specs=in_specs, out_specs=out_specs, out_shape=out_shape,
        scratch_shapes=[pltpu.VMEM((t + CONV_PAD, LANES), F32), pltpu.VMEM((t + CONV_PAD, LANES), F32)],
        compiler_params=pltpu.CompilerParams(dimension_semantics=("parallel",)),
    )(*[s[0] for s in srcs], *[d[0] for d in dys], w, b)


def _ssd_chunk(xbc, dtfull, s_in, dt_bias, a_log, dskip):
    q = xbc.shape[0]
    gw = SSD_INNER // SSD_GROUPS
    x = xbc[:, :SSD_INNER]
    dt_all = jax.nn.softplus(dtfull + dt_bias)
    da_all = dt_all * (-jnp.exp(a_log))
    row = lax.broadcasted_iota(jnp.int32, (q, q), 0)
    col = lax.broadcasted_iota(jnp.int32, (q, q), 1)
    causal = row >= col
    cs = _select_dot(causal.astype(jnp.bfloat16), da_all)
    cs_last = cs[q - 1:q, :]
    per_head = jnp.concatenate([dt_all, jnp.exp(cs), jnp.exp(cs_last - cs), jnp.broadcast_to(dskip, (8, LANES))], axis=0)
    wide = _spread_dot(per_head, _one_hot_groups(LANES, SSD_INNER, SSD_HEAD_DIM))
    xdt = x * wide[0:q]
    from_start, to_end, d_wide = wide[q:2 * q], wide[2 * q:3 * q], wide[3 * q:3 * q + 1]
    xdt_end = xdt * to_end
    chunk_decay = jnp.exp(cs_last)
    decay_rows = jnp.concatenate([jnp.broadcast_to(chunk_decay[:, h:h + 1], (SSD_HEAD_DIM, 1)) for h in range(SSD_HEADS)], axis=0)
    cs_t = jnp.transpose(cs)
    y_off, new_states, y_diag = [], [], []
    for g in range(SSD_GROUPS):
        bm = xbc[:, SSD_INNER + g * SSD_STATE:SSD_INNER + (g + 1) * SSD_STATE]
        cm = xbc[:, SSD_INNER + (SSD_GROUPS + g) * SSD_STATE:SSD_INNER + (SSD_GROUPS + g + 1) * SSD_STATE]
        cb = _bdot(cm, bm, ((1,), (1,)))
        y_off.append(_bdot(cm, s_in[g * gw:(g + 1) * gw, :], ((1,), (1,))))
        new_states.append(_bdot(xdt_end[:, g * gw:(g + 1) * gw], bm, ((0,), (0,))))
        for j in range(HEADS_PER_GROUP):
            h = g * HEADS_PER_GROUP + j
            diff = cs[:, h:h + 1] - cs_t[h:h + 1, :]
            decay = jnp.where(causal, jnp.exp(jnp.where(causal, diff, 0.0)), 0.0)
            y_diag.append(_bdot(cb * decay, xdt[:, h * SSD_HEAD_DIM:(h + 1) * SSD_HEAD_DIM], ((1,), (0,))))
    s_out = s_in * decay_rows + jnp.concatenate(new_states, axis=0)
    y = jnp.concatenate(y_diag, axis=-1) + jnp.concatenate(y_off, axis=-1) * from_start + x * d_wide
    return y, s_out


def _ssd_fwd(xbc, pdt, dt_bias, a_log, dskip, *, name, rider=None):
    t = xbc.shape[0]
    nc = t // SSD_CHUNK
    ride = _Ride(rider, 5, 2, 1)

    def body(*refs):
        (x_ref, dt_ref, tb_ref, al_ref, d_ref), (y_ref, s_ref), (state_ref,) = ride.split(refs)
        c = pl.program_id(0)
        ride.start(refs, c)
        ride.relay(refs, c, nc)

        @pl.when(c == 0)
        def _():
            state_ref[...] = jnp.zeros_like(state_ref)
        s_in = state_ref[...]
        s_ref[...] = s_in
        y, s_out = _ssd_chunk(x_ref[...], dt_ref[...], s_in, tb_ref[...], al_ref[...], d_ref[...])
        y_ref[...] = y
        state_ref[...] = s_out
        ride.finish(refs, c, nc)

    vec = pl.BlockSpec((1, LANES), lambda c: (0, 0))
    res = pl.pallas_call(
        body, name=name, grid=(nc,),
        in_specs=[pl.BlockSpec((SSD_CHUNK, SSD_XBC), lambda c: (c, 0)), pl.BlockSpec((SSD_CHUNK, LANES), lambda c: (c, 0)), vec, vec, vec] + ride.in_specs,
        out_specs=[pl.BlockSpec((SSD_CHUNK, SSD_INNER), lambda c: (c, 0)), pl.BlockSpec((None, SSD_INNER, SSD_STATE), lambda c: (c, 0, 0))] + ride.out_specs,
        out_shape=[jax.ShapeDtypeStruct((t, SSD_INNER), F32), jax.ShapeDtypeStruct((nc, SSD_INNER, SSD_STATE), F32)] + ride.out_shape,
        scratch_shapes=[pltpu.VMEM((SSD_INNER, SSD_STATE), F32)] + ride.scratch,
        compiler_params=pltpu.CompilerParams(dimension_semantics=("arbitrary",)),
    )(xbc, pdt, dt_bias, a_log, dskip, *ride.args)
    return res[:2], res[2:]


def _ssd_bwd(xbc, pdt, states, dy, dt_bias, a_log, dskip, *, name, rider=None):
    t = xbc.shape[0]
    nc = t // SSD_CHUNK
    rev = lambda c: nc - 1 - c
    ride = _Ride(rider, 7, 5, 1)

    def body(*refs):
        (x_ref, dt_ref, s_ref, dy_ref, tb_ref, al_ref, d_ref), (dx_ref, ddt_ref, dtb_ref, dal_ref, dd_ref), (dstate_ref,) = ride.split(refs)
        c = pl.program_id(0)
        ride.start(refs, c)
        ride.relay(refs, c, nc)

        @pl.when(c == 0)
        def _():
            dstate_ref[...] = jnp.zeros_like(dstate_ref)
            dtb_ref[...] = jnp.zeros_like(dtb_ref)
            dal_ref[...] = jnp.zeros_like(dal_ref)
            dd_ref[...] = jnp.zeros_like(dd_ref)
        _, vjp = jax.vjp(_ssd_chunk, x_ref[...], dt_ref[...], s_ref[...], tb_ref[...], al_ref[...], d_ref[...])
        dx, ddt, ds_in, dtb, dal, dd = vjp((dy_ref[...].astype(F32), dstate_ref[...]))
        dx_ref[...] = dx
        ddt_ref[...] = ddt
        dstate_ref[...] = ds_in
        dtb_ref[...] += dtb
        dal_ref[...] += dal
        dd_ref[...] += dd
        ride.finish(refs, c, nc)

    vec = pl.BlockSpec((1, LANES), lambda c: (0, 0))
    in_specs = [pl.BlockSpec((SSD_CHUNK, SSD_XBC), lambda c: (rev(c), 0)), pl.BlockSpec((SSD_CHUNK, LANES), lambda c: (rev(c), 0)),
                pl.BlockSpec((None, SSD_INNER, SSD_STATE), lambda c: (rev(c), 0, 0)), pl.BlockSpec((SSD_CHUNK, SSD_INNER), lambda c: (rev(c), 0)),
                vec, vec, vec]
    out_specs = [pl.BlockSpec((SSD_CHUNK, SSD_XBC), lambda c: (rev(c), 0)), pl.BlockSpec((SSD_CHUNK, LANES), lambda c: (rev(c), 0)), vec, vec, vec]
    out_shape = [jax.ShapeDtypeStruct((t, SSD_XBC), F32), jax.ShapeDtypeStruct((t, LANES), F32)] + [jax.ShapeDtypeStruct((1, LANES), F32)] * 3
    res = pl.pallas_call(
        body, name=name, grid=(nc,), in_specs=in_specs + ride.in_specs, out_specs=out_specs + ride.out_specs,
        out_shape=out_shape + ride.out_shape, scratch_shapes=[pltpu.VMEM((SSD_INNER, SSD_STATE), F32)] + ride.scratch,
        compiler_params=pltpu.CompilerParams(dimension_semantics=("arbitrary",)),
    )(xbc, pdt, states, dy, dt_bias, a_log, dskip, *ride.args)
    return res[:5], res[5:]


def _rel_buckets():
    qi = np.arange(ATTN_BLOCK)[:, None] + ATTN_BLOCK
    kj = np.arange(2 * ATTN_BLOCK)[None, :]
    dist = qi - kj
    max_exact = REL_BUCKETS // 2
    d = np.maximum(dist, 1).astype(np.float32)
    large = max_exact + (np.log(d / np.float32(max_exact)) / np.float32(math.log(REL_MAX_DIST / max_exact))
                         * np.float32(REL_BUCKETS - max_exact)).astype(np.int32)
    large = np.minimum(large, REL_BUCKETS - 1)
    return np.where(dist < max_exact, np.maximum(dist, 0), large).astype(np.int32)


def _onehot_buckets(bucket_ref):
    n = bucket_ref.shape[-1]
    return (lax.broadcasted_iota(jnp.int32, (REL_BUCKETS, n), 0) == bucket_ref[...]).astype(F32)


def _band_bias(rel_table_t, buckets):
    n = buckets.shape[-1]

    def body(rt_ref, bk_ref, o_ref):
        o_ref[...] = _hdot(rt_ref[...], _onehot_buckets(bk_ref), ((1,), (0,)))
    return pl.pallas_call(body, name="band_bias", out_shape=jax.ShapeDtypeStruct((ATTN_HEADS, n), F32))(rel_table_t, buckets)


def _band_bias_bwd(dbias, buckets):
    def body(db_ref, bk_ref, o_ref):
        d = db_ref[0]
        for layer in range(1, db_ref.shape[0]):
            d = d + db_ref[layer]
        o_ref[...] = _hdot(d, _onehot_buckets(bk_ref), ((1,), (1,)))
    return pl.pallas_call(body, name="band_bias_bwd", out_shape=jax.ShapeDtypeStruct((ATTN_HEADS, REL_BUCKETS), F32))(dbias, buckets)


def _heads_rms_lanes(x, g):
    return [_rms(x[:, h * ATTN_HEAD_DIM:(h + 1) * ATTN_HEAD_DIM], g) for h in range(x.shape[-1] // ATTN_HEAD_DIM)]


def _attn_block(q, kvp, kvc, bias, sinks, qg, kg, first, norm=_heads_rms):
    qn = q.shape[0]
    rows = ATTN_REP * qn
    ri = lax.broadcasted_iota(jnp.int32, (rows, 2 * qn), 0) & (qn - 1)
    cj = lax.broadcasted_iota(jnp.int32, (rows, 2 * qn), 1)
    jj = cj & (qn - 1)
    no_prev = jnp.where(first, qn, 0)
    mask = ((cj < qn) & (jj > ri + no_prev)) | ((cj >= qn) & (jj <= ri))
    kvd = ATTN_KV_HEADS * ATTN_HEAD_DIM
    q_normed = norm(q, qg)
    kn_prev, kn_cur = norm(kvp[:, :kvd], kg), norm(kvc[:, :kvd], kg)
    outs = []
    for g in range(ATTN_KV_HEADS):
        vsl = slice(kvd + g * ATTN_HEAD_DIM, kvd + (g + 1) * ATTN_HEAD_DIM)
        qs = jnp.concatenate(q_normed[g * ATTN_REP:(g + 1) * ATTN_REP], axis=0)
        kb = jnp.concatenate([kn_prev[g], kn_cur[g]], axis=0)
        vb = jnp.concatenate([kvp[:, vsl], kvc[:, vsl]], axis=0)
        logits = _bdot(qs, kb, ((1,), (1,))) * (ATTN_HEAD_DIM ** -0.5) + bias[g]
        logits = jnp.where(mask, logits, NEG_INF)
        sink = jnp.concatenate([jnp.broadcast_to(sinks[:, g * ATTN_REP + j:g * ATTN_REP + j + 1], (qn, 1)) for j in range(ATTN_REP)], axis=0)
        m = jnp.maximum(jnp.max(logits, axis=-1, keepdims=True), sink)
        pexp = jnp.exp(logits - m)
        probs = pexp * (1.0 / (jnp.sum(pexp, axis=-1, keepdims=True) + jnp.exp(sink - m)))
        o = _bdot(probs, vb, ((1,), (0,)))
        outs += [o[j * qn:(j + 1) * qn, :] for j in range(ATTN_REP)]
    return jnp.concatenate(outs, axis=-1)


def _attn_specs(nmap):
    qd, kvw = ATTN_HEADS * ATTN_HEAD_DIM, 2 * ATTN_KV_HEADS * ATTN_HEAD_DIM
    full = lambda shp: pl.BlockSpec(shp, lambda i: (0,) * len(shp))
    return [
        pl.BlockSpec((ATTN_BLOCK, qd), lambda i: (nmap(i), 0)),
        pl.BlockSpec((ATTN_BLOCK, kvw), lambda i: (jnp.maximum(nmap(i) - 1, 0), qd // kvw)),
        pl.BlockSpec((ATTN_BLOCK, kvw), lambda i: (nmap(i), qd // kvw)),
        full((ATTN_KV_HEADS, ATTN_REP * ATTN_BLOCK, 2 * ATTN_BLOCK)), full((1, ATTN_HEADS)), full((1, ATTN_HEAD_DIM)), full((1, ATTN_HEAD_DIM)),
    ]


def _attn_fwd(pqkv, bias, sinks, qg, kg, *, name, rider=None):
    t = pqkv.shape[0]
    nb = t // ATTN_BLOCK
    qd = ATTN_HEADS * ATTN_HEAD_DIM
    ride = _Ride(rider, 7, 1, 0)

    def body(*refs):
        (q_ref, kvp_ref, kvc_ref, bias_ref, sk_ref, qg_ref, kg_ref), (o_ref,), _ = ride.split(refs)
        i = pl.program_id(0)
        ride.start(refs, i)
        ride.relay(refs, i, nb)
        o_ref[...] = _attn_block(q_ref[...], kvp_ref[...], kvc_ref[...], bias_ref[...], sk_ref[...], qg_ref[...], kg_ref[...], i == 0,
                                 norm=_heads_rms_lanes).astype(o_ref.dtype)
        ride.finish(refs, i, nb)

    res = pl.pallas_call(
        body, name=name, grid=(nb,), in_specs=_attn_specs(lambda i: i) + ride.in_specs,
        out_specs=[pl.BlockSpec((ATTN_BLOCK, qd), lambda i: (i, 0))] + ride.out_specs,
        out_shape=[jax.ShapeDtypeStruct((t, qd), BF16)] + ride.out_shape, scratch_shapes=ride.scratch,
        compiler_params=pltpu.CompilerParams(dimension_semantics=("arbitrary",)),
    )(pqkv, pqkv, pqkv, bias, sinks, qg, kg, *ride.args)
    return res[0], res[1:]


def _attn_bwd(pqkv, do, bias, sinks, qg, kg, *, name, rider=None):
    t = pqkv.shape[0]
    nb = t // ATTN_BLOCK
    qd, kvw = ATTN_HEADS * ATTN_HEAD_DIM, 2 * ATTN_KV_HEADS * ATTN_HEAD_DIM
    rev = lambda i: nb - 1 - i
    ride = _Ride(rider, 8, 5, 1)

    def body(*refs):
        (q_ref, kvp_ref, kvc_ref, bias_ref, sk_ref, qg_ref, kg_ref, do_ref), (dqkv_ref, dbias_ref, dsk_ref, dqg_ref, dkg_ref), (carry_ref,) = ride.split(refs)
        i = pl.program_id(0)
        ride.start(refs, i)
        ride.relay(refs, i, nb)
        first = rev(i) == 0
        f = functools.partial(_attn_block, first=first)
        _, vjp = jax.vjp(f, q_ref[...], kvp_ref[...], kvc_ref[...], bias_ref[...], sk_ref[...], qg_ref[...], kg_ref[...])
        dq, dkvp, dkvc, dbias, dsk, dqg, dkg = vjp(do_ref[...].astype(F32))

        @pl.when(i == 0)
        def _():
            carry_ref[...] = jnp.zeros_like(carry_ref)
            dbias_ref[...] = jnp.zeros_like(dbias_ref)
            dsk_ref[...] = jnp.zeros_like(dsk_ref)
            dqg_ref[...] = jnp.zeros_like(dqg_ref)
            dkg_ref[...] = jnp.zeros_like(dkg_ref)
        dqkv_ref[:, 0:qd] = dq.astype(dqkv_ref.dtype)
        dqkv_ref[:, qd:] = (dkvc + carry_ref[...]).astype(dqkv_ref.dtype)
        carry_ref[...] = dkvp
        dbias_ref[...] += dbias
        dsk_ref[...] += dsk
        dqg_ref[...] += dqg
        dkg_ref[...] += dkg
        ride.finish(refs, i, nb)

    full = lambda shp: pl.BlockSpec(shp, lambda i: (0,) * len(shp))
    bshape = (ATTN_KV_HEADS, ATTN_REP * ATTN_BLOCK, 2 * ATTN_BLOCK)
    res = pl.pallas_call(
        body, name=name, grid=(nb,), in_specs=_attn_specs(rev) + [pl.BlockSpec((ATTN_BLOCK, qd), lambda i: (rev(i), 0))] + ride.in_specs,
        out_specs=[pl.BlockSpec((ATTN_BLOCK, qd + kvw), lambda i: (rev(i), 0)), full(bshape), full((1, ATTN_HEADS)),
                   full((1, ATTN_HEAD_DIM)), full((1, ATTN_HEAD_DIM))] + ride.out_specs,
        out_shape=[jax.ShapeDtypeStruct((t, qd + kvw), BF16), jax.ShapeDtypeStruct(bshape, F32), jax.ShapeDtypeStruct((1, ATTN_HEADS), F32),
                   jax.ShapeDtypeStruct((1, ATTN_HEAD_DIM), F32), jax.ShapeDtypeStruct((1, ATTN_HEAD_DIM), F32)] + ride.out_shape,
        scratch_shapes=[pltpu.VMEM((ATTN_BLOCK, kvw), F32)] + ride.scratch,
        compiler_params=pltpu.CompilerParams(dimension_semantics=("arbitrary",)),
    )(pqkv, pqkv, pqkv, bias, sinks, qg, kg, do, *ride.args)
    return res[:5], res[5:]


def _dev_index(dev):
    return 4 * dev[0] + 2 * dev[1] + dev[2]


def _gather_ops(x_refs, o_refs, send_sems, recv_sems, local_sems, axes):
    n = len(x_refs)
    x, y, c = lax.axis_index("x"), lax.axis_index("y"), lax.axis_index("c")
    me, sibling = (x, y, c), (x, y, 1 - c)
    chips = [(1 - x, y), (x, 1 - y), (1 - x, 1 - y)]

    def slot(i, dev):
        idx = _dev_index(dev)
        return o_refs[i].at[idx] if axes[i] == 0 else o_refs[i].at[:, idx]

    def copy(i, k, block, to, src=None):
        return pltpu.make_async_remote_copy(
            src_ref=slot(i, block) if src is None else src, dst_ref=slot(i, block),
            send_sem=send_sems.at[i, k], recv_sem=recv_sems.at[i, k], device_id=to, device_id_type=MESH)

    mine = [pltpu.make_async_copy(x_refs[i], slot(i, me), local_sems.at[i]) for i in range(n)]
    first = []
    for i in range(n):
        first.append(copy(i, 0, me, sibling, src=x_refs[i]))
        first += [copy(i, 1 + j, me, (*chip, c), src=x_refs[i]) for j, chip in enumerate(chips)]

    def start():
        for cp in mine + first:
            cp.start()

    passed = [copy(i, 4 + j, (*chip, c), sibling) for j, chip in enumerate(chips) for i in range(n)]

    def relay():
        for j, chip in enumerate(chips):
            for i in range(n):
                copy(i, 1 + j, (*chip, c), me).wait_recv()
                passed[j * n + i].start()

    def finish():
        for i in range(n):
            copy(i, 0, sibling, me).wait_recv()
        for j, chip in enumerate(chips):
            for i in range(n):
                copy(i, 4 + j, (*chip, 1 - c), me).wait_recv()
        for cp in first + passed:
            cp.wait_send()
        for cp in mine:
            cp.wait()

    return start, relay, finish


def _exchange_ops(g_refs, o_refs, send_sems, recv_sems, local_sems):
    n = len(g_refs)
    x, y, c = lax.axis_index("x"), lax.axis_index("y"), lax.axis_index("c")
    me = _dev_index((x, y, c))
    peers = [(x ^ ((k >> 2) & 1), y ^ ((k >> 1) & 1), c ^ (k & 1)) for k in range(1, N_DEV)]

    def copy(i, k):
        peer = peers[k]
        return pltpu.make_async_remote_copy(
            src_ref=g_refs[i].at[_dev_index(peer)], dst_ref=o_refs[i].at[me],
            send_sem=send_sems.at[i, k], recv_sem=recv_sems.at[i, k], device_id=peer, device_id_type=MESH)

    def arrival(i, k):
        peer = peers[k]
        return pltpu.make_async_remote_copy(
            src_ref=g_refs[i].at[me], dst_ref=o_refs[i].at[_dev_index(peer)],
            send_sem=send_sems.at[i, k], recv_sem=recv_sems.at[i, k], device_id=peer, device_id_type=MESH)

    mine = [pltpu.make_async_copy(g_refs[i].at[me], o_refs[i].at[me], local_sems.at[i]) for i in range(n)]
    sends = [copy(i, k) for i in range(n) for k in range(N_DEV - 1)]

    def start():
        for cp in mine + sends:
            cp.start()

    def finish():
        for i in range(n):
            for k in range(N_DEV - 1):
                arrival(i, k).wait_recv()
        for cp in sends:
            cp.wait_send()
        for cp in mine:
            cp.wait()

    return start, lambda: None, finish


class _Ride:
    def __init__(self, rider, n_in, n_out, n_scr):
        self.rider, self.n_in, self.n_out, self.n_scr = rider, n_in, n_out, n_scr
        self.args = [] if rider is None else list(rider[1])
        n = self.n = len(self.args)
        hbm = pl.BlockSpec(memory_space=pltpu.HBM)
        self.in_specs, self.out_specs = [hbm] * n, [hbm] * n
        if rider is None:
            self.out_shape, self.scratch = [], []
            return
        if rider[0] == 'gather':
            self.out_shape = [jax.ShapeDtypeStruct(a.shape[:ax] + (N_DEV,) + a.shape[ax:], a.dtype) for a, ax in zip(self.args, rider[2])]
        else:
            self.out_shape = [jax.ShapeDtypeStruct(a.shape, a.dtype) for a in self.args]
        self.scratch = [pltpu.SemaphoreType.DMA((n, 7)), pltpu.SemaphoreType.DMA((n, 7)), pltpu.SemaphoreType.DMA((n,))]

    def split(self, refs):
        a = self.n_in
        c = a + self.n + self.n_out
        e = c + self.n
        return refs[:a], refs[a + self.n:c], refs[e:e + self.n_scr]

    def _ops(self, refs):
        a = self.n_in
        c = a + self.n + self.n_out
        e = c + self.n + self.n_scr
        x_refs, o_refs, sems = refs[a:a + self.n], refs[c:c + self.n], refs[e:e + 3]
        if self.rider[0] == 'gather':
            return _gather_ops(x_refs, o_refs, *sems, self.rider[2])
        return _exchange_ops(x_refs, o_refs, *sems)

    def start(self, refs, step):
        if self.rider is not None:
            pl.when(step == 0)(lambda: self._ops(refs)[0]())

    def relay(self, refs, step, n_steps):
        if self.rider is not None and self.rider[0] == 'gather':
            pl.when(step == n_steps - 1 - n_steps // 4)(lambda: self._ops(refs)[1]())

    def finish(self, refs, step, n_steps):
        if self.rider is not None:
            pl.when(step == n_steps - 1)(lambda: self._ops(refs)[2]())


def _comm_call(rider, *, name):
    ride = _Ride(rider, 0, 0, 0)

    def body(*refs):
        start, relay, finish = ride._ops(refs)
        start()
        relay()
        finish()

    return pl.pallas_call(body, name=name, in_specs=ride.in_specs, out_specs=ride.out_specs, out_shape=ride.out_shape,
                          scratch_shapes=ride.scratch)(*ride.args)


def _all_gather(xs, axes, *, name):
    return _comm_call(('gather', xs, axes), name=name)


def _exchange(gs, *, name):
    return _comm_call(('exchange', gs), name=name)


W_IN_SEGMENTS = [('a', 0, D_MODEL), ('ag', D_MODEL, D_MODEL), ('z', OFF_Z, OFF_XBC - OFF_Z), ('x', OFF_XBC, OFF_DT - OFF_XBC),
                 ('dt', OFF_DT, OFF_Q - OFF_DT), ('qkv', OFF_Q, OFF_GATE - OFF_Q), ('g', OFF_GATE, IN_COLS - OFF_GATE)]
W_IN_SHARD = IN_COLS // N_DEV


def _w_in_pieces():
    out = []
    for d in range(N_DEV):
        for k, (_, off, n) in enumerate(W_IN_SEGMENTS):
            lo, hi = max(d * W_IN_SHARD, off), min((d + 1) * W_IN_SHARD, off + n)
            if lo < hi:
                out.append((d, k, lo - d * W_IN_SHARD, lo - off, hi - lo))
    return out


def _seg_width(n):
    return -(-n // LANES) * LANES


def _w_in_segments(gathered, *, tr=256):
    rows = gathered.shape[1]
    widths = [_seg_width(n) for _, _, n in W_IN_SEGMENTS]

    def body(g_ref, *o_refs):
        for k, (_, _, n) in enumerate(W_IN_SEGMENTS):
            if widths[k] != n:
                o_refs[k][...] = jnp.zeros_like(o_refs[k])
        for d, k, at_dev, at_seg, n in _w_in_pieces():
            o_refs[k][:, at_seg:at_seg + n] = g_ref[d, :, at_dev:at_dev + n]

    return pl.pallas_call(
        body, name="w_in_segments", grid=(rows // tr,), in_specs=[pl.BlockSpec((N_DEV, tr, W_IN_SHARD), lambda i: (0, i, 0))],
        out_specs=[pl.BlockSpec((tr, w), lambda i: (i, 0)) for w in widths],
        out_shape=[jax.ShapeDtypeStruct((rows, w), gathered.dtype) for w in widths],
        compiler_params=pltpu.CompilerParams(dimension_semantics=("parallel",)),
    )(gathered)


def _w_in_by_device(segs, *, tr=256):
    rows = segs[0].shape[0]

    def body(*refs):
        o_ref = refs[-1]
        for d, k, at_dev, at_seg, n in _w_in_pieces():
            o_ref[d, :, at_dev:at_dev + n] = refs[k][:, at_seg:at_seg + n]

    return pl.pallas_call(
        body, name="w_in_by_device", grid=(rows // tr,), in_specs=[pl.BlockSpec((tr, s.shape[1]), lambda i: (i, 0)) for s in segs],
        out_specs=pl.BlockSpec((N_DEV, tr, W_IN_SHARD), lambda i: (0, i, 0)),
        out_shape=jax.ShapeDtypeStruct((N_DEV, rows, W_IN_SHARD), segs[0].dtype),
        compiler_params=pltpu.CompilerParams(dimension_semantics=("parallel",)),
    )(*segs)


def _pack(arrays):
    parts = []
    for a in arrays:
        flat = a.reshape(-1)
        pad = (-flat.shape[0]) % LANES
        if pad:
            flat = jnp.concatenate([flat, jnp.zeros((pad,), flat.dtype)])
        parts.append(flat.reshape(-1, LANES))
    return jnp.concatenate(parts, axis=0)


def _unpack(buf, shapes):
    out, row = [], 0
    for shp in shapes:
        size = int(np.prod(shp))
        rows = -(-size // LANES)
        out.append(buf[row:row + rows].reshape(-1)[:size].reshape(shp))
        row += rows
    return out


def _row_tile(rows, width, n_bufs):
    padded = -(-width // LANES) * LANES
    cap = max(16, (12 << 20) // (padded * 4 * n_bufs))
    if rows <= cap:
        return rows
    tr = (cap // 16) * 16
    while tr > 16 and rows % tr:
        tr -= 16
    return tr if rows % tr == 0 else rows


def _step(p, m, v, x, mem, loss_target):
    t = x.shape[1]
    h0 = x.reshape(t, D_MODEL)
    mem2 = mem.reshape(MEM_LEN, D_MODEL)
    tgt = loss_target.reshape(t, D_MODEL)
    tr = 256 if t % 256 == 0 else t
    my = _dev_index((lax.axis_index("x"), lax.axis_index("y"), lax.axis_index("c")))

    small_sh_shapes = [p[n].shape for n in SMALL_SHARDED]
    gathered_small = _all_gather([_pack([p[n] for n in SMALL_SHARDED])], [0], name="gather_small")[0]
    full = {}
    for n, a in zip(SMALL_SHARDED, zip(*[_unpack(gathered_small[d], small_sh_shapes) for d in range(N_DEV)])):
        full[n] = jnp.concatenate(a, axis=-1)
    big_names = ROW_SHARDED + COL_SHARDED
    seg_order = ['a', 'ag', 'z', 'x', 'dt', 'qkv', 'g']
    wg = {n: [] for n in big_names}
    w_seg = {k: [] for k in seg_order}
    others = [n for n in big_names if n != 'w_in']

    def shards(layer, names):
        return [p[n][layer].astype(BF16) for n in names]

    def use_weights(layer_names, got):
        for n, a in zip([n for _, names in layer_names for n in names], got):
            if n in ROW_SHARDED:
                wg[n].append(a.reshape(a.shape[0] * a.shape[1], a.shape[2]))
            elif n != 'w_in':
                wg[n].append(a)
            else:
                for k, seg in zip(seg_order, _w_in_segments(a)):
                    w_seg[k].append(seg)

    def gather_rider(layer_names):
        xs = [a for layer, names in layer_names for a in shards(layer, names)]
        return ('gather', xs, [0] * len(xs)) if xs else None

    def riding(layer):
        if layer == 0:
            return [(0, others)], [(1, ['w_in']), (1, others)]
        if layer + 1 < DEPTH:
            return [(layer + 1, ['w_in'])], [(layer + 1, others)]
        return [], []

    use_weights([(0, ['w_in'])], _all_gather(shards(0, ['w_in']), [0], name="gather_weights"))

    def vec(name, layer, width=None):
        a = p[name][layer].reshape(1, -1)
        if width is not None and a.shape[1] < width:
            a = jnp.pad(a, ((0, 0), (0, width - a.shape[1])))
        return a

    buckets = jnp.asarray(_rel_buckets().reshape(1, -1))
    bias = _band_bias(jnp.transpose(p['rel_table']), buckets).reshape(ATTN_KV_HEADS, ATTN_REP * ATTN_BLOCK, 2 * ATTN_BLOCK)

    saved = []
    h = h0
    for l in range(DEPTH):
        s = {'h0': h}
        u = _rowwise(lambda a, g: (_f_rms(a, g), ()), [(h, 0, D_MODEL)], [vec('norm_mix', l)], [(D_MODEL, BF16)], tr=tr, name="rms_mix")[0]
        s['u'] = u
        pr = {k: _mm(u, (w_seg[k], 'plain', l), mode='nn', name="proj_" + k) for k in seg_order}
        s['pr'] = pr
        dw_w, dw_b = full['conv_dw_w'][l], vec('conv_dw_b', l)
        ca = _conv_fwd([(pr['a'], 0), (pr['ag'], 0)], dw_w, dw_b, kk=CONV_KERNEL, pre_glu=True, post_silu=False, name="conv31")
        s['ca'] = ca
        ya_in = _rowwise(lambda a, g, b: (_f_lnsilu(a, g, b), ()), [(ca, 0, D_MODEL)], [vec('conv_ln_g', l), vec('conv_ln_b', l)],
                         [(D_MODEL, BF16)], tr=tr, name="ln_silu")[0]
        s['ya_in'] = ya_in
        xbc = _conv_fwd([(pr['x'], 0)], full['ssd_conv_w'][l], vec('ssd_conv_b', l), kk=SSD_CONV, pre_glu=False, post_silu=True, name="conv4")
        s['xbc'] = xbc
        ssd_vecs = [vec('ssd_dt_bias', l, LANES), vec('ssd_A_log', l, LANES), vec('ssd_D', l, LANES)]
        with_ssd, with_swa = riding(l)
        (y_ssd, states), got = _ssd_fwd(xbc, pr['dt'], *ssd_vecs, name="ssd", rider=gather_rider(with_ssd))
        use_weights(with_ssd, got)
        s['y_ssd'], s['states'] = y_ssd, states
        yb_in = _rowwise(lambda a, z, g: (_f_ssdgate(a, z, g), ()), [(y_ssd, 0, SSD_INNER), (pr['z'], 0, SSD_INNER)], [vec('ssd_norm_g', l)],
                         [(SSD_INNER, BF16)], tr=tr, name="ssd_gate")[0]
        s['yb_in'] = yb_in
        y_b = _mm(yb_in, (wg['w_ssd_out'], 'plain', l), mode='nn', name="ssd_out")
        att, got = _attn_fwd(pr['qkv'], bias, vec('attn_sinks', l), vec('attn_q_norm', l), vec('attn_k_norm', l), name="swa",
                             rider=gather_rider(with_swa))
        use_weights(with_swa, got)
        s['att'] = att
        y_c = _mm(att, (wg['w_attn_out'], 'plain', l), mode='nn', name="attn_out")
        y_a = _mm(ya_in, (wg['w_conv_out'], 'plain', l), mode='nn', name="conv_out")
        s['y_a'], s['y_b'], s['y_c'] = y_a, y_b, y_c
        merged = _rowwise(lambda pg, a, b, c, gb: (_f_merge(pg, a, b, c, gb), ()),
                          [(pr['g'], 0, 3 * D_MODEL), (y_a, 0, D_MODEL), (y_b, 0, D_MODEL), (y_c, 0, D_MODEL)], [full['gate_bias'][l]],
                          [(D_MODEL, BF16)], tr=tr, name="merge")[0]
        s['merged'] = merged
        h = _mm(merged, (wg['w_mix_out'], 'plain', l), mode='nn', add=h, name="mix_out")
        s['h1'] = h
        un = _rowwise(lambda a, g: (_f_rms(a, g), ()), [(h, 0, D_MODEL)], [vec('norm_xattn', l)], [(D_MODEL, BF16)], tr=tr, name="rms_xattn")[0]
        memn = _rowwise(lambda a, g: (_f_rms(a, g), ()), [(mem2, 0, D_MODEL)], [vec('norm_mem', l)], [(D_MODEL, BF16)], tr=MEM_LEN, name="rms_mem")[0]
        s['un'], s['memn'] = un, memn
        xq = _mm(un, (wg['w_xq'], 'plain', l), mode='nn', name="xq")
        kv = _mm(memn, (wg['w_xkv'], 'col', l), mode='nn', name="xkv", tn=256)
        s['xq'], s['kv'] = xq, kv
        xo = _rowwise(lambda q, kvv, qg, kg: (_f_xattn(q, kvv, qg, kg), ()), [(xq, 0, D_MODEL)], [kv, vec('xattn_q_norm', l), vec('xattn_k_norm', l)],
                      [(D_MODEL, BF16)], tr=tr, name="xattn")[0]
        s['xo'] = xo
        h = _mm(xo, (wg['w_xo'], 'plain', l), mode='nn', add=h, name="xattn_out")
        s['h2'] = h
        um = _rowwise(lambda a, g: (_f_rms(a, g), ()), [(h, 0, D_MODEL)], [vec('norm_mlp', l)], [(D_MODEL, BF16)], tr=tr, name="rms_mlp")[0]
        s['um'] = um
        up, act = _mm(um, (wg['w_mlp_up'], 'col', l), mode='nn', name="mlp_up", epi=(lambda acc: (acc,) + _f_relu2(acc), [], [F32, BF16]))
        s['up'], s['act'] = up, act
        h = _mm(act, (wg['w_mlp_down'], 'plain', l), mode='nn', add=h, name="mlp_down")
        saved.append(s)

    dh, dh_b, loss_part = _rowwise(_f_loss, [(h, 0, D_MODEL), (tgt, 0, D_MODEL)], [], [(D_MODEL, F32), (D_MODEL, BF16)], [(1, LANES)], tr=tr, name="loss")

    sg = {n: [None] * DEPTH for n in SMALL if n != 'rel_table'}
    dbias_layers = [None] * DEPTH
    recv = {}
    with_d_swa = ['w_mlp_down', 'w_mlp_up', 'w_xo', 'w_xq', 'w_xkv', 'w_mix_out', 'w_attn_out']
    left_over = []

    def by_device(g_):
        return g_.reshape(N_DEV, g_.shape[0] // N_DEV, g_.shape[1])

    for l in reversed(range(DEPTH)):
        s = saved[l]
        bg = {}
        dup = _mm(dh_b, (wg['w_mlp_down'], 'plain', l), mode='nt', name="d_act",
                  epi=(lambda acc, up_: _vjp_rows(_f_relu2, 1, 1)(up_, acc)[0], [s['up']], [BF16]))[0]
        bg['w_mlp_down'] = _mm(s['act'], dh_b, mode='tn', out_dtype=BF16, name="dw_mlp_down")
        bg['w_mlp_up'] = _mm(s['um'], dup, mode='tn', out_dtype=BF16, out_col=True, name="dw_mlp_up")
        dum = _mm(dup, (wg['w_mlp_up'], 'col', l), mode='nt', name="d_um", tk=512, tn=1024)
        dh, dh_b, dg = _rms_bwd_call(s['h2'], vec('norm_mlp', l), dum, dh, tr, "d_rms_mlp")
        sg['norm_mlp'][l] = dg
        dxo = _mm(dh_b, (wg['w_xo'], 'plain', l), mode='nt', out_dtype=BF16, name="d_xo")
        bg['w_xo'] = _mm(s['xo'], dh_b, mode='tn', out_dtype=BF16, name="dw_xo")
        qg, kg = vec('xattn_q_norm', l), vec('xattn_k_norm', l)
        dxq, dkv, dqg, dkg = _xattn_bwd_call(s['xq'], s['kv'], qg, kg, dxo, tr)
        sg['xattn_q_norm'][l], sg['xattn_k_norm'][l] = dqg, dkg
        bg['w_xq'] = _mm(s['un'], dxq, mode='tn', out_dtype=BF16, name="dw_xq")
        dun = _mm(dxq, (wg['w_xq'], 'plain', l), mode='nt', name="d_un")
        dh, dh_b, dg = _rms_bwd_call(s['h1'], vec('norm_xattn', l), dun, dh, tr, "d_rms_xattn")
        sg['norm_xattn'][l] = dg
        bg['w_xkv'] = _mm(s['memn'], dkv, mode='tn', out_dtype=BF16, out_col=True, name="dw_xkv", tn=256)
        dmemn = _mm(dkv, (wg['w_xkv'], 'col', l), mode='nt', name="d_memn", tk=256)
        _, _, dg = _rms_bwd_call(mem2, vec('norm_mem', l), dmemn, jnp.zeros_like(mem2), MEM_LEN, "d_rms_mem")
        sg['norm_mem'][l] = dg
        dmerged = _mm(dh_b, (wg['w_mix_out'], 'plain', l), mode='nt', out_dtype=BF16, name="d_merged")
        bg['w_mix_out'] = _mm(s['merged'], dh_b, mode='tn', out_dtype=BF16, name="dw_mix_out")
        pr = s['pr']
        gb = full['gate_bias'][l]
        dpg, dya, dyb, dyc, dgb = _merge_bwd_call(pr['g'], s['y_a'], s['y_b'], s['y_c'], gb, dmerged, tr)
        sg['gate_bias'][l] = dgb
        dseg = {'g': dpg}
        datt = _mm(dyc, (wg['w_attn_out'], 'plain', l), mode='nt', out_dtype=BF16, name="d_att")
        bg['w_attn_out'] = _mm(s['att'], dyc, mode='tn', out_dtype=BF16, name="dw_attn_out")
        early = [bg[n] if n in COL_SHARDED else by_device(bg[n]) for n in with_d_swa]
        (dqkv, dbias_l, dsk, dqn, dkn), got = _attn_bwd(pr['qkv'], datt, bias, vec('attn_sinks', l), vec('attn_q_norm', l), vec('attn_k_norm', l),
                                                     name="d_swa", rider=('exchange', early))
        recv.update({(n, l): r for n, r in zip(with_d_swa, got)})
        dseg['qkv'] = dqkv
        dbias_layers[l] = dbias_l
        sg['attn_sinks'][l], sg['attn_q_norm'][l], sg['attn_k_norm'][l] = dsk, dqn, dkn
        dyb_in = _mm(dyb, (wg['w_ssd_out'], 'plain', l), mode='nt', out_dtype=BF16, name="d_yb_in")
        bg['w_ssd_out'] = _mm(s['yb_in'], dyb, mode='tn', out_dtype=BF16, name="dw_ssd_out")
        ng = vec('ssd_norm_g', l)
        dy_ssd, dz, dng = _ssdgate_bwd_call(s['y_ssd'], pr['z'], ng, dyb_in, tr)
        sg['ssd_norm_g'][l] = dng
        dseg['z'] = dz
        ssd_vecs = [vec('ssd_dt_bias', l, LANES), vec('ssd_A_log', l, LANES), vec('ssd_D', l, LANES)]
        dya_in = _mm(dya, (wg['w_conv_out'], 'plain', l), mode='nt', out_dtype=BF16, name="d_ya_in")
        bg['w_conv_out'] = _mm(s['ya_in'], dya, mode='tn', out_dtype=BF16, name="dw_conv_out")
        late = [('w_ssd_out', l, by_device(bg['w_ssd_out'])), ('w_conv_out', l, by_device(bg['w_conv_out']))] + left_over
        (dxbc_act, ddt, dtb, dal, ddsk), got = _ssd_bwd(s['xbc'], pr['dt'], s['states'], dy_ssd, *ssd_vecs, name="d_ssd",
                                                        rider=('exchange', [a for _, _, a in late]))
        recv.update({(n, ll): r for (n, ll, _), r in zip(late, got)})
        dseg['dt'] = ddt
        sg['ssd_dt_bias'][l], sg['ssd_A_log'][l], sg['ssd_D'][l] = dtb[:, :SSD_HEADS], dal[:, :SSD_HEADS], ddsk[:, :SSD_HEADS]
        dxbc, dcw, dcb = _conv_bwd([(pr['x'], 0)], full['ssd_conv_w'][l], vec('ssd_conv_b', l), [(dxbc_act, 0, SSD_XBC // LANES)],
                                   kk=SSD_CONV, pre_glu=False, post_silu=True, name="d_conv4", dx_dtype=BF16)
        dseg['x'] = dxbc
        sg['ssd_conv_w'][l], sg['ssd_conv_b'][l] = dcw, dcb
        lg, lb = vec('conv_ln_g', l), vec('conv_ln_b', l)
        dca, dlg, dlb = _lnsilu_bwd_call(s['ca'], lg, lb, dya_in, tr)
        sg['conv_ln_g'][l], sg['conv_ln_b'][l] = dlg, dlb
        dseg['a'], dseg['ag'], dww, dwb = _conv_bwd([(pr['a'], 0), (pr['ag'], 0)], full['conv_dw_w'][l], vec('conv_dw_b', l),
                                                    [(dca, 0, D_MODEL // LANES)], kk=CONV_KERNEL, pre_glu=True, post_silu=False, name="d_conv31", dx_dtype=BF16)
        sg['conv_dw_w'][l], sg['conv_dw_b'][l] = dww, dwb
        du = None
        dw_parts = []
        for k in seg_order:
            du = _mm(dseg[k], (w_seg[k], 'plain', l), mode='nt', add=du, name="d_u_" + k)
            dw_parts.append(_mm(s['u'], dseg[k], mode='tn', out_dtype=BF16, name="dw_in_" + k))
        dw_in = _w_in_by_device(dw_parts)
        dh, dh_b, dg = _rms_bwd_call(s['h0'], vec('norm_mix', l), du, dh, tr, "d_rms_mix")
        sg['norm_mix'][l] = dg
        left_over = [('w_in', l, dw_in)]

    got = _exchange([a for _, _, a in left_over], name="exchange_grads")
    recv.update({(n, ll): r for (n, ll, _), r in zip(left_over, got)})
    grad_x = dh.reshape(x.shape)
    d_rel = jnp.transpose(_band_bias_bwd(jnp.stack(dbias_layers).reshape(DEPTH, ATTN_HEADS, -1), buckets))

    small_full = {'rel_table': d_rel}
    for n in SMALL:
        if n != 'rel_table':
            small_full[n] = jnp.stack(sg[n]).reshape((DEPTH,) + (full[n].shape[1:] if n in SMALL_SHARDED else p[n].shape[1:]))
    small_shapes = [(1, LANES)] + [small_full[n].shape for n in SMALL]
    packed = _pack([loss_part] + [small_full[n] for n in SMALL])
    slots = _all_gather([packed], [0], name="gather_small_grads")[0]
    rows = packed.shape[0]
    reduced = _rowwise(_f_sum_slots, [(slots, 0, LANES)], [], [(LANES, F32)], tr=_row_tile(rows, LANES, 12), name="sum_small")[0]
    red = _unpack(reduced, small_shapes)
    loss = red[0][0, 0]
    small_grad = {}
    for n, g_ in zip(SMALL, red[1:]):
        if n in SMALL_SHARDED:
            wdt = p[n].shape[-1]
            g_ = lax.dynamic_slice_in_dim(g_, my * wdt, wdt, axis=g_.ndim - 1)
        small_grad[n] = g_
    local_shapes = [p[n].shape for n in SMALL]
    pk = lambda d: _pack([d[n] for n in SMALL])
    pg_, pw_, pm_, pv_ = pk(small_grad), pk(p), pk(m), pk(v)
    srows = pg_.shape[0]
    sd, sm, sv = _rowwise(_f_adam, [(pg_, 0, LANES), (pw_, 0, LANES), (pm_, 0, LANES), (pv_, 0, LANES)], [],
                          [(LANES, F32)] * 3, tr=_row_tile(srows, LANES, 16), name="adam_small")
    out_delta = dict(zip(SMALL, _unpack(sd, local_shapes)))
    out_m = dict(zip(SMALL, _unpack(sm, local_shapes)))
    out_v = dict(zip(SMALL, _unpack(sv, local_shapes)))
    out_grad = dict(small_grad)

    per_layer = {n: [] for n in big_names}
    for l in range(DEPTH):
        for n in big_names:
            r = recv[(n, l)]
            rws, wdt = r.shape[1], r.shape[2]
            tr_w = _row_tile(rws, wdt, 24)
            outs = _rowwise(_f_adam_slots, [(r, 0, wdt), (p[n], 0, wdt, l), (m[n], 0, wdt, l), (v[n], 0, wdt, l)], [],
                            [(wdt, F32)] * 4, tr=tr_w, name="adam_" + n)
            per_layer[n].append(outs)
    for n in big_names:
        for k, dst in enumerate((out_grad, out_delta, out_m, out_v)):
            dst[n] = jnp.stack([per_layer[n][l][k] for l in range(DEPTH)])
    return (loss, grad_x, *[out_grad[n] for n in WEIGHTS], *[out_delta[n] for n in WEIGHTS],
            *[out_m[n] for n in WEIGHTS], *[out_v[n] for n in WEIGHTS])


def _rms_bwd_call(h, g, du, dres, tr, name):
    return _rowwise(lambda a, d, r, gg: _f_rms_bwd(a, gg, d, r), [(h, 0, D_MODEL), (du, 0, D_MODEL), (dres, 0, D_MODEL)], [g],
                    [(D_MODEL, F32), (D_MODEL, BF16)], [g.shape], tr=tr, name=name)


def _xattn_bwd_call(xq, kv, qg, kg, dxo, tr):
    def f(q, d, kvv, qgv, kgv):
        return _vjp_rows(lambda a, b, c, e: _f_xattn(a, b, c, e), 4, 1)(q, kvv, qgv, kgv, d)
    return _rowwise(f, [(xq, 0, D_MODEL), (dxo, 0, D_MODEL)], [kv, qg, kg], [(D_MODEL, BF16)], [kv.shape, qg.shape, kg.shape], tr=tr, name="d_xattn")


def _merge_bwd_call(pg, ya, yb, yc, gb, dmerged, tr):
    def f(a, b, c, e, d, gbv):
        return _vjp_rows(_f_merge, 5, 4)(a, b, c, e, gbv, d)
    return _rowwise(f, [(pg, 0, 3 * D_MODEL), (ya, 0, D_MODEL), (yb, 0, D_MODEL), (yc, 0, D_MODEL), (dmerged, 0, D_MODEL)], [gb],
                    [(3 * D_MODEL, BF16)] + [(D_MODEL, BF16)] * 3, [gb.shape], tr=tr, name="d_merge")


def _ssdgate_bwd_call(y, z, ng, dy, tr):
    def f(a, b, d, g):
        return _vjp_rows(_f_ssdgate, 3, 2)(a, b, g, d)
    return _rowwise(f, [(y, 0, SSD_INNER), (z, 0, SSD_INNER), (dy, 0, SSD_INNER)], [ng], [(SSD_INNER, F32), (SSD_INNER, BF16)], [ng.shape], tr=tr, name="d_ssd_gate")


def _lnsilu_bwd_call(ca, lg, lb, dy, tr):
    def f(a, d, g, b):
        return _vjp_rows(_f_lnsilu, 3, 1)(a, g, b, d)
    return _rowwise(f, [(ca, 0, D_MODEL), (dy, 0, D_MODEL)], [lg, lb], [(D_MODEL, F32)], [lg.shape, lb.shape], tr=tr, name="d_ln_silu")


def kernel(x, mem, rel_table, norm_mix, w_in, gate_bias, conv_dw_w, conv_dw_b, conv_ln_g, conv_ln_b, w_conv_out, ssd_conv_w, ssd_conv_b, ssd_dt_bias, ssd_A_log, ssd_D, ssd_norm_g, w_ssd_out, attn_q_norm, attn_k_norm, attn_sinks, w_attn_out, w_mix_out, norm_xattn, norm_mem, w_xq, w_xkv, xattn_q_norm, xattn_k_norm, w_xo, norm_mlp, w_mlp_up, w_mlp_down, loss_target, m_rel_table, m_norm_mix, m_w_in, m_gate_bias, m_conv_dw_w, m_conv_dw_b, m_conv_ln_g, m_conv_ln_b, m_w_conv_out, m_ssd_conv_w, m_ssd_conv_b, m_ssd_dt_bias, m_ssd_A_log, m_ssd_D, m_ssd_norm_g, m_w_ssd_out, m_attn_q_norm, m_attn_k_norm, m_attn_sinks, m_w_attn_out, m_w_mix_out, m_norm_xattn, m_norm_mem, m_w_xq, m_w_xkv, m_xattn_q_norm, m_xattn_k_norm, m_w_xo, m_norm_mlp, m_w_mlp_up, m_w_mlp_down, v_rel_table, v_norm_mix, v_w_in, v_gate_bias, v_conv_dw_w, v_conv_dw_b, v_conv_ln_g, v_conv_ln_b, v_w_conv_out, v_ssd_conv_w, v_ssd_conv_b, v_ssd_dt_bias, v_ssd_A_log, v_ssd_D, v_ssd_norm_g, v_w_ssd_out, v_attn_q_norm, v_attn_k_norm, v_attn_sinks, v_w_attn_out, v_w_mix_out, v_norm_xattn, v_norm_mem, v_w_xq, v_w_xkv, v_xattn_q_norm, v_xattn_k_norm, v_w_xo, v_norm_mlp, v_w_mlp_up, v_w_mlp_down):
    args = locals()
    p = {n: args[n] for n in WEIGHTS}
    m = {n: args["m_" + n] for n in WEIGHTS}
    v = {n: args["v_" + n] for n in WEIGHTS}
    return _step(p, m, v, x, mem, loss_target)
```
